```python
import math, functools
import jax, jax.numpy as jnp
from jax import lax
import numpy as np

D_MODEL = 2048
BATCH = 4
SEQ = 4096
DEPTH = 4
DEC_BATCH = 16
DEC_SEQ = 32
PAST_LEN = 1024

CHUNK = 64
N_META = 16
N_MIXERS = 2
N_ATTN_LAYERS = (DEPTH + N_MIXERS - 1) // N_MIXERS
N_LRU_LAYERS = DEPTH // N_MIXERS
HEAD_DIM = 64
N_HEADS = D_MODEL // HEAD_DIM
N_KV_HEADS = N_HEADS // 8
GROUP = N_HEADS // N_KV_HEADS
Q_W = N_HEADS * HEAD_DIM
KV_W = N_KV_HEADS * HEAD_DIM
WINDOW = 128
BAND_CHUNKS = WINDOW // CHUNK
ROT_DIM = HEAD_DIM // 4
ROPE_THETA = 500000.0
ATTN_SCALE = HEAD_DIM ** -0.5
D_RNN = D_MODEL
N_LRU_BLOCKS = 8
LRU_BLOCK_W = D_RNN // N_LRU_BLOCKS
CONV_W = 4
LRU_C = 8.0
N_EXPERTS = 64
N_GROUPS = 8
TOPK_GROUPS = 4
TOPK = 8
D_EXPERT = 512
D_SHARED = 512
ROUTED_SCALE = 2.5
EXPERT_BLOCK = 128
ALPHA = (2 * DEPTH) ** 0.25
BETA = (8 * DEPTH) ** -0.25
LN_EPS = 1e-5
NEG_INF = -1e30

kernel_name = 'hybrid_swa_sink_rglru_moe_stream_step'


def layer_norm(x, g, b):
    xf = x.astype(jnp.float32)
    mu = jnp.mean(xf, -1, keepdims=True)
    var = jnp.mean(jnp.square(xf - mu), -1, keepdims=True)
    y = (xf - mu) * lax.rsqrt(var + LN_EPS) * g.astype(jnp.float32) + b.astype(jnp.float32)
    return y.astype(x.dtype)


def rope(x, pos):
    half = ROT_DIM // 2
    freqs = ROPE_THETA ** (-jnp.arange(0, ROT_DIM, 2, dtype=jnp.float32) / ROT_DIM)
    ang = pos[:, None] * freqs[None, :]
    cos = jnp.cos(ang)[:, None, :]
    sin = jnp.sin(ang)[:, None, :]
    xf = x.astype(jnp.float32)
    x1 = xf[..., :half]
    x2 = xf[..., half:ROT_DIM]
    out = jnp.concatenate([x1 * cos - x2 * sin, x2 * cos + x1 * sin, xf[..., ROT_DIM:]], -1)
    return out.astype(x.dtype)


def softmax_with_sink(s, sink):
    m = jnp.maximum(jnp.max(s, -1, keepdims=True), sink)
    p = jnp.exp(s - m)
    return p / (jnp.sum(p, -1, keepdims=True) + jnp.exp(sink - m))


def qkv_split(x, w_in, pos):
    B, L, _ = x.shape
    qkv = x @ w_in
    q = rope(qkv[..., :Q_W].reshape(B, L, N_HEADS, HEAD_DIM), pos)
    k = rope(qkv[..., Q_W:Q_W + KV_W].reshape(B, L, N_KV_HEADS, HEAD_DIM), pos)
    v = qkv[..., Q_W + KV_W:].reshape(B, L, N_KV_HEADS, HEAD_DIM)
    return q, k, v


def attn_prompt(x, w_in, w_out, sink, pos):
    B, L, _ = x.shape
    q, k, v = qkv_split(x, w_in, pos)
    sink_b = sink.astype(jnp.float32).reshape(N_KV_HEADS, GROUP, 1, 1)
    qm, qr = q[:, :N_META], q[:, N_META:]
    km, kr = k[:, :N_META], k[:, N_META:]
    vm, vr = v[:, :N_META], v[:, N_META:]
    n = L - N_META
    nc = n // CHUNK
    s_mm = jnp.einsum('bmhgd,bjhd->bhgmj', qm.reshape(B, N_META, N_KV_HEADS, GROUP, HEAD_DIM), km)
    p_mm = softmax_with_sink(s_mm.astype(jnp.float32) * ATTN_SCALE, sink_b).astype(x.dtype)
    o_m = jnp.einsum('bhgmj,bjhd->bmhgd', p_mm, vm).reshape(B, N_META, Q_W)

    def bands(t):
        tp = jnp.pad(t, ((0, 0), (BAND_CHUNKS * CHUNK, 0), (0, 0), (0, 0)))
        tp = tp.reshape(B, nc + BAND_CHUNKS, CHUNK, N_KV_HEADS, HEAD_DIM)
        return jnp.concatenate([tp[:, j:j + nc] for j in range(BAND_CHUNKS + 1)], axis=2)

    kb, vb = bands(kr), bands(vr)
    qc = qr.reshape(B, nc, CHUNK, N_KV_HEADS, GROUP, HEAD_DIM)
    s_band = jnp.einsum('bnchgd,bnjhd->bnhgcj', qc, kb).astype(jnp.float32) * ATTN_SCALE
    valid = (jnp.arange(nc)[:, None] + jnp.arange(BAND_CHUNKS + 1)[None, :] - BAND_CHUNKS) >= 0
    valid = jnp.repeat(valid, CHUNK, axis=1)
    s_band = jnp.where(valid[None, :, None, None, None, :], s_band, NEG_INF)
    s_meta = jnp.einsum('bnchgd,bjhd->bnhgcj', qc, km).astype(jnp.float32) * ATTN_SCALE
    p = softmax_with_sink(jnp.concatenate([s_meta, s_band], -1), sink_b).astype(x.dtype)
    o_r = (jnp.einsum('bnhgcj,bjhd->bnchgd', p[..., :N_META], vm)
           + jnp.einsum('bnhgcj,bnjhd->bnchgd', p[..., N_META:], vb)).reshape(B, n, Q_W)
    o = jnp.concatenate([o_m, o_r], 1)
    return o @ w_out, km, vm, kr[:, -WINDOW:], vr[:, -WINDOW:]


def attn_sample(x, meta_k, meta_v, win_k, win_v, w_in, w_out, sink, pos):
    B, S, _ = x.shape
    q, k, v = qkv_split(x, w_in, pos)
    sink_b = sink.astype(jnp.float32).reshape(N_KV_HEADS, GROUP, 1, 1)
    keys = jnp.concatenate([meta_k.astype(x.dtype), win_k.astype(x.dtype), k], 1)
    vals = jnp.concatenate([meta_v.astype(x.dtype), win_v.astype(x.dtype), v], 1)
    s = jnp.einsum('bshgd,bjhd->bhgsj', q.reshape(B, S, N_KV_HEADS, GROUP, HEAD_DIM), keys)
    p = softmax_with_sink(s.astype(jnp.float32) * ATTN_SCALE, sink_b).astype(x.dtype)
    o = jnp.einsum('bhgsj,bjhd->bshgd', p, vals).reshape(B, S, Q_W)
    return o @ w_out, k, v


def _lin_combine(e1, e2):
    a1, b1 = e1
    a2, b2 = e2
    return a1 * a2, a2 * b1 + b2


def rglru_mixer(x, conv_prev, h_prev, w_in, conv_w, conv_b, ga_w, ga_b, gx_w, gx_b, lam, w_out):
    B, L, _ = x.shape
    u = x @ w_in
    xb, gb = u[..., :D_RNN], u[..., D_RNN:]
    xcat = jnp.concatenate([conv_prev.astype(x.dtype), xb], 1)
    xc = conv_b
    for tap in range(CONV_W):
        xc = xc + xcat[:, tap:tap + L] * conv_w[tap]
    xg = xc.reshape(B, L, N_LRU_BLOCKS, LRU_BLOCK_W)
    r = jax.nn.sigmoid(jnp.einsum('blnd,nde->blne', xg, ga_w) + ga_b.reshape(N_LRU_BLOCKS, LRU_BLOCK_W))
    i = jax.nn.sigmoid(jnp.einsum('blnd,nde->blne', xg, gx_w) + gx_b.reshape(N_LRU_BLOCKS, LRU_BLOCK_W))
    r = r.reshape(B, L, D_RNN).astype(jnp.float32)
    i = i.reshape(B, L, D_RNN).astype(jnp.float32)
    log_a = -LRU_C * r * jax.nn.softplus(-lam.astype(jnp.float32))
    a = jnp.exp(log_a)
    bt = jnp.sqrt(-jnp.expm1(2.0 * log_a)) * (i * xc.astype(jnp.float32))
    bt = bt.at[:, 0].add(a[:, 0] * h_prev.astype(jnp.float32))
    _, h = lax.associative_scan(_lin_combine, (a, bt), axis=1)
    y = (h.astype(x.dtype) * jax.nn.gelu(gb)) @ w_out
    return y, xcat[:, -(CONV_W - 1):], h[:, -1].astype(x.dtype)


def swiglu(x, w1, w3, w2):
    return (jax.nn.silu(x @ w1) * (x @ w3)) @ w2


def routed_experts(x, eidx, gate, w1, w3, w2):
    T, D = x.shape
    TK = T * TOPK
    flat_e = eidx.reshape(TK).astype(jnp.int32)
    order = jnp.argsort(flat_e).astype(jnp.int32)
    e_sorted = flat_e[order]
    tok_sorted = order // TOPK
    g_sorted = gate.reshape(TK)[order].astype(x.dtype)
    counts = jnp.bincount(flat_e, length=N_EXPERTS).astype(jnp.int32)
    padded = (counts + EXPERT_BLOCK - 1) // EXPERT_BLOCK * EXPERT_BLOCK
    pad_end = jnp.cumsum(padded)
    pad_start = pad_end - padded
    grp_start = jnp.cumsum(counts) - counts
    dest = pad_start[e_sorted] + jnp.arange(TK, dtype=jnp.int32) - grp_start[e_sorted]
    n_blocks = -(-TK // EXPERT_BLOCK) + N_EXPERTS
    n_rows = n_blocks * EXPERT_BLOCK
    row_tok = jnp.full((n_rows,), T, jnp.int32).at[dest].set(tok_sorted)
    row_gate = jnp.zeros((n_rows,), x.dtype).at[dest].set(g_sorted)
    block_exp = jnp.searchsorted(pad_end, jnp.arange(n_blocks, dtype=jnp.int32) * EXPERT_BLOCK, side='right')
    block_exp = jnp.minimum(block_exp, N_EXPERTS - 1)
    x_pad = jnp.concatenate([x, jnp.zeros((1, D), x.dtype)], 0)

    def block_fn(args):
        rows, g, e = args
        xb = x_pad[rows]
        return swiglu(xb, w1[e], w3[e], w2[e]) * g[:, None]

    yb = lax.map(block_fn, (row_tok.reshape(n_blocks, EXPERT_BLOCK),
                            row_gate.reshape(n_blocks, EXPERT_BLOCK), block_exp))
    out = jnp.zeros((T + 1, D), x.dtype).at[row_tok].add(yb.reshape(n_rows, D))
    return out[:T]


def moe(x, w_router, router_bias, w1, w3, w2, ws1, ws3, ws2):
    T = x.shape[0]
    scores = jax.nn.sigmoid((x @ w_router).astype(jnp.float32))
    biased = scores + router_bias.astype(jnp.float32)
    grp = biased.reshape(T, N_GROUPS, N_EXPERTS // N_GROUPS)
    grp_score = jnp.sum(lax.top_k(grp, 2)[0], -1)
    _, gidx = lax.top_k(grp_score, TOPK_GROUPS)
    gsel = jnp.any(gidx[..., None] == jnp.arange(N_GROUPS)[None, None, :], axis=1)
    emask = jnp.repeat(gsel, N_EXPERTS // N_GROUPS, axis=-1)
    _, eidx = lax.top_k(jnp.where(emask, biased, NEG_INF), TOPK)
    g = jnp.take_along_axis(scores, eidx, axis=-1)
    g = g / (jnp.sum(g, -1, keepdims=True) + 1e-20) * ROUTED_SCALE
    return routed_experts(x, eidx, g, w1, w3, w2) + swiglu(x, ws1, ws3, ws2)


def setup_inputs(seed: int = 0) -> dict:
    key = jax.random.key(seed)
    ks = iter(jax.random.split(key, 40))

    def nrm(shape, scale=1.0):
        return jax.random.normal(next(ks), shape, jnp.float32) * scale

    win = min(WINDOW, PAST_LEN)
    u = jax.random.uniform(next(ks), (N_LRU_LAYERS, D_RNN), jnp.float32, minval=0.9, maxval=0.999)
    s = u ** (1.0 / LRU_C)
    lam = jnp.log(s) - jnp.log1p(-s)
    return {
        'x_prompt': nrm((BATCH, SEQ, D_MODEL)),
        'x_sample': nrm((DEC_BATCH, DEC_SEQ, D_MODEL)),
        'cache_meta_k': nrm((N_ATTN_LAYERS, DEC_BATCH, N_META, N_KV_HEADS, HEAD_DIM)),
        'cache_meta_v': nrm((N_ATTN_LAYERS, DEC_BATCH, N_META, N_KV_HEADS, HEAD_DIM)),
        'cache_win_k': nrm((N_ATTN_LAYERS, DEC_BATCH, win, N_KV_HEADS, HEAD_DIM)),
        'cache_win_v': nrm((N_ATTN_LAYERS, DEC_BATCH, win, N_KV_HEADS, HEAD_DIM)),
        'state_conv': nrm((N_LRU_LAYERS, DEC_BATCH, CONV_W - 1, D_RNN)),
        'state_h': nrm((N_LRU_LAYERS, DEC_BATCH, D_RNN), 0.5),
        'meta_tokens': nrm((N_META, D_MODEL)),
        'ln_g': 1.0 + nrm((DEPTH, 2, D_MODEL), 0.02),
        'ln_b': nrm((DEPTH, 2, D_MODEL), 0.02),
        'attn_w_in': nrm((N_ATTN_LAYERS, D_MODEL, Q_W + 2 * KV_W), D_MODEL ** -0.5),
        'attn_w_out': nrm((N_ATTN_LAYERS, Q_W, D_MODEL), Q_W ** -0.5 * BETA),
        'attn_sink': nrm((N_ATTN_LAYERS, N_HEADS)),
        'lru_w_in': nrm((N_LRU_LAYERS, D_MODEL, 2 * D_RNN), D_MODEL ** -0.5),
        'lru_conv_w': nrm((N_LRU_LAYERS, CONV_W, D_RNN), CONV_W ** -0.5),
        'lru_conv_b': nrm((N_LRU_LAYERS, D_RNN), 0.02),
        'lru_gate_a_w': nrm((N_LRU_LAYERS, N_LRU_BLOCKS, LRU_BLOCK_W, LRU_BLOCK_W), LRU_BLOCK_W ** -0.5),
        'lru_gate_a_b': nrm((N_LRU_LAYERS, D_RNN), 0.02),
        'lru_gate_x_w': nrm((N_LRU_LAYERS, N_LRU_BLOCKS, LRU_BLOCK_W, LRU_BLOCK_W), LRU_BLOCK_W ** -0.5),
        'lru_gate_x_b': nrm((N_LRU_LAYERS, D_RNN), 0.02),
        'lru_lambda': lam,
        'lru_w_out': nrm((N_LRU_LAYERS, D_RNN, D_MODEL), D_RNN ** -0.5 * BETA),
        'moe_router_w': nrm((DEPTH, D_MODEL, N_EXPERTS), D_MODEL ** -0.5),
        'moe_router_bias': nrm((DEPTH, N_EXPERTS), 0.01),
        'moe_w1': nrm((DEPTH, N_EXPERTS, D_MODEL, D_EXPERT), D_MODEL ** -0.5),
        'moe_w3': nrm((DEPTH, N_EXPERTS, D_MODEL, D_EXPERT), D_MODEL ** -0.5),
        'moe_w2': nrm((DEPTH, N_EXPERTS, D_EXPERT, D_MODEL), D_EXPERT ** -0.5 * BETA),
        'moe_shared_w1': nrm((DEPTH, D_MODEL, D_SHARED), D_MODEL ** -0.5),
        'moe_shared_w3': nrm((DEPTH, D_MODEL, D_SHARED), D_MODEL ** -0.5),
        'moe_shared_w2': nrm((DEPTH, D_SHARED, D_MODEL), D_SHARED ** -0.5 * BETA),
    }


def reference(x_prompt, x_sample, cache_meta_k, cache_meta_v, cache_win_k, cache_win_v,
              state_conv, state_h, meta_tokens, ln_g, ln_b, attn_w_in, attn_w_out, attn_sink,
              lru_w_in, lru_conv_w, lru_conv_b, lru_gate_a_w, lru_gate_a_b, lru_gate_x_w,
              lru_gate_x_b, lru_lambda, lru_w_out, moe_router_w, moe_router_bias, moe_w1, moe_w3,
              moe_w2, moe_shared_w1, moe_shared_w3, moe_shared_w2):
    B = x_prompt.shape[0]
    meta = jnp.broadcast_to(meta_tokens.astype(x_prompt.dtype)[None], (B, N_META, D_MODEL))
    xp = jnp.concatenate([meta, x_prompt], 1)
    xs = x_sample
    lp = xp.shape[1]
    pos_p = jnp.arange(lp, dtype=jnp.float32)
    pos_s = PAST_LEN + N_META + jnp.arange(xs.shape[1], dtype=jnp.float32)
    n_p = B * lp

    mk_p, mv_p, wk_p, wv_p, k_s, v_s = [], [], [], [], [], []
    conv_p, h_p, conv_s, h_s = [], [], [], []
    for l in range(DEPTH):
        idx = l // N_MIXERS
        if l % N_MIXERS == 0:
            yp, mk, mv, wk, wv = attn_prompt(xp, attn_w_in[idx], attn_w_out[idx], attn_sink[idx], pos_p)
            ys, nk, nv = attn_sample(xs, cache_meta_k[idx], cache_meta_v[idx], cache_win_k[idx],
                                     cache_win_v[idx], attn_w_in[idx], attn_w_out[idx],
                                     attn_sink[idx], pos_s)
            mk_p.append(mk); mv_p.append(mv); wk_p.append(wk); wv_p.append(wv)
            k_s.append(nk); v_s.append(nv)
        else:
            lru_args = (lru_w_in[idx], lru_conv_w[idx], lru_conv_b[idx], lru_gate_a_w[idx],
                        lru_gate_a_b[idx], lru_gate_x_w[idx], lru_gate_x_b[idx], lru_lambda[idx],
                        lru_w_out[idx])
            zc = jnp.zeros((B, CONV_W - 1, D_RNN), xp.dtype)
            zh = jnp.zeros((B, D_RNN), xp.dtype)
            yp, cp, hp = rglru_mixer(xp, zc, zh, *lru_args)
            ys, cs, hs = rglru_mixer(xs, state_conv[idx], state_h[idx], *lru_args)
            conv_p.append(cp); h_p.append(hp); conv_s.append(cs); h_s.append(hs)
        xp = layer_norm(ALPHA * xp + yp, ln_g[l, 0], ln_b[l, 0])
        xs = layer_norm(ALPHA * xs + ys, ln_g[l, 0], ln_b[l, 0])
        tokens = jnp.concatenate([xp.reshape(-1, D_MODEL), xs.reshape(-1, D_MODEL)], 0)
        f = moe(tokens, moe_router_w[l], moe_router_bias[l], moe_w1[l], moe_w3[l], moe_w2[l],
                moe_shared_w1[l], moe_shared_w3[l], moe_shared_w2[l])
        xp = layer_norm(ALPHA * xp + f[:n_p].reshape(xp.shape), ln_g[l, 1], ln_b[l, 1])
        xs = layer_norm(ALPHA * xs + f[n_p:].reshape(xs.shape), ln_g[l, 1], ln_b[l, 1])

    y_prompt = xp[:, N_META:]
    y_sample = xs
    new_meta_k_p = jnp.stack(mk_p)
    new_meta_v_p = jnp.stack(mv_p)
    new_win_k_p = jnp.stack(wk_p)
    new_win_v_p = jnp.stack(wv_p)
    new_k_s = jnp.stack(k_s)
    new_v_s = jnp.stack(v_s)
    new_conv_p = jnp.stack(conv_p)
    new_h_p = jnp.stack(h_p)
    new_conv_s = jnp.stack(conv_s)
    new_h_s = jnp.stack(h_s)
    return (y_prompt, y_sample, new_meta_k_p, new_meta_v_p, new_win_k_p, new_win_v_p,
            new_k_s, new_v_s, new_conv_p, new_h_p, new_conv_s, new_h_s)
```

```python
import functools
import math

import numpy as np
import jax
import jax.numpy as jnp
from jax import lax
from jax.experimental import pallas as pl
from jax.experimental.pallas import tpu as pltpu

F32 = jnp.float32
BF16 = jnp.bfloat16

PAST_LEN = 1024
CHUNK = 64
WINDOW = 128
HEAD_DIM = 64
GROUP = 8
ROT_DIM = HEAD_DIM // 4
ROPE_THETA = 500000.0
ATTN_SCALE = HEAD_DIM ** -0.5
N_LRU_BLOCKS = 8
CONV_W = 4
LRU_C = 8.0
N_GROUPS = 8
TOPK_GROUPS = 4
TOPK = 8
ROUTED_SCALE = 2.5
LN_EPS = 1e-5
NEG_INF = -1e30

LANES = 128
SUBLANES = 8
QBLK = 2 * CHUNK
META_BLK = 128
ROW_TILE = 512
LN_TILE = 256
LRU_TILE = 256
SCAN_LANES = 512
MOE_BLK = 256
VMEM_LIMIT = 56 * 1024 * 1024


def _cparams(*sem):
    return pltpu.CompilerParams(dimension_semantics=sem, vmem_limit_bytes=VMEM_LIMIT)


def _mm_kernel(x_ref, w_ref, o_ref):
    o_ref[...] = jnp.dot(x_ref[...], w_ref[...], preferred_element_type=F32)


def _matmul(x, w, tn, name):
    m, k = x.shape
    n = w.shape[1]
    tm = ROW_TILE
    return pl.pallas_call(
        _mm_kernel,
        grid=(n // tn, m // tm),
        in_specs=[pl.BlockSpec((tm, k), lambda j, i: (i, 0)),
                  pl.BlockSpec((k, tn), lambda j, i: (0, j))],
        out_specs=pl.BlockSpec((tm, tn), lambda j, i: (i, j)),
        out_shape=jax.ShapeDtypeStruct((m, n), F32),
        compiler_params=_cparams("parallel", "parallel"),
        name=name,
    )(x, w)


def _qkv_kernel(x_ref, w_ref, c_ref, s1_ref, s2_ref, o_ref, *, tn, rope_cols):
    j = pl.program_id(0)
    acc = jnp.dot(x_ref[...], w_ref[...], preferred_element_type=F32)
    c, s1, s2 = c_ref[...], s1_ref[...], s2_ref[...]
    lane = lax.broadcasted_iota(jnp.int32, (acc.shape[0], LANES), 1)
    for g in range(tn // LANES):
        a = acc[:, g * LANES:(g + 1) * LANES]
        lo = pltpu.roll(a, ROT_DIM // 2, 1)
        hi = pltpu.roll(a, LANES - ROT_DIM // 2, 1)
        roped = a * c + lo * s1 + hi * s2
        col = j * tn + g * LANES + lane
        o_ref[:, g * LANES:(g + 1) * LANES] = jnp.where(col < rope_cols, roped, a)


def _qkv_proj(xb, w, cos_t, sin_lo_t, sin_hi_t, rope_cols):
    m, k = xb.shape
    n = w.shape[1]
    tm = ROW_TILE
    tn = 512 if n % 512 == 0 else 256
    assert n % tn == 0 and m % tm == 0
    tab = pl.BlockSpec((tm, LANES), lambda j, i: (i, 0))
    return pl.pallas_call(
        functools.partial(_qkv_kernel, tn=tn, rope_cols=rope_cols),
        grid=(n // tn, m // tm),
        in_specs=[pl.BlockSpec((tm, k), lambda j, i: (i, 0)),
                  pl.BlockSpec((k, tn), lambda j, i: (0, j)),
                  tab, tab, tab],
        out_specs=pl.BlockSpec((tm, tn), lambda j, i: (i, j)),
        out_shape=jax.ShapeDtypeStruct((m, n), F32),
        compiler_params=_cparams("parallel", "parallel"),
        name="qkv_rope",
    )(xb, w, cos_t, sin_lo_t, sin_hi_t)


def _layer_norm_rows(z, g, b):
    mu = jnp.mean(z, -1, keepdims=True)
    d = z - mu
    var = jnp.mean(d * d, -1, keepdims=True)
    return d * lax.rsqrt(var + LN_EPS) * g + b


def _proj_ln_kernel(a_ref, w_ref, x_ref, g_ref, b_ref, wrh_ref, wrl_ref, o_ref, ob_ref, lg_ref, *, alpha):
    acc = jnp.dot(a_ref[...], w_ref[...], preferred_element_type=F32)
    y = _layer_norm_rows(alpha * x_ref[...] + acc, g_ref[...], b_ref[...])
    o_ref[...] = y
    yb = y.astype(BF16)
    ob_ref[...] = yb
    ylo = (y - yb.astype(F32)).astype(BF16)
    wrh = wrh_ref[...]
    lg_ref[...] = (jnp.dot(yb, wrh, preferred_element_type=F32)
                   + jnp.dot(ylo, wrh, preferred_element_type=F32)
                   + jnp.dot(yb, wrl_ref[...], preferred_element_type=F32))


def _proj_ln(a, w, x, g, b, wr_hi, wr_lo, alpha):
    m, k = a.shape
    d = w.shape[1]
    ne = wr_hi.shape[1]
    tm = LN_TILE
    row = lambda i: (i, 0)
    fix = lambda i: (0, 0)
    return pl.pallas_call(
        functools.partial(_proj_ln_kernel, alpha=alpha),
        grid=(m // tm,),
        in_specs=[pl.BlockSpec((tm, k), row), pl.BlockSpec((k, d), fix), pl.BlockSpec((tm, d), row),
                  pl.BlockSpec((1, d), fix), pl.BlockSpec((1, d), fix),
                  pl.BlockSpec((d, ne), fix), pl.BlockSpec((d, ne), fix)],
        out_specs=[pl.BlockSpec((tm, d), row), pl.BlockSpec((tm, d), row), pl.BlockSpec((tm, ne), row)],
        out_shape=[jax.ShapeDtypeStruct((m, d), F32), jax.ShapeDtypeStruct((m, d), BF16),
                   jax.ShapeDtypeStruct((m, ne), F32)],
        compiler_params=_cparams("parallel"),
        name="proj_ln_router",
    )(a, w, x, g, b, wr_hi, wr_lo)


def _silu(x):
    return x * (1.0 / (1.0 + jnp.exp(-x)))


def _shared_ln_kernel(xb_ref, x_ref, r_ref, w13_ref, w2_ref, g_ref, b_ref, o_ref, ob_ref, *, alpha, dh):
    u = jnp.dot(xb_ref[...], w13_ref[...], preferred_element_type=F32)
    hs = (_silu(u[:, :dh]) * u[:, dh:]).astype(BF16)
    sh = jnp.dot(hs, w2_ref[...], preferred_element_type=F32)
    y = _layer_norm_rows(alpha * x_ref[...] + (r_ref[...] + sh), g_ref[...], b_ref[...])
    o_ref[...] = y
    ob_ref[...] = y.astype(BF16)


def _shared_ln(xb, x, routed, w13, w2, g, b, alpha):
    m, d = x.shape
    dh = w2.shape[0]
    tm = LN_TILE
    row = lambda i: (i, 0)
    fix = lambda i: (0, 0)
    return pl.pallas_call(
        functools.partial(_shared_ln_kernel, alpha=alpha, dh=dh),
        grid=(m // tm,),
        in_specs=[pl.BlockSpec((tm, d), row), pl.BlockSpec((tm, d), row), pl.BlockSpec((tm, d), row),
                  pl.BlockSpec((d, 2 * dh), fix), pl.BlockSpec((dh, d), fix),
                  pl.BlockSpec((1, d), fix), pl.BlockSpec((1, d), fix)],
        out_specs=[pl.BlockSpec((tm, d), row), pl.BlockSpec((tm, d), row)],
        out_shape=[jax.ShapeDtypeStruct((m, d), F32), jax.ShapeDtypeStruct((m, d), BF16)],
        compiler_params=_cparams("parallel"),
        name="shared_ln",
    )(xb, x, routed, w13, w2, g, b)


def _attn_kernel(q_ref, k0_ref, k1_ref, k2_ref, v0_ref, v1_ref, v2_ref, bias_ref, sink_ref, o_ref, ot_ref,
                 *, n_kv):
    hd = HEAD_DIM
    qt = (q_ref[...] * ATTN_SCALE).T.astype(BF16)
    kb = jnp.concatenate([k0_ref[...], k1_ref[...], k2_ref[...]], 0).astype(BF16)
    vt = jnp.concatenate([v0_ref[...], v1_ref[...], v2_ref[...]], 0).T.astype(BF16)
    bias = bias_ref[0]
    bias = jnp.concatenate([bias] * GROUP, axis=1)
    for h in range(n_kv):
        kh = kb[:, h * hd:(h + 1) * hd]
        qth = jnp.concatenate(
            [qt[(h * GROUP + g) * hd:(h * GROUP + g + 1) * hd, :] for g in range(GROUP)], axis=1)
        s = jnp.dot(kh, qth, preferred_element_type=F32) + bias
        sink = sink_ref[h]
        m = jnp.maximum(jnp.max(s, axis=0, keepdims=True), sink)
        p = jnp.exp(s - m)
        den = jnp.sum(p, axis=0, keepdims=True) + jnp.exp(sink - m)
        ot = jnp.dot(vt[h * hd:(h + 1) * hd, :], p.astype(BF16), preferred_element_type=F32)
        ot = ot * (1.0 / den)
        for g in range(GROUP):
            r0 = (h * GROUP + g) * hd
            ot_ref[r0:r0 + hd, :] = ot[:, g * QBLK:(g + 1) * QBLK]
    o_ref[...] = ot_ref[...].T.astype(BF16)


def _attention(q_arr, k_arr, n_steps, q_map, kv_maps, bias, bias_map, sink_rows, out_rows, out_map,
               q_w, kv_w, k_col, v_col, name):
    n_kv = kv_w // HEAD_DIM
    kspecs = [pl.BlockSpec((QBLK, kv_w), (lambda i, f=f: (f(i), k_col))) for f in kv_maps]
    vspecs = [pl.BlockSpec((QBLK, kv_w), (lambda i, f=f: (f(i), v_col))) for f in kv_maps]
    nk = 3 * QBLK
    return pl.pallas_call(
        functools.partial(_attn_kernel, n_kv=n_kv),
        grid=(n_steps,),
        in_specs=[pl.BlockSpec((QBLK, q_w), lambda i: (q_map(i), 0))] + kspecs + vspecs + [
            pl.BlockSpec((1, nk, QBLK), lambda i: (bias_map(i), 0, 0)),
            pl.BlockSpec((n_kv, 1, GROUP * QBLK), lambda i: (0, 0, 0))],
        out_specs=pl.BlockSpec((QBLK, q_w), lambda i: (out_map(i), 0)),
        out_shape=jax.ShapeDtypeStruct((out_rows, q_w), BF16),
        scratch_shapes=[pltpu.VMEM((q_w, QBLK), F32)],
        compiler_params=_cparams("parallel"),
        name=name,
    )(q_arr, k_arr, k_arr, k_arr, k_arr, k_arr, k_arr, bias, sink_rows)


def _gelu_tanh(x):
    c = math.sqrt(2.0 / math.pi)
    return x * (0.5 * (1.0 + jnp.tanh(c * (x + 0.044715 * (x * x * x)))))


def _sigmoid(x):
    return 1.0 / (1.0 + jnp.exp(-x))


def _lru_kernel(xb_ref, gb_ref, cprev_ref, hprev_ref, cw_ref, cb_ref, gaw_ref, gab_ref, gxw_ref, gxb_ref,
                sp_ref, y_ref, cnew_ref, hnew_ref, xcat, a_scr, b_scr, hcar, *, tt, valid_last):
    j = pl.program_id(1)
    c_rnn = xb_ref.shape[1]
    bw = c_rnn // N_LRU_BLOCKS
    tail = SUBLANES

    @pl.when(j == 0)
    def _():
        xcat[0:tail, :] = jnp.zeros((tail, c_rnn), F32)
        xcat[tail - (CONV_W - 1):tail, :] = cprev_ref[...]
        hcar[...] = jnp.broadcast_to(hprev_ref[...], (SUBLANES, c_rnn))

    xcat[tail:tail + tt, :] = xb_ref[...]
    cw = cw_ref[...]
    xc = cb_ref[...]
    for tap in range(CONV_W):
        off = tail - (CONV_W - 1) + tap
        xc = xc + xcat[off:off + tt, :] * cw[tap:tap + 1, :]
    cnew_ref[...] = xcat[tail + valid_last - (CONV_W - 1):tail + valid_last, :]
    xcat[0:tail, :] = xcat[tt:tt + tail, :]

    xcb = xc.astype(BF16)
    rs, gs = [], []
    for n in range(N_LRU_BLOCKS):
        xs = xcb[:, n * bw:(n + 1) * bw]
        rs.append(jnp.dot(xs, gaw_ref[n], preferred_element_type=F32))
        gs.append(jnp.dot(xs, gxw_ref[n], preferred_element_type=F32))
    r = _sigmoid(jnp.concatenate(rs, axis=1) + gab_ref[...])
    gi = _sigmoid(jnp.concatenate(gs, axis=1) + gxb_ref[...])
    log_a = (-LRU_C * r) * sp_ref[...]
    a = jnp.exp(log_a)
    a_scr[...] = a
    b_scr[...] = jnp.sqrt(-jnp.tanh(log_a) * (a * a + 1.0)) * (gi * xc)

    row = lax.broadcasted_iota(jnp.int32, (SUBLANES, SCAN_LANES), 0)
    for c in range(c_rnn // SCAN_LANES):
        cs = slice(c * SCAN_LANES, (c + 1) * SCAN_LANES)

        def body(i, carry, cs=cs):
            r0 = pl.multiple_of(i * SUBLANES, SUBLANES)
            av = a_scr[pl.ds(r0, SUBLANES), cs]
            bv = b_scr[pl.ds(r0, SUBLANES), cs]
            for k in (1, 2, 4):
                a_sh = jnp.where(row >= k, pltpu.roll(av, k, 0), 1.0)
                b_sh = jnp.where(row >= k, pltpu.roll(bv, k, 0), 0.0)
                bv = av * b_sh + bv
                av = av * a_sh
            hv = av * carry + bv
            b_scr[pl.ds(r0, SUBLANES), cs] = hv
            return jnp.broadcast_to(hv[SUBLANES - 1:SUBLANES, :], (SUBLANES, SCAN_LANES))

        hcar[:, cs] = lax.fori_loop(0, tt // SUBLANES, body, hcar[:, cs])

    h = b_scr[...]
    hnew_ref[...] = b_scr[valid_last - 1:valid_last, :]
    y_ref[...] = (h * _gelu_tanh(gb_ref[...])).astype(BF16)


def _lru_seq(u, n_seq, n_tiles, tt, row0, seq_stride, valid_last, out_rows, out_row0, cprev, hprev, wts):
    c_rnn = u.shape[1] // 2
    cw, cb, gaw, gab, gxw, gxb, sp = wts
    b0, bs, ob0 = row0 // tt, seq_stride // tt, out_row0 // tt
    bw = c_rnn // N_LRU_BLOCKS
    fix2 = lambda s, j: (0, 0)
    fix3 = lambda s, j: (0, 0, 0)
    per_seq = lambda s, j: (s, 0, 0)
    return pl.pallas_call(
        functools.partial(_lru_kernel, tt=tt, valid_last=valid_last),
        grid=(n_seq, n_tiles),
        in_specs=[pl.BlockSpec((tt, c_rnn), lambda s, j: (b0 + s * bs + j, 0)),
                  pl.BlockSpec((tt, c_rnn), lambda s, j: (b0 + s * bs + j, 1)),
                  pl.BlockSpec((None, CONV_W - 1, c_rnn), per_seq),
                  pl.BlockSpec((None, 1, c_rnn), per_seq),
                  pl.BlockSpec((CONV_W, c_rnn), fix2), pl.BlockSpec((1, c_rnn), fix2),
                  pl.BlockSpec((N_LRU_BLOCKS, bw, bw), fix3), pl.BlockSpec((1, c_rnn), fix2),
                  pl.BlockSpec((N_LRU_BLOCKS, bw, bw), fix3), pl.BlockSpec((1, c_rnn), fix2),
                  pl.BlockSpec((1, c_rnn), fix2)],
        out_specs=[pl.BlockSpec((tt, c_rnn), lambda s, j: (ob0 + s * bs + j, 0)),
                   pl.BlockSpec((None, CONV_W - 1, c_rnn), per_seq),
                   pl.BlockSpec((None, 1, c_rnn), per_seq)],
        out_shape=[jax.ShapeDtypeStruct((out_rows, c_rnn), BF16),
                   jax.ShapeDtypeStruct((n_seq, CONV_W - 1, c_rnn), F32),
                   jax.ShapeDtypeStruct((n_seq, 1, c_rnn), F32)],
        scratch_shapes=[pltpu.VMEM((SUBLANES + tt, c_rnn), F32), pltpu.VMEM((tt, c_rnn), F32),
                        pltpu.VMEM((tt, c_rnn), F32), pltpu.VMEM((SUBLANES, c_rnn), F32)],
        compiler_params=_cparams("parallel", "arbitrary"),
        name="rglru_seq",
    )(u, u, cprev, hprev, cw, cb, gaw, gab, gxw, gxb, sp)


def _moe_kernel(be_ref, nu_ref, x_ref, w1_ref, w3_ref, w2_ref, o_ref, w1b, w3b, w2b):
    i = pl.program_id(0)

    @pl.when(i < nu_ref[0])
    def _():
        prev = be_ref[jnp.maximum(i - 1, 0)]

        @pl.when(jnp.logical_or(i == 0, be_ref[i] != prev))
        def _():
            w1b[...] = w1_ref[...].astype(BF16)
            w3b[...] = w3_ref[...].astype(BF16)
            w2b[...] = w2_ref[...].astype(BF16)

        x = x_ref[...]
        a = jnp.dot(x, w1b[...], preferred_element_type=F32)
        b = jnp.dot(x, w3b[...], preferred_element_type=F32)
        h = (_silu(a) * b).astype(BF16)
        o_ref[...] = jnp.dot(h, w2b[...], preferred_element_type=F32)


def _moe_experts(x_sorted, w1, w3, w2, block_expert, n_used):
    n_rows, d = x_sorted.shape
    de = w1.shape[2]
    nb = n_rows // MOE_BLK
    blk = lambda i, be, nu: (jnp.minimum(i, nu[0] - 1), 0)
    wmap = lambda i, be, nu: (be[jnp.minimum(i, nu[0] - 1)], 0, 0)
    grid_spec = pltpu.PrefetchScalarGridSpec(
        num_scalar_prefetch=2,
        grid=(nb,),
        in_specs=[pl.BlockSpec((MOE_BLK, d), blk),
                  pl.BlockSpec((None, d, de), wmap), pl.BlockSpec((None, d, de), wmap),
                  pl.BlockSpec((None, de, d), wmap)],
        out_specs=pl.BlockSpec((MOE_BLK, d), blk),
        scratch_shapes=[pltpu.VMEM((d, de), BF16), pltpu.VMEM((d, de), BF16), pltpu.VMEM((de, d), BF16)],
    )
    return pl.pallas_call(
        _moe_kernel,
        grid_spec=grid_spec,
        out_shape=jax.ShapeDtypeStruct((n_rows, d), F32),
        compiler_params=_cparams("arbitrary"),
        name="moe_experts",
    )(block_expert, n_used, x_sorted, w1, w3, w2)


def _route(logits, router_bias, valid, n_experts):
    t = logits.shape[0]
    per = n_experts // N_GROUPS
    scores = jax.nn.sigmoid(logits)
    biased = scores + router_bias.astype(F32)
    grp = biased.reshape(t, N_GROUPS, per)
    grp_score = jnp.sum(lax.top_k(grp, 2)[0], -1)
    _, gidx = lax.top_k(grp_score, TOPK_GROUPS)
    gsel = jnp.any(gidx[..., None] == jnp.arange(N_GROUPS)[None, None, :], axis=1)
    emask = jnp.repeat(gsel, per, axis=-1)
    _, eidx = lax.top_k(jnp.where(emask, biased, NEG_INF), TOPK)
    g = jnp.take_along_axis(scores, eidx, axis=-1)
    g = g / (jnp.sum(g, -1, keepdims=True) + 1e-20) * ROUTED_SCALE
    g = jnp.where(valid[:, None], g, 0.0)
    return eidx.astype(jnp.int32), g


def _dispatch_plan(eidx, valid, n_experts, n_rows):
    t = eidx.shape[0]
    tk = t * TOPK
    flat_e = jnp.where(valid[:, None], eidx, n_experts).reshape(tk)
    order = jnp.argsort(flat_e).astype(jnp.int32)
    tok_sorted = order // TOPK
    member = jnp.sum((flat_e.reshape(t, TOPK)[:, :, None] == jnp.arange(n_experts)[None, None, :])
                     .astype(jnp.int32), axis=1)
    cum_excl = jnp.cumsum(member, axis=0) - member
    counts = jnp.sum(member, axis=0)
    padded = (counts + MOE_BLK - 1) // MOE_BLK * MOE_BLK
    pad_end = jnp.cumsum(padded)
    pad_start = pad_end - padded
    grp_start = jnp.cumsum(counts) - counts
    e_safe = jnp.minimum(eidx, n_experts - 1)
    dest = pad_start[e_safe] + jnp.take_along_axis(cum_excl, e_safe, axis=1)
    dest = jnp.where(valid[:, None], dest, 0).astype(jnp.int32)
    rows = jnp.arange(n_rows, dtype=jnp.int32)
    row_e = jnp.minimum(jnp.sum((pad_end[None, :] <= rows[:, None]).astype(jnp.int32), axis=1), n_experts - 1)
    within = rows - pad_start[row_e]
    src = jnp.clip(grp_start[row_e] + within, 0, tk - 1)
    row_tok = jnp.where(within < counts[row_e], tok_sorted[src], 0).astype(jnp.int32)
    block_expert = row_e[::MOE_BLK].astype(jnp.int32)
    n_used = jnp.maximum(pad_end[-1] // MOE_BLK, 1).astype(jnp.int32).reshape(1)
    return row_tok, dest, block_expert, n_used


def _split_bf16(w):
    hi = w.astype(BF16)
    lo = (w - hi.astype(F32)).astype(BF16)
    return hi, lo


def kernel(x_prompt, x_sample, cache_meta_k, cache_meta_v, cache_win_k, cache_win_v, state_conv, state_h, meta_tokens, ln_g, ln_b, attn_w_in, attn_w_out, attn_sink, lru_w_in, lru_conv_w, lru_conv_b, lru_gate_a_w, lru_gate_a_b, lru_gate_x_w, lru_gate_x_b, lru_lambda, lru_w_out, moe_router_w, moe_router_bias, moe_w1, moe_w3, moe_w2, moe_shared_w1, moe_shared_w3, moe_shared_w2):
    bsz, seq, d = x_prompt.shape
    dec_b, dec_s, _ = x_sample.shape
    n_meta = meta_tokens.shape[0]
    depth = ln_g.shape[0]
    n_experts = moe_router_w.shape[2]
    past_len = PAST_LEN
    n_heads = d // HEAD_DIM
    n_kv = n_heads // GROUP
    q_w, kv_w = n_heads * HEAD_DIM, n_kv * HEAD_DIM
    c_rnn = lru_w_in.shape[2] // 2
    alpha = (2 * depth) ** 0.25

    t_real = bsz * seq
    s_base, s_rows = t_real, dec_b * dec_s
    m_base = s_base + s_rows
    t_pad = m_base + bsz * META_BLK
    assert seq % LRU_TILE == 0 and seq % QBLK == 0 and s_rows % QBLK == 0 and t_pad % ROW_TILE == 0
    assert n_meta <= META_BLK and dec_s <= QBLK and q_w == d and kv_w % LANES == 0
    assert cache_win_k.shape[2] == WINDOW and WINDOW == QBLK and n_meta + WINDOW + dec_s <= 3 * QBLK
    nqb = seq // QBLK

    pos = np.zeros((t_pad,), np.float32)
    pos[:t_real] = np.tile(np.arange(seq) + n_meta, bsz)
    pos[s_base:m_base] = np.tile(past_len + n_meta + np.arange(dec_s), dec_b)
    valid_np = np.zeros((t_pad,), bool)
    valid_np[:m_base] = True
    for b in range(bsz):
        pos[m_base + b * META_BLK:m_base + b * META_BLK + n_meta] = np.arange(n_meta)
        valid_np[m_base + b * META_BLK:m_base + b * META_BLK + n_meta] = True
    valid = jnp.asarray(valid_np)

    half = ROT_DIM // 2
    freqs = ROPE_THETA ** (-jnp.arange(0, ROT_DIM, 2, dtype=F32) / ROT_DIM)
    ang = jnp.asarray(pos)[:, None] * freqs[None, :]
    cos, sin = jnp.cos(ang), jnp.sin(ang)
    ones = jnp.ones((t_pad, HEAD_DIM - ROT_DIM), F32)
    zeros_h = jnp.zeros((t_pad, half), F32)
    zeros_r = jnp.zeros((t_pad, HEAD_DIM - ROT_DIM), F32)
    reps = LANES // HEAD_DIM
    cos_t = jnp.tile(jnp.concatenate([cos, cos, ones], 1), (1, reps))
    sin_lo_t = jnp.tile(jnp.concatenate([zeros_h, sin, zeros_r], 1), (1, reps))
    sin_hi_t = jnp.tile(jnp.concatenate([-sin, zeros_h, zeros_r], 1), (1, reps))

    kj = np.arange(3 * QBLK)[:, None]
    qc = np.arange(QBLK)[None, :] // CHUNK
    key_chunk = np.where(kj < 2 * QBLK, kj // CHUNK - 2, 0)
    band_ok = (kj < 2 * QBLK) & (qc - key_chunk >= 0) & (qc - key_chunk <= WINDOW // CHUNK)
    meta_ok = (kj >= 2 * QBLK) & (kj < 2 * QBLK + n_meta)
    later = band_ok | meta_ok
    first = (band_ok & (kj >= QBLK)) | meta_ok
    only_meta = np.broadcast_to(meta_ok, later.shape)
    bias_p = jnp.asarray(np.where(np.stack([first, later, only_meta]), 0.0, NEG_INF).astype(np.float32))
    n_keys_s = n_meta + WINDOW + dec_s
    bias_s = jnp.asarray(np.where(np.broadcast_to(kj < n_keys_s, later.shape), 0.0, NEG_INF)
                         .astype(np.float32))[None]

    meta_rows = jnp.concatenate([meta_tokens.astype(F32), jnp.zeros((META_BLK - n_meta, d), F32)], 0)
    x = jnp.concatenate([x_prompt.reshape(t_real, d), x_sample.reshape(s_rows, d),
                         jnp.tile(meta_rows, (bsz, 1))], 0)
    xb = x.astype(BF16)

    n_assign = (t_real + s_rows + bsz * n_meta) * TOPK
    n_rows = -(-(n_assign + n_experts * (MOE_BLK - 1)) // MOE_BLK) * MOE_BLK

    outs = {k: [] for k in ("mk", "mv", "wk", "wv", "ks", "vs", "cp", "hp", "cs", "hs")}
    for l in range(depth):
        idx = l // 2
        if l % 2 == 0:
            w_in = attn_w_in[idx].astype(BF16)
            qkv = _qkv_proj(xb, w_in, cos_t, sin_lo_t, sin_hi_t, q_w + kv_w)
            k_col, v_col = q_w // kv_w, q_w // kv_w + 1
            sink_rows = jnp.repeat(attn_sink[idx].astype(F32).reshape(n_kv, 1, GROUP), QBLK, axis=2)

            steps = nqb + 1
            is_meta = lambda i: (i % steps) == nqb
            bidx = lambda i: i // steps
            pblk = lambda i: i % steps
            frame_blk = lambda i: bidx(i) * nqb + jnp.minimum(pblk(i), nqb - 1)
            meta_blk = lambda i: m_base // QBLK + bidx(i)
            q_map = lambda i: jnp.where(is_meta(i), meta_blk(i), frame_blk(i))
            prev_map = lambda i: bidx(i) * nqb + jnp.clip(pblk(i) - 1, 0, nqb - 1)
            bias_map = lambda i: jnp.where(is_meta(i), 2, jnp.minimum(pblk(i), 1))
            o = _attention(qkv, qkv, bsz * steps, q_map, [prev_map, frame_blk, meta_blk], bias_p, bias_map,
                           sink_rows, t_pad, q_map, q_w, kv_w, k_col, v_col, "attn_prompt")

            q_s = qkv[s_base:m_base, :q_w].reshape(dec_b, dec_s, q_w)
            q_s = jnp.pad(q_s, ((0, 0), (0, QBLK - dec_s), (0, 0))).reshape(dec_b * QBLK, q_w)
            k_new = qkv[s_base:m_base, q_w:q_w + kv_w].reshape(dec_b, dec_s, kv_w)
            v_new = qkv[s_base:m_base, q_w + kv_w:].reshape(dec_b, dec_s, kv_w)
            pad_k = jnp.zeros((dec_b, 3 * QBLK - n_keys_s, kv_w), F32)
            k_s = jnp.concatenate([cache_meta_k[idx].reshape(dec_b, n_meta, kv_w).astype(F32),
                                   cache_win_k[idx].reshape(dec_b, WINDOW, kv_w).astype(F32), k_new, pad_k], 1)
            v_s = jnp.concatenate([cache_meta_v[idx].reshape(dec_b, n_meta, kv_w).astype(F32),
                                   cache_win_v[idx].reshape(dec_b, WINDOW, kv_w).astype(F32), v_new, pad_k], 1)
            kv_s = jnp.concatenate([k_s, v_s], axis=2).reshape(dec_b * 3 * QBLK, 2 * kv_w)
            o_s = _attention(q_s, kv_s, dec_b, lambda i: i,
                             [lambda i: 3 * i, lambda i: 3 * i + 1, lambda i: 3 * i + 2], bias_s, lambda i: 0,
                             sink_rows, dec_b * QBLK, lambda i: i, q_w, kv_w, 0, 1, "attn_sample")
            o_s = o_s.reshape(dec_b, QBLK, q_w)[:, :dec_s].reshape(s_rows, q_w)
            o = lax.dynamic_update_slice(o, o_s, (s_base, 0))

            kp = qkv[:t_real, q_w:q_w + kv_w].reshape(bsz, seq, n_kv, HEAD_DIM)
            vp = qkv[:t_real, q_w + kv_w:].reshape(bsz, seq, n_kv, HEAD_DIM)
            km = qkv[m_base:, q_w:q_w + kv_w].reshape(bsz, META_BLK, n_kv, HEAD_DIM)[:, :n_meta]
            vm = qkv[m_base:, q_w + kv_w:].reshape(bsz, META_BLK, n_kv, HEAD_DIM)[:, :n_meta]
            outs["mk"].append(km); outs["mv"].append(vm)
            outs["wk"].append(kp[:, -WINDOW:]); outs["wv"].append(vp[:, -WINDOW:])
            outs["ks"].append(k_new.reshape(dec_b, dec_s, n_kv, HEAD_DIM))
            outs["vs"].append(v_new.reshape(dec_b, dec_s, n_kv, HEAD_DIM))
            w_out = attn_w_out[idx].astype(BF16)
        else:
            u = _matmul(xb, lru_w_in[idx].astype(BF16), 1024, "lru_in_proj")
            sp = jax.nn.softplus(-lru_lambda[idx].astype(F32)).reshape(1, c_rnn)
            wts = (lru_conv_w[idx].astype(F32), lru_conv_b[idx].astype(F32).reshape(1, c_rnn),
                   lru_gate_a_w[idx].astype(BF16), lru_gate_a_b[idx].astype(F32).reshape(1, c_rnn),
                   lru_gate_x_w[idx].astype(BF16), lru_gate_x_b[idx].astype(F32).reshape(1, c_rnn), sp)
            zc = jnp.zeros((bsz, CONV_W - 1, c_rnn), F32)
            zh = jnp.zeros((bsz, 1, c_rnn), F32)
            y_m, c_m, h_m = _lru_seq(u, bsz, 1, META_BLK, m_base, META_BLK, n_meta, bsz * META_BLK, 0,
                                     zc, zh, wts)
            o, c_p, h_p = _lru_seq(u, bsz, seq // LRU_TILE, LRU_TILE, 0, seq, LRU_TILE, t_pad, 0,
                                   c_m, h_m, wts)
            y_s, c_s, h_s = _lru_seq(u, dec_b, 1, dec_s, s_base, dec_s, dec_s, s_rows, 0,
                                     state_conv[idx].astype(F32), state_h[idx].astype(F32).reshape(dec_b, 1, c_rnn),
                                     wts)
            o = lax.dynamic_update_slice(o, y_s, (s_base, 0))
            o = lax.dynamic_update_slice(o, y_m, (m_base, 0))
            outs["cp"].append(c_p); outs["hp"].append(h_p.reshape(bsz, c_rnn))
            outs["cs"].append(c_s); outs["hs"].append(h_s.reshape(dec_b, c_rnn))
            w_out = lru_w_out[idx].astype(BF16)

        ne_pad = -(-n_experts // LANES) * LANES
        wr = jnp.pad(moe_router_w[l].astype(F32), ((0, 0), (0, ne_pad - n_experts)))
        wr_hi, wr_lo = _split_bf16(wr)
        x1, x1b, logits = _proj_ln(o, w_out, x, ln_g[l, 0].astype(F32).reshape(1, d),
                                   ln_b[l, 0].astype(F32).reshape(1, d), wr_hi, wr_lo, alpha)

        eidx, gate = _route(logits[:, :n_experts], moe_router_bias[l], valid, n_experts)
        row_tok, dest, block_expert, n_used = _dispatch_plan(eidx, valid, n_experts, n_rows)
        x_sorted = jnp.take(x1b, row_tok, axis=0)
        y_sorted = _moe_experts(x_sorted, moe_w1[l], moe_w3[l], moe_w2[l], block_expert, n_used)
        routed = jnp.sum(jnp.take(y_sorted, dest, axis=0) * gate[:, :, None], axis=1)

        w13 = jnp.concatenate([moe_shared_w1[l], moe_shared_w3[l]], axis=1).astype(BF16)
        x, xb = _shared_ln(x1b, x1, routed, w13, moe_shared_w2[l].astype(BF16),
                           ln_g[l, 1].astype(F32).reshape(1, d), ln_b[l, 1].astype(F32).reshape(1, d), alpha)

    y_prompt = x[:t_real].reshape(bsz, seq, d)
    y_sample = x[s_base:m_base].reshape(dec_b, dec_s, d)
    st = lambda k: jnp.stack(outs[k])
    return (y_prompt, y_sample, st("mk"), st("mv"), st("wk"), st("wv"), st("ks"), st("vs"),
            st("cp"), st("hp"), st("cs"), st("hs"))
```

```python
import functools
import math

import numpy as np
import jax
import jax.numpy as jnp
from jax import lax
from jax.experimental import pallas as pl
from jax.experimental.pallas import tpu as pltpu

F32 = jnp.float32
BF16 = jnp.bfloat16

PAST_LEN = 1024
CHUNK = 64
WINDOW = 128
HEAD_DIM = 64
GROUP = 8
ROT_DIM = HEAD_DIM // 4
ROPE_THETA = 500000.0
ATTN_SCALE = HEAD_DIM ** -0.5
N_LRU_BLOCKS = 8
CONV_W = 4
LRU_C = 8.0
N_GROUPS = 8
TOPK_GROUPS = 4
TOPK = 8
ROUTED_SCALE = 2.5
LN_EPS = 1e-5
NEG_INF = -1e30

LANES = 128
SUBLANES = 8
QBLK = 2 * CHUNK
META_BLK = 128
ROW_TILE = 512
LN_TILE = 256
LRU_TILE = 256
SCAN_LANES = 512
MOE_BLK = 256
VMEM_LIMIT = 56 * 1024 * 1024


def _cparams(*sem):
    return pltpu.CompilerParams(dimension_semantics=sem, vmem_limit_bytes=VMEM_LIMIT)


def _mm_kernel(x_ref, w_ref, o_ref):
    o_ref[...] = jnp.dot(x_ref[...], w_ref[...], preferred_element_type=F32)


def _matmul(x, w, tn, name):
    m, k = x.shape
    n = w.shape[1]
    tm = ROW_TILE
    return pl.pallas_call(
        _mm_kernel,
        grid=(n // tn, m // tm),
        in_specs=[pl.BlockSpec((tm, k), lambda j, i: (i, 0)),
                  pl.BlockSpec((k, tn), lambda j, i: (0, j))],
        out_specs=pl.BlockSpec((tm, tn), lambda j, i: (i, j)),
        out_shape=jax.ShapeDtypeStruct((m, n), F32),
        compiler_params=_cparams("parallel", "parallel"),
        name=name,
    )(x, w)


def _qkv_kernel(x_ref, w_ref, c_ref, s1_ref, s2_ref, o_ref, *, tn, rope_cols):
    j = pl.program_id(0)
    acc = jnp.dot(x_ref[...], w_ref[...], preferred_element_type=F32)
    c, s1, s2 = c_ref[...], s1_ref[...], s2_ref[...]
    lane = lax.broadcasted_iota(jnp.int32, (acc.shape[0], LANES), 1)
    for g in range(tn // LANES):
        a = acc[:, g * LANES:(g + 1) * LANES]
        lo = pltpu.roll(a, ROT_DIM // 2, 1)
        hi = pltpu.roll(a, LANES - ROT_DIM // 2, 1)
        roped = a * c + lo * s1 + hi * s2
        col = j * tn + g * LANES + lane
        o_ref[:, g * LANES:(g + 1) * LANES] = jnp.where(col < rope_cols, roped, a)


def _qkv_proj(xb, w, cos_t, sin_lo_t, sin_hi_t, rope_cols):
    m, k = xb.shape
    n = w.shape[1]
    tm = ROW_TILE
    tn = 512 if n % 512 == 0 else 256
    assert n % tn == 0 and m % tm == 0
    tab = pl.BlockSpec((tm, LANES), lambda j, i: (i, 0))
    return pl.pallas_call(
        functools.partial(_qkv_kernel, tn=tn, rope_cols=rope_cols),
        grid=(n // tn, m // tm),
        in_specs=[pl.BlockSpec((tm, k), lambda j, i: (i, 0)),
                  pl.BlockSpec((k, tn), lambda j, i: (0, j)),
                  tab, tab, tab],
        out_specs=pl.BlockSpec((tm, tn), lambda j, i: (i, j)),
        out_shape=jax.ShapeDtypeStruct((m, n), F32),
        compiler_params=_cparams("parallel", "parallel"),
        name="qkv_rope",
    )(xb, w, cos_t, sin_lo_t, sin_hi_t)


def _layer_norm_rows(z, g, b):
    mu = jnp.mean(z, -1, keepdims=True)
    d = z - mu
    var = jnp.mean(d * d, -1, keepdims=True)
    return d * lax.rsqrt(var + LN_EPS) * g + b


def _pack_halves(y):
    h = y.shape[1] // 2
    lo = lax.bitcast_convert_type(y[:, :h].astype(BF16).astype(F32), jnp.uint32)
    hi = lax.bitcast_convert_type(y[:, h:].astype(BF16).astype(F32), jnp.uint32)
    return (lo >> 16) | (hi & jnp.uint32(0xFFFF0000))


def _unpack_halves(w):
    lo = lax.bitcast_convert_type(w << 16, F32).astype(BF16)
    hi = lax.bitcast_convert_type(w & jnp.uint32(0xFFFF0000), F32).astype(BF16)
    return lo, hi


def _proj_ln_kernel(a_ref, w_ref, x_ref, g_ref, b_ref, wrh_ref, wrl_ref, o_ref, ob_ref, op_ref, lg_ref, *, alpha):
    acc = jnp.dot(a_ref[...], w_ref[...], preferred_element_type=F32)
    y = _layer_norm_rows(alpha * x_ref[...] + acc, g_ref[...], b_ref[...])
    o_ref[...] = y
    yb = y.astype(BF16)
    ob_ref[...] = yb
    op_ref[...] = _pack_halves(y)
    ylo = (y - yb.astype(F32)).astype(BF16)
    wrh = wrh_ref[...]
    lg_ref[...] = (jnp.dot(yb, wrh, preferred_element_type=F32)
                   + jnp.dot(ylo, wrh, preferred_element_type=F32)
                   + jnp.dot(yb, wrl_ref[...], preferred_element_type=F32))


def _proj_ln(a, w, x, g, b, wr_hi, wr_lo, alpha):
    m, k = a.shape
    d = w.shape[1]
    ne = wr_hi.shape[1]
    tm = LN_TILE
    row = lambda i: (i, 0)
    fix = lambda i: (0, 0)
    return pl.pallas_call(
        functools.partial(_proj_ln_kernel, alpha=alpha),
        grid=(m // tm,),
        in_specs=[pl.BlockSpec((tm, k), row), pl.BlockSpec((k, d), fix), pl.BlockSpec((tm, d), row),
                  pl.BlockSpec((1, d), fix), pl.BlockSpec((1, d), fix),
                  pl.BlockSpec((d, ne), fix), pl.BlockSpec((d, ne), fix)],
        out_specs=[pl.BlockSpec((tm, d), row), pl.BlockSpec((tm, d), row), pl.BlockSpec((tm, d // 2), row),
                   pl.BlockSpec((tm, ne), row)],
        out_shape=[jax.ShapeDtypeStruct((m, d), F32), jax.ShapeDtypeStruct((m, d), BF16),
                   jax.ShapeDtypeStruct((m, d // 2), jnp.uint32), jax.ShapeDtypeStruct((m, ne), F32)],
        compiler_params=_cparams("parallel"),
        name="proj_ln_router",
    )(a, w, x, g, b, wr_hi, wr_lo)


def _silu(x):
    return x * (1.0 / (1.0 + jnp.exp(-x)))


def _shared_ln_kernel(xb_ref, x_ref, r_ref, w13_ref, w2_ref, g_ref, b_ref, o_ref, ob_ref, *, alpha, dh):
    u = jnp.dot(xb_ref[...], w13_ref[...], preferred_element_type=F32)
    hs = (_silu(u[:, :dh]) * u[:, dh:]).astype(BF16)
    sh = jnp.dot(hs, w2_ref[...], preferred_element_type=F32)
    y = _layer_norm_rows(alpha * x_ref[...] + (r_ref[...] + sh), g_ref[...], b_ref[...])
    o_ref[...] = y
    ob_ref[...] = y.astype(BF16)


def _shared_ln(xb, x, routed, w13, w2, g, b, alpha):
    m, d = x.shape
    dh = w2.shape[0]
    tm = LN_TILE
    row = lambda i: (i, 0)
    fix = lambda i: (0, 0)
    return pl.pallas_call(
        functools.partial(_shared_ln_kernel, alpha=alpha, dh=dh),
        grid=(m // tm,),
        in_specs=[pl.BlockSpec((tm, d), row), pl.BlockSpec((tm, d), row), pl.BlockSpec((tm, d), row),
                  pl.BlockSpec((d, 2 * dh), fix), pl.BlockSpec((dh, d), fix),
                  pl.BlockSpec((1, d), fix), pl.BlockSpec((1, d), fix)],
        out_specs=[pl.BlockSpec((tm, d), row), pl.BlockSpec((tm, d), row)],
        out_shape=[jax.ShapeDtypeStruct((m, d), F32), jax.ShapeDtypeStruct((m, d), BF16)],
        compiler_params=_cparams("parallel"),
        name="shared_ln",
    )(xb, x, routed, w13, w2, g, b)


def _attn_kernel(q_ref, k0_ref, k1_ref, k2_ref, v0_ref, v1_ref, v2_ref, bias_ref, sink_ref, init_ref, o_ref,
                 ot_ref, *, n_kv):
    del init_ref
    hd = HEAD_DIM
    qt = (q_ref[...] * ATTN_SCALE).T.astype(BF16)
    kb = jnp.concatenate([k0_ref[...], k1_ref[...], k2_ref[...]], 0).astype(BF16)
    vt = jnp.concatenate([v0_ref[...], v1_ref[...], v2_ref[...]], 0).T.astype(BF16)
    bias = bias_ref[0]
    bias = jnp.concatenate([bias] * GROUP, axis=1)
    for h in range(n_kv):
        kh = kb[:, h * hd:(h + 1) * hd]
        qth = jnp.concatenate(
            [qt[(h * GROUP + g) * hd:(h * GROUP + g + 1) * hd, :] for g in range(GROUP)], axis=1)
        s = jnp.dot(kh, qth, preferred_element_type=F32) + bias
        sink = sink_ref[h]
        m = jnp.maximum(jnp.max(s, axis=0, keepdims=True), sink)
        p = jnp.exp(s - m)
        den = jnp.sum(p, axis=0, keepdims=True) + jnp.exp(sink - m)
        ot = jnp.dot(vt[h * hd:(h + 1) * hd, :], p.astype(BF16), preferred_element_type=F32)
        ot = ot * (1.0 / den)
        for g in range(GROUP):
            r0 = (h * GROUP + g) * hd
            ot_ref[r0:r0 + hd, :] = ot[:, g * QBLK:(g + 1) * QBLK]
    o_ref[...] = ot_ref[...].T.astype(BF16)


def _attention(q_arr, k_arr, n_steps, q_map, kv_maps, bias, bias_map, sink_rows, init, out_map,
               q_w, kv_w, k_col, v_col, name):
    n_kv = kv_w // HEAD_DIM
    kspecs = [pl.BlockSpec((QBLK, kv_w), (lambda i, f=f: (f(i), k_col))) for f in kv_maps]
    vspecs = [pl.BlockSpec((QBLK, kv_w), (lambda i, f=f: (f(i), v_col))) for f in kv_maps]
    nk = 3 * QBLK
    return pl.pallas_call(
        functools.partial(_attn_kernel, n_kv=n_kv),
        grid=(n_steps,),
        in_specs=[pl.BlockSpec((QBLK, q_w), lambda i: (q_map(i), 0))] + kspecs + vspecs + [
            pl.BlockSpec((1, nk, QBLK), lambda i: (bias_map(i), 0, 0)),
            pl.BlockSpec((n_kv, 1, GROUP * QBLK), lambda i: (0, 0, 0)),
            pl.BlockSpec(memory_space=pl.ANY)],
        out_specs=pl.BlockSpec((QBLK, q_w), lambda i: (out_map(i), 0)),
        out_shape=jax.ShapeDtypeStruct(init.shape, BF16),
        scratch_shapes=[pltpu.VMEM((q_w, QBLK), F32)],
        input_output_aliases={9: 0},
        compiler_params=_cparams("parallel"),
        name=name,
    )(q_arr, k_arr, k_arr, k_arr, k_arr, k_arr, k_arr, bias, sink_rows, init)


def _gelu_tanh(x):
    c = math.sqrt(2.0 / math.pi)
    return x * (0.5 * (1.0 + jnp.tanh(c * (x + 0.044715 * (x * x * x)))))


def _sigmoid(x):
    return 1.0 / (1.0 + jnp.exp(-x))


def _lru_kernel(xb_ref, gb_ref, cprev_ref, hprev_ref, cw_ref, cb_ref, gaw_ref, gab_ref, gxw_ref, gxb_ref,
                sp_ref, init_ref, y_ref, cnew_ref, hnew_ref, xcat, a_scr, b_scr, hcar, *, tt, valid_last):
    del init_ref
    j = pl.program_id(1)
    c_rnn = xb_ref.shape[1]
    bw = c_rnn // N_LRU_BLOCKS
    tail = SUBLANES

    @pl.when(j == 0)
    def _():
        xcat[0:tail, :] = jnp.zeros((tail, c_rnn), F32)
        xcat[tail - (CONV_W - 1):tail, :] = cprev_ref[...]
        hcar[...] = jnp.broadcast_to(hprev_ref[...], (SUBLANES, c_rnn))

    xcat[tail:tail + tt, :] = xb_ref[...]
    cw = cw_ref[...]
    xc = cb_ref[...]
    for tap in range(CONV_W):
        off = tail - (CONV_W - 1) + tap
        xc = xc + xcat[off:off + tt, :] * cw[tap:tap + 1, :]
    cnew_ref[...] = xcat[tail + valid_last - (CONV_W - 1):tail + valid_last, :]
    xcat[0:tail, :] = xcat[tt:tt + tail, :]

    xcb = xc.astype(BF16)
    rs, gs = [], []
    for n in range(N_LRU_BLOCKS):
        xs = xcb[:, n * bw:(n + 1) * bw]
        rs.append(jnp.dot(xs, gaw_ref[n], preferred_element_type=F32))
        gs.append(jnp.dot(xs, gxw_ref[n], preferred_element_type=F32))
    r = _sigmoid(jnp.concatenate(rs, axis=1) + gab_ref[...])
    gi = _sigmoid(jnp.concatenate(gs, axis=1) + gxb_ref[...])
    log_a = (-LRU_C * r) * sp_ref[...]
    a = jnp.exp(log_a)
    a_scr[...] = a
    b_scr[...] = jnp.sqrt(-jnp.tanh(log_a) * (a * a + 1.0)) * (gi * xc)

    row = lax.broadcasted_iota(jnp.int32, (SUBLANES, SCAN_LANES), 0)
    for c in range(c_rnn // SCAN_LANES):
        cs = slice(c * SCAN_LANES, (c + 1) * SCAN_LANES)

        def body(i, carry, cs=cs):
            r0 = pl.multiple_of(i * SUBLANES, SUBLANES)
            av = a_scr[pl.ds(r0, SUBLANES), cs]
            bv = b_scr[pl.ds(r0, SUBLANES), cs]
            for k in (1, 2, 4):
                a_sh = jnp.where(row >= k, pltpu.roll(av, k, 0), 1.0)
                b_sh = jnp.where(row >= k, pltpu.roll(bv, k, 0), 0.0)
                bv = av * b_sh + bv
                av = av * a_sh
            hv = av * carry + bv
            b_scr[pl.ds(r0, SUBLANES), cs] = hv
            return jnp.broadcast_to(hv[SUBLANES - 1:SUBLANES, :], (SUBLANES, SCAN_LANES))

        hcar[:, cs] = lax.fori_loop(0, tt // SUBLANES, body, hcar[:, cs])

    h = b_scr[...]
    hnew_ref[...] = b_scr[valid_last - 1:valid_last, :]
    y_ref[...] = (h * _gelu_tanh(gb_ref[...])).astype(BF16)


def _lru_seq(u, n_seq, n_tiles, tt, row0, seq_stride, valid_last, init, out_row0, cprev, hprev, wts):
    c_rnn = u.shape[1] // 2
    cw, cb, gaw, gab, gxw, gxb, sp = wts
    b0, bs, ob0 = row0 // tt, seq_stride // tt, out_row0 // tt
    bw = c_rnn // N_LRU_BLOCKS
    fix2 = lambda s, j: (0, 0)
    fix3 = lambda s, j: (0, 0, 0)
    per_seq = lambda s, j: (s, 0, 0)
    return pl.pallas_call(
        functools.partial(_lru_kernel, tt=tt, valid_last=valid_last),
        grid=(n_seq, n_tiles),
        in_specs=[pl.BlockSpec((tt, c_rnn), lambda s, j: (b0 + s * bs + j, 0)),
                  pl.BlockSpec((tt, c_rnn), lambda s, j: (b0 + s * bs + j, 1)),
                  pl.BlockSpec((None, CONV_W - 1, c_rnn), per_seq),
                  pl.BlockSpec((None, 1, c_rnn), per_seq),
                  pl.BlockSpec((CONV_W, c_rnn), fix2), pl.BlockSpec((1, c_rnn), fix2),
                  pl.BlockSpec((N_LRU_BLOCKS, bw, bw), fix3), pl.BlockSpec((1, c_rnn), fix2),
                  pl.BlockSpec((N_LRU_BLOCKS, bw, bw), fix3), pl.BlockSpec((1, c_rnn), fix2),
                  pl.BlockSpec((1, c_rnn), fix2), pl.BlockSpec(memory_space=pl.ANY)],
        out_specs=[pl.BlockSpec((tt, c_rnn), lambda s, j: (ob0 + s * bs + j, 0)),
                   pl.BlockSpec((None, CONV_W - 1, c_rnn), per_seq),
                   pl.BlockSpec((None, 1, c_rnn), per_seq)],
        out_shape=[jax.ShapeDtypeStruct(init.shape, BF16),
                   jax.ShapeDtypeStruct((n_seq, CONV_W - 1, c_rnn), F32),
                   jax.ShapeDtypeStruct((n_seq, 1, c_rnn), F32)],
        scratch_shapes=[pltpu.VMEM((SUBLANES + tt, c_rnn), F32), pltpu.VMEM((tt, c_rnn), F32),
                        pltpu.VMEM((tt, c_rnn), F32), pltpu.VMEM((SUBLANES, c_rnn), F32)],
        input_output_aliases={11: 0},
        compiler_params=_cparams("parallel", "arbitrary"),
        name="rglru_seq",
    )(u, u, cprev, hprev, cw, cb, gaw, gab, gxw, gxb, sp, init)


def _dispatch_kernel(dest_ref, x_hbm, init_hbm, o_hbm, sem, *, tm):
    del init_hbm
    i = pl.program_id(0)

    def row_copy(src_row, dst_row):
        return pltpu.make_async_copy(x_hbm.at[pl.ds(src_row, 1)], o_hbm.at[pl.ds(dst_row, 1)], sem)

    def issue(r, c):
        for k in range(TOPK):
            row_copy(i * tm + r, dest_ref[r * TOPK + k]).start()
        return c

    lax.fori_loop(0, tm, issue, 0)

    def drain(r, c):
        for k in range(TOPK):
            row_copy(0, 0).wait()
        return c

    lax.fori_loop(0, tm, drain, 0)


def _dispatch(xp, dest_flat, n_total):
    t_pad, hw = xp.shape
    tm = LN_TILE
    init = jnp.zeros((n_total, hw), xp.dtype)
    return pl.pallas_call(
        functools.partial(_dispatch_kernel, tm=tm),
        grid=(t_pad // tm,),
        in_specs=[pl.BlockSpec((tm * TOPK,), lambda i: (i,), memory_space=pltpu.SMEM),
                  pl.BlockSpec(memory_space=pl.ANY), pl.BlockSpec(memory_space=pl.ANY)],
        out_specs=pl.BlockSpec(memory_space=pl.ANY),
        out_shape=jax.ShapeDtypeStruct((n_total, hw), xp.dtype),
        scratch_shapes=[pltpu.SemaphoreType.DMA(())],
        input_output_aliases={2: 0},
        compiler_params=_cparams("arbitrary"),
        name="moe_dispatch",
    )(dest_flat, xp, init)


def _moe_kernel(be_ref, nu_ref, x_ref, w1_ref, w3_ref, w2_ref, o_ref, w1b, w3b, w2b):
    i = pl.program_id(0)

    @pl.when(i < nu_ref[0])
    def _():
        prev = be_ref[jnp.maximum(i - 1, 0)]

        @pl.when(jnp.logical_or(i == 0, be_ref[i] != prev))
        def _():
            w1b[...] = w1_ref[...].astype(BF16)
            w3b[...] = w3_ref[...].astype(BF16)
            w2b[...] = w2_ref[...].astype(BF16)

        xlo, xhi = _unpack_halves(x_ref[...])
        hk = xlo.shape[1]
        a = (jnp.dot(xlo, w1b[0:hk, :], preferred_element_type=F32)
             + jnp.dot(xhi, w1b[hk:2 * hk, :], preferred_element_type=F32))
        b = (jnp.dot(xlo, w3b[0:hk, :], preferred_element_type=F32)
             + jnp.dot(xhi, w3b[hk:2 * hk, :], preferred_element_type=F32))
        h = (_silu(a) * b).astype(BF16)
        o_ref[...] = jnp.dot(h, w2b[...], preferred_element_type=F32)


def _moe_experts(x_sorted, w1, w3, w2, layer, block_expert, n_used, n_rows):
    hw = x_sorted.shape[1]
    d, de = w1.shape[2], w1.shape[3]
    nb = n_rows // MOE_BLK
    blk = lambda i, be, nu: (jnp.minimum(i, nu[0] - 1), 0)
    wmap = lambda i, be, nu: (layer, be[jnp.minimum(i, nu[0] - 1)], 0, 0)
    grid_spec = pltpu.PrefetchScalarGridSpec(
        num_scalar_prefetch=2,
        grid=(nb,),
        in_specs=[pl.BlockSpec((MOE_BLK, hw), blk),
                  pl.BlockSpec((None, None, d, de), wmap), pl.BlockSpec((None, None, d, de), wmap),
                  pl.BlockSpec((None, None, de, d), wmap)],
        out_specs=pl.BlockSpec((MOE_BLK, d), blk),
        scratch_shapes=[pltpu.VMEM((d, de), BF16), pltpu.VMEM((d, de), BF16), pltpu.VMEM((de, d), BF16)],
    )
    return pl.pallas_call(
        _moe_kernel,
        grid_spec=grid_spec,
        out_shape=jax.ShapeDtypeStruct((n_rows, d), F32),
        compiler_params=_cparams("arbitrary"),
        name="moe_experts",
    )(block_expert, n_used, x_sorted, w1, w3, w2)


def _route(logits, router_bias, valid, n_experts):
    t = logits.shape[0]
    per = n_experts // N_GROUPS
    scores = jax.nn.sigmoid(logits)
    biased = scores + router_bias.astype(F32)
    grp = biased.reshape(t, N_GROUPS, per)
    grp_score = jnp.sum(lax.top_k(grp, 2)[0], -1)
    _, gidx = lax.top_k(grp_score, TOPK_GROUPS)
    gsel = jnp.any(gidx[..., None] == jnp.arange(N_GROUPS)[None, None, :], axis=1)
    emask = jnp.repeat(gsel, per, axis=-1)
    _, eidx = lax.top_k(jnp.where(emask, biased, NEG_INF), TOPK)
    g = jnp.take_along_axis(scores, eidx, axis=-1)
    g = g / (jnp.sum(g, -1, keepdims=True) + 1e-20) * ROUTED_SCALE
    g = jnp.where(valid[:, None], g, 0.0)
    return eidx.astype(jnp.int32), g


def _dispatch_plan(eidx, valid, spare_rows, n_experts, n_rows):
    t = eidx.shape[0]
    member = jnp.sum((eidx[:, :, None] == jnp.arange(n_experts)[None, None, :]).astype(jnp.int32), axis=1)
    member = jnp.where(valid[:, None], member, 0)
    cum_excl = jnp.cumsum(member, axis=0) - member
    counts = jnp.sum(member, axis=0)
    padded = (counts + MOE_BLK - 1) // MOE_BLK * MOE_BLK
    pad_end = jnp.cumsum(padded)
    pad_start = pad_end - padded
    dest = pad_start[eidx] + jnp.take_along_axis(cum_excl, eidx, axis=1)
    dest = jnp.where(valid[:, None], dest, spare_rows).astype(jnp.int32)
    blk_row = jnp.arange(n_rows // MOE_BLK, dtype=jnp.int32) * MOE_BLK
    block_expert = jnp.minimum(jnp.sum((pad_end[None, :] <= blk_row[:, None]).astype(jnp.int32), axis=1),
                               n_experts - 1).astype(jnp.int32)
    n_used = jnp.maximum(pad_end[-1] // MOE_BLK, 1).astype(jnp.int32).reshape(1)
    return dest, block_expert, n_used


def _split_bf16(w):
    hi = w.astype(BF16)
    lo = (w - hi.astype(F32)).astype(BF16)
    return hi, lo


def kernel(x_prompt, x_sample, cache_meta_k, cache_meta_v, cache_win_k, cache_win_v, state_conv, state_h, meta_tokens, ln_g, ln_b, attn_w_in, attn_w_out, attn_sink, lru_w_in, lru_conv_w, lru_conv_b, lru_gate_a_w, lru_gate_a_b, lru_gate_x_w, lru_gate_x_b, lru_lambda, lru_w_out, moe_router_w, moe_router_bias, moe_w1, moe_w3, moe_w2, moe_shared_w1, moe_shared_w3, moe_shared_w2):
    bsz, seq, d = x_prompt.shape
    dec_b, dec_s, _ = x_sample.shape
    n_meta = meta_tokens.shape[0]
    depth = ln_g.shape[0]
    n_experts = moe_router_w.shape[2]
    past_len = PAST_LEN
    n_heads = d // HEAD_DIM
    n_kv = n_heads // GROUP
    q_w, kv_w = n_heads * HEAD_DIM, n_kv * HEAD_DIM
    c_rnn = lru_w_in.shape[2] // 2
    alpha = (2 * depth) ** 0.25

    t_real = bsz * seq
    s_base, s_rows = t_real, dec_b * dec_s
    m_base = s_base + s_rows
    t_pad = m_base + bsz * META_BLK
    assert seq % LRU_TILE == 0 and seq % QBLK == 0 and s_rows % QBLK == 0 and t_pad % ROW_TILE == 0
    assert n_meta <= META_BLK and dec_s <= QBLK and q_w == d and kv_w % LANES == 0
    assert cache_win_k.shape[2] == WINDOW and WINDOW == QBLK and n_meta + WINDOW + dec_s <= 3 * QBLK
    nqb = seq // QBLK

    pos = np.zeros((t_pad,), np.float32)
    pos[:t_real] = np.tile(np.arange(seq) + n_meta, bsz)
    pos[s_base:m_base] = np.tile(past_len + n_meta + np.arange(dec_s), dec_b)
    valid_np = np.zeros((t_pad,), bool)
    valid_np[:m_base] = True
    for b in range(bsz):
        pos[m_base + b * META_BLK:m_base + b * META_BLK + n_meta] = np.arange(n_meta)
        valid_np[m_base + b * META_BLK:m_base + b * META_BLK + n_meta] = True
    valid = jnp.asarray(valid_np)

    half = ROT_DIM // 2
    freqs = ROPE_THETA ** (-jnp.arange(0, ROT_DIM, 2, dtype=F32) / ROT_DIM)
    ang = jnp.asarray(pos)[:, None] * freqs[None, :]
    cos, sin = jnp.cos(ang), jnp.sin(ang)
    ones = jnp.ones((t_pad, HEAD_DIM - ROT_DIM), F32)
    zeros_h = jnp.zeros((t_pad, half), F32)
    zeros_r = jnp.zeros((t_pad, HEAD_DIM - ROT_DIM), F32)
    reps = LANES // HEAD_DIM
    cos_t = jnp.tile(jnp.concatenate([cos, cos, ones], 1), (1, reps))
    sin_lo_t = jnp.tile(jnp.concatenate([zeros_h, sin, zeros_r], 1), (1, reps))
    sin_hi_t = jnp.tile(jnp.concatenate([-sin, zeros_h, zeros_r], 1), (1, reps))

    kj = np.arange(3 * QBLK)[:, None]
    qc = np.arange(QBLK)[None, :] // CHUNK
    key_chunk = np.where(kj < 2 * QBLK, kj // CHUNK - 2, 0)
    band_ok = (kj < 2 * QBLK) & (qc - key_chunk >= 0) & (qc - key_chunk <= WINDOW // CHUNK)
    meta_ok = (kj >= 2 * QBLK) & (kj < 2 * QBLK + n_meta)
    later = band_ok | meta_ok
    first = (band_ok & (kj >= QBLK)) | meta_ok
    only_meta = np.broadcast_to(meta_ok, later.shape)
    bias_p = jnp.asarray(np.where(np.stack([first, later, only_meta]), 0.0, NEG_INF).astype(np.float32))
    n_keys_s = n_meta + WINDOW + dec_s
    bias_s = jnp.asarray(np.where(np.broadcast_to(kj < n_keys_s, later.shape), 0.0, NEG_INF)
                         .astype(np.float32))[None]

    meta_rows = jnp.concatenate([meta_tokens.astype(F32), jnp.zeros((META_BLK - n_meta, d), F32)], 0)
    x = jnp.concatenate([x_prompt.reshape(t_real, d), x_sample.reshape(s_rows, d),
                         jnp.tile(meta_rows, (bsz, 1))], 0)
    xb = x.astype(BF16)

    n_assign = (t_real + s_rows + bsz * n_meta) * TOPK
    n_rows = -(-(n_assign + n_experts * (MOE_BLK - 1)) // MOE_BLK) * MOE_BLK
    spare_np = n_rows + (np.cumsum(~valid_np) - 1)[:, None] * TOPK + np.arange(TOPK)[None, :]
    spare_rows = jnp.asarray(np.where(valid_np[:, None], 0, spare_np).astype(np.int32))
    n_total = n_rows + int((~valid_np).sum()) * TOPK

    outs = {k: [] for k in ("mk", "mv", "wk", "wv", "ks", "vs", "cp", "hp", "cs", "hs")}
    for l in range(depth):
        idx = l // 2
        if l % 2 == 0:
            w_in = attn_w_in[idx].astype(BF16)
            qkv = _qkv_proj(xb, w_in, cos_t, sin_lo_t, sin_hi_t, q_w + kv_w)
            k_col, v_col = q_w // kv_w, q_w // kv_w + 1
            sink_rows = jnp.repeat(attn_sink[idx].astype(F32).reshape(n_kv, 1, GROUP), QBLK, axis=2)

            q_s = qkv[s_base:m_base, :q_w].reshape(dec_b, dec_s, q_w)
            q_s = jnp.pad(q_s, ((0, 0), (0, QBLK - dec_s), (0, 0))).reshape(dec_b * QBLK, q_w)
            k_new = qkv[s_base:m_base, q_w:q_w + kv_w].reshape(dec_b, dec_s, kv_w)
            v_new = qkv[s_base:m_base, q_w + kv_w:].reshape(dec_b, dec_s, kv_w)
            pad_k = jnp.zeros((dec_b, 3 * QBLK - n_keys_s, kv_w), F32)
            k_s = jnp.concatenate([cache_meta_k[idx].reshape(dec_b, n_meta, kv_w).astype(F32),
                                   cache_win_k[idx].reshape(dec_b, WINDOW, kv_w).astype(F32), k_new, pad_k], 1)
            v_s = jnp.concatenate([cache_meta_v[idx].reshape(dec_b, n_meta, kv_w).astype(F32),
                                   cache_win_v[idx].reshape(dec_b, WINDOW, kv_w).astype(F32), v_new, pad_k], 1)
            kv_s = jnp.concatenate([k_s, v_s], axis=2).reshape(dec_b * 3 * QBLK, 2 * kv_w)
            o_s = _attention(q_s, kv_s, dec_b, lambda i: i,
                             [lambda i: 3 * i, lambda i: 3 * i + 1, lambda i: 3 * i + 2], bias_s, lambda i: 0,
                             sink_rows, jnp.zeros((dec_b * QBLK, q_w), BF16), lambda i: i, q_w, kv_w, 0, 1,
                             "attn_sample")
            o_s = o_s.reshape(dec_b, QBLK, q_w)[:, :dec_s].reshape(s_rows, q_w)

            steps = nqb + 1
            is_meta = lambda i: (i % steps) == nqb
            bidx = lambda i: i // steps
            pblk = lambda i: i % steps
            frame_blk = lambda i: bidx(i) * nqb + jnp.minimum(pblk(i), nqb - 1)
            meta_blk = lambda i: m_base // QBLK + bidx(i)
            q_map = lambda i: jnp.where(is_meta(i), meta_blk(i), frame_blk(i))
            prev_map = lambda i: bidx(i) * nqb + jnp.clip(pblk(i) - 1, 0, nqb - 1)
            bias_map = lambda i: jnp.where(is_meta(i), 2, jnp.minimum(pblk(i), 1))
            o_init = jnp.concatenate([jnp.zeros((t_real, q_w), BF16), o_s,
                                      jnp.zeros((t_pad - m_base, q_w), BF16)], 0)
            o = _attention(qkv, qkv, bsz * steps, q_map, [prev_map, frame_blk, meta_blk], bias_p, bias_map,
                           sink_rows, o_init, q_map, q_w, kv_w, k_col, v_col, "attn_prompt")

            kp = qkv[:t_real, q_w:q_w + kv_w].reshape(bsz, seq, n_kv, HEAD_DIM)
            vp = qkv[:t_real, q_w + kv_w:].reshape(bsz, seq, n_kv, HEAD_DIM)
            km = qkv[m_base:, q_w:q_w + kv_w].reshape(bsz, META_BLK, n_kv, HEAD_DIM)[:, :n_meta]
            vm = qkv[m_base:, q_w + kv_w:].reshape(bsz, META_BLK, n_kv, HEAD_DIM)[:, :n_meta]
            outs["mk"].append(km); outs["mv"].append(vm)
            outs["wk"].append(kp[:, -WINDOW:]); outs["wv"].append(vp[:, -WINDOW:])
            outs["ks"].append(k_new.reshape(dec_b, dec_s, n_kv, HEAD_DIM))
            outs["vs"].append(v_new.reshape(dec_b, dec_s, n_kv, HEAD_DIM))
            w_out = attn_w_out[idx].astype(BF16)
        else:
            u = _matmul(xb, lru_w_in[idx].astype(BF16), 1024, "lru_in_proj")
            sp = jax.nn.softplus(-lru_lambda[idx].astype(F32)).reshape(1, c_rnn)
            wts = (lru_conv_w[idx].astype(F32), lru_conv_b[idx].astype(F32).reshape(1, c_rnn),
                   lru_gate_a_w[idx].astype(BF16), lru_gate_a_b[idx].astype(F32).reshape(1, c_rnn),
                   lru_gate_x_w[idx].astype(BF16), lru_gate_x_b[idx].astype(F32).reshape(1, c_rnn), sp)
            zc = jnp.zeros((bsz, CONV_W - 1, c_rnn), F32)
            zh = jnp.zeros((bsz, 1, c_rnn), F32)
            y_m, c_m, h_m = _lru_seq(u, bsz, 1, META_BLK, m_base, META_BLK, n_meta,
                                     jnp.zeros((bsz * META_BLK, c_rnn), BF16), 0, zc, zh, wts)
            y_s, c_s, h_s = _lru_seq(u, dec_b, 1, dec_s, s_base, dec_s, dec_s,
                                     jnp.zeros((s_rows, c_rnn), BF16), 0,
                                     state_conv[idx].astype(F32), state_h[idx].astype(F32).reshape(dec_b, 1, c_rnn),
                                     wts)
            o_init = jnp.concatenate([jnp.zeros((t_real, c_rnn), BF16), y_s, y_m], 0)
            o, c_p, h_p = _lru_seq(u, bsz, seq // LRU_TILE, LRU_TILE, 0, seq, LRU_TILE, o_init, 0,
                                   c_m, h_m, wts)
            outs["cp"].append(c_p); outs["hp"].append(h_p.reshape(bsz, c_rnn))
            outs["cs"].append(c_s); outs["hs"].append(h_s.reshape(dec_b, c_rnn))
            w_out = lru_w_out[idx].astype(BF16)

        ne_pad = -(-n_experts // LANES) * LANES
        wr = jnp.pad(moe_router_w[l].astype(F32), ((0, 0), (0, ne_pad - n_experts)))
        wr_hi, wr_lo = _split_bf16(wr)
        x1, x1b, x1p, logits = _proj_ln(o, w_out, x, ln_g[l, 0].astype(F32).reshape(1, d),
                                   ln_b[l, 0].astype(F32).reshape(1, d), wr_hi, wr_lo, alpha)

        eidx, gate = _route(logits[:, :n_experts], moe_router_bias[l], valid, n_experts)
        dest, block_expert, n_used = _dispatch_plan(eidx, valid, spare_rows, n_experts, n_rows)
        x_sorted = _dispatch(x1p, dest.reshape(-1), n_total)
        y_sorted = _moe_experts(x_sorted, moe_w1, moe_w3, moe_w2, l, block_expert, n_used, n_rows)
        dest_c = jnp.where(valid[:, None], dest, 0)
        routed = jnp.sum(jnp.take(y_sorted, dest_c, axis=0) * gate[:, :, None], axis=1)

        w13 = jnp.concatenate([moe_shared_w1[l], moe_shared_w3[l]], axis=1).astype(BF16)
        x, xb = _shared_ln(x1b, x1, routed, w13, moe_shared_w2[l].astype(BF16),
                           ln_g[l, 1].astype(F32).reshape(1, d), ln_b[l, 1].astype(F32).reshape(1, d), alpha)

    y_prompt = x[:t_real].reshape(bsz, seq, d)
    y_sample = x[s_base:m_base].reshape(dec_b, dec_s, d)
    st = lambda k: jnp.stack(outs[k])
    return (y_prompt, y_sample, st("mk"), st("mv"), st("wk"), st("wv"), st("ks"), st("vs"),
            st("cp"), st("hp"), st("cs"), st("hs"))
```

```python
import functools
import math

import numpy as np
import jax
import jax.numpy as jnp
from jax import lax
from jax.experimental import pallas as pl
from jax.experimental.pallas import tpu as pltpu

F32 = jnp.float32
BF16 = jnp.bfloat16

PAST_LEN = 1024
CHUNK = 64
WINDOW = 128
HEAD_DIM = 64
GROUP = 8
ROT_DIM = HEAD_DIM // 4
ROPE_THETA = 500000.0
ATTN_SCALE = HEAD_DIM ** -0.5
N_LRU_BLOCKS = 8
CONV_W = 4
LRU_C = 8.0
N_GROUPS = 8
TOPK_GROUPS = 4
TOPK = 8
ROUTED_SCALE = 2.5
LN_EPS = 1e-5
NEG_INF = -1e30

LANES = 128
SUBLANES = 8
QBLK = 2 * CHUNK
META_BLK = 128
ROW_TILE = 512
LN_TILE = 256
LRU_TILE = 256
SCAN_LANES = 512
MOE_BLK = 256
VMEM_LIMIT = 56 * 1024 * 1024


def _cparams(*sem):
    return pltpu.CompilerParams(dimension_semantics=sem, vmem_limit_bytes=VMEM_LIMIT)


def _mm_kernel(x_ref, w_ref, o_ref):
    o_ref[...] = jnp.dot(x_ref[...], w_ref[...], preferred_element_type=F32)


def _matmul(x, w, tn, name):
    m, k = x.shape
    n = w.shape[1]
    tm = ROW_TILE
    return pl.pallas_call(
        _mm_kernel,
        grid=(n // tn, m // tm),
        in_specs=[pl.BlockSpec((tm, k), lambda j, i: (i, 0)),
                  pl.BlockSpec((k, tn), lambda j, i: (0, j))],
        out_specs=pl.BlockSpec((tm, tn), lambda j, i: (i, j)),
        out_shape=jax.ShapeDtypeStruct((m, n), F32),
        compiler_params=_cparams("parallel", "parallel"),
        name=name,
    )(x, w)


def _qkv_kernel(x_ref, w_ref, c_ref, s1_ref, s2_ref, o_ref, *, tn, rope_cols):
    j = pl.program_id(0)
    acc = jnp.dot(x_ref[...], w_ref[...], preferred_element_type=F32)
    c, s1, s2 = c_ref[...], s1_ref[...], s2_ref[...]
    lane = lax.broadcasted_iota(jnp.int32, (acc.shape[0], LANES), 1)
    for g in range(tn // LANES):
        a = acc[:, g * LANES:(g + 1) * LANES]
        lo = pltpu.roll(a, ROT_DIM // 2, 1)
        hi = pltpu.roll(a, LANES - ROT_DIM // 2, 1)
        roped = a * c + lo * s1 + hi * s2
        col = j * tn + g * LANES + lane
        o_ref[:, g * LANES:(g + 1) * LANES] = jnp.where(col < rope_cols, roped, a)


def _qkv_proj(xb, w, cos_t, sin_lo_t, sin_hi_t, rope_cols):
    m, k = xb.shape
    n = w.shape[1]
    tm = ROW_TILE
    tn = 512 if n % 512 == 0 else 256
    assert n % tn == 0 and m % tm == 0
    tab = pl.BlockSpec((tm, LANES), lambda j, i: (i, 0))
    return pl.pallas_call(
        functools.partial(_qkv_kernel, tn=tn, rope_cols=rope_cols),
        grid=(n // tn, m // tm),
        in_specs=[pl.BlockSpec((tm, k), lambda j, i: (i, 0)),
                  pl.BlockSpec((k, tn), lambda j, i: (0, j)),
                  tab, tab, tab],
        out_specs=pl.BlockSpec((tm, tn), lambda j, i: (i, j)),
        out_shape=jax.ShapeDtypeStruct((m, n), F32),
        compiler_params=_cparams("parallel", "parallel"),
        name="qkv_rope",
    )(xb, w, cos_t, sin_lo_t, sin_hi_t)


def _layer_norm_rows(z, g, b):
    mu = jnp.mean(z, -1, keepdims=True)
    d = z - mu
    var = jnp.mean(d * d, -1, keepdims=True)
    return d * lax.rsqrt(var + LN_EPS) * g + b


def _proj_ln_kernel(a_ref, w_ref, x_ref, g_ref, b_ref, wrh_ref, wrl_ref, o_ref, ob_ref, lg_ref, *, alpha):
    acc = jnp.dot(a_ref[...], w_ref[...], preferred_element_type=F32)
    y = _layer_norm_rows(alpha * x_ref[...] + acc, g_ref[...], b_ref[...])
    o_ref[...] = y
    yb = y.astype(BF16)
    ob_ref[...] = yb
    ylo = (y - yb.astype(F32)).astype(BF16)
    wrh = wrh_ref[...]
    lg_ref[...] = (jnp.dot(yb, wrh, preferred_element_type=F32)
                   + jnp.dot(ylo, wrh, preferred_element_type=F32)
                   + jnp.dot(yb, wrl_ref[...], preferred_element_type=F32))


def _proj_ln(a, w, x, g, b, wr_hi, wr_lo, alpha):
    m, k = a.shape
    d = w.shape[1]
    ne = wr_hi.shape[1]
    tm = LN_TILE
    row = lambda i: (i, 0)
    fix = lambda i: (0, 0)
    return pl.pallas_call(
        functools.partial(_proj_ln_kernel, alpha=alpha),
        grid=(m // tm,),
        in_specs=[pl.BlockSpec((tm, k), row), pl.BlockSpec((k, d), fix), pl.BlockSpec((tm, d), row),
                  pl.BlockSpec((1, d), fix), pl.BlockSpec((1, d), fix),
                  pl.BlockSpec((d, ne), fix), pl.BlockSpec((d, ne), fix)],
        out_specs=[pl.BlockSpec((tm, d), row), pl.BlockSpec((tm, d), row), pl.BlockSpec((tm, ne), row)],
        out_shape=[jax.ShapeDtypeStruct((m, d), F32), jax.ShapeDtypeStruct((m, d), BF16),
                   jax.ShapeDtypeStruct((m, ne), F32)],
        compiler_params=_cparams("parallel"),
        name="proj_ln_router",
    )(a, w, x, g, b, wr_hi, wr_lo)


def _silu(x):
    return x * (1.0 / (1.0 + jnp.exp(-x)))


def _shared_ln_kernel(xb_ref, x_ref, r_ref, w13_ref, w2_ref, g_ref, b_ref, o_ref, ob_ref, *, alpha, dh):
    u = jnp.dot(xb_ref[...], w13_ref[...], preferred_element_type=F32)
    hs = (_silu(u[:, :dh]) * u[:, dh:]).astype(BF16)
    sh = jnp.dot(hs, w2_ref[...], preferred_element_type=F32)
    y = _layer_norm_rows(alpha * x_ref[...] + (r_ref[...] + sh), g_ref[...], b_ref[...])
    o_ref[...] = y
    ob_ref[...] = y.astype(BF16)


def _shared_ln(xb, x, routed, w13, w2, g, b, alpha):
    m, d = x.shape
    dh = w2.shape[0]
    tm = LN_TILE
    row = lambda i: (i, 0)
    fix = lambda i: (0, 0)
    return pl.pallas_call(
        functools.partial(_shared_ln_kernel, alpha=alpha, dh=dh),
        grid=(m // tm,),
        in_specs=[pl.BlockSpec((tm, d), row), pl.BlockSpec((tm, d), row), pl.BlockSpec((tm, d), row),
                  pl.BlockSpec((d, 2 * dh), fix), pl.BlockSpec((dh, d), fix),
                  pl.BlockSpec((1, d), fix), pl.BlockSpec((1, d), fix)],
        out_specs=[pl.BlockSpec((tm, d), row), pl.BlockSpec((tm, d), row)],
        out_shape=[jax.ShapeDtypeStruct((m, d), F32), jax.ShapeDtypeStruct((m, d), BF16)],
        compiler_params=_cparams("parallel"),
        name="shared_ln",
    )(xb, x, routed, w13, w2, g, b)


def _attn_kernel(q_ref, k0_ref, k1_ref, k2_ref, v0_ref, v1_ref, v2_ref, bias_ref, sink_ref, init_ref, o_ref,
                 ot_ref, *, n_kv):
    del init_ref
    hd = HEAD_DIM
    qt = (q_ref[...] * ATTN_SCALE).T.astype(BF16)
    kb = jnp.concatenate([k0_ref[...], k1_ref[...], k2_ref[...]], 0).astype(BF16)
    vt = jnp.concatenate([v0_ref[...], v1_ref[...], v2_ref[...]], 0).T.astype(BF16)
    bias = bias_ref[0]
    bias = jnp.concatenate([bias] * GROUP, axis=1)
    for h in range(n_kv):
        kh = kb[:, h * hd:(h + 1) * hd]
        qth = jnp.concatenate(
            [qt[(h * GROUP + g) * hd:(h * GROUP + g + 1) * hd, :] for g in range(GROUP)], axis=1)
        s = jnp.dot(kh, qth, preferred_element_type=F32) + bias
        sink = sink_ref[h]
        m = jnp.maximum(jnp.max(s, axis=0, keepdims=True), sink)
        p = jnp.exp(s - m)
        den = jnp.sum(p, axis=0, keepdims=True) + jnp.exp(sink - m)
        ot = jnp.dot(vt[h * hd:(h + 1) * hd, :], p.astype(BF16), preferred_element_type=F32)
        ot = ot * (1.0 / den)
        for g in range(GROUP):
            r0 = (h * GROUP + g) * hd
            ot_ref[r0:r0 + hd, :] = ot[:, g * QBLK:(g + 1) * QBLK]
    o_ref[...] = ot_ref[...].T.astype(BF16)


def _attention(q_arr, k_arr, n_steps, q_map, kv_maps, bias, bias_map, sink_rows, init, out_map,
               q_w, kv_w, k_col, v_col, name):
    n_kv = kv_w // HEAD_DIM
    kspecs = [pl.BlockSpec((QBLK, kv_w), (lambda i, f=f: (f(i), k_col))) for f in kv_maps]
    vspecs = [pl.BlockSpec((QBLK, kv_w), (lambda i, f=f: (f(i), v_col))) for f in kv_maps]
    nk = 3 * QBLK
    return pl.pallas_call(
        functools.partial(_attn_kernel, n_kv=n_kv),
        grid=(n_steps,),
        in_specs=[pl.BlockSpec((QBLK, q_w), lambda i: (q_map(i), 0))] + kspecs + vspecs + [
            pl.BlockSpec((1, nk, QBLK), lambda i: (bias_map(i), 0, 0)),
            pl.BlockSpec((n_kv, 1, GROUP * QBLK), lambda i: (0, 0, 0)),
            pl.BlockSpec(memory_space=pl.ANY)],
        out_specs=pl.BlockSpec((QBLK, q_w), lambda i: (out_map(i), 0)),
        out_shape=jax.ShapeDtypeStruct(init.shape, BF16),
        scratch_shapes=[pltpu.VMEM((q_w, QBLK), F32)],
        input_output_aliases={9: 0},
        compiler_params=_cparams("parallel"),
        name=name,
    )(q_arr, k_arr, k_arr, k_arr, k_arr, k_arr, k_arr, bias, sink_rows, init)


def _gelu_tanh(x):
    c = math.sqrt(2.0 / math.pi)
    return x * (0.5 * (1.0 + jnp.tanh(c * (x + 0.044715 * (x * x * x)))))


def _sigmoid(x):
    return 1.0 / (1.0 + jnp.exp(-x))


def _lru_kernel(xb_ref, gb_ref, cprev_ref, hprev_ref, cw_ref, cb_ref, gaw_ref, gab_ref, gxw_ref, gxb_ref,
                sp_ref, init_ref, y_ref, cnew_ref, hnew_ref, xcat, a_scr, b_scr, hcar, *, tt, valid_last):
    del init_ref
    j = pl.program_id(1)
    c_rnn = xb_ref.shape[1]
    bw = c_rnn // N_LRU_BLOCKS
    tail = SUBLANES

    @pl.when(j == 0)
    def _():
        xcat[0:tail, :] = jnp.zeros((tail, c_rnn), F32)
        xcat[tail - (CONV_W - 1):tail, :] = cprev_ref[...]
        hcar[...] = jnp.broadcast_to(hprev_ref[...], (SUBLANES, c_rnn))

    xcat[tail:tail + tt, :] = xb_ref[...]
    cw = cw_ref[...]
    xc = cb_ref[...]
    for tap in range(CONV_W):
        off = tail - (CONV_W - 1) + tap
        xc = xc + xcat[off:off + tt, :] * cw[tap:tap + 1, :]
    cnew_ref[...] = xcat[tail + valid_last - (CONV_W - 1):tail + valid_last, :]
    xcat[0:tail, :] = xcat[tt:tt + tail, :]

    xcb = xc.astype(BF16)
    rs, gs = [], []
    for n in range(N_LRU_BLOCKS):
        xs = xcb[:, n * bw:(n + 1) * bw]
        rs.append(jnp.dot(xs, gaw_ref[n], preferred_element_type=F32))
        gs.append(jnp.dot(xs, gxw_ref[n], preferred_element_type=F32))
    r = _sigmoid(jnp.concatenate(rs, axis=1) + gab_ref[...])
    gi = _sigmoid(jnp.concatenate(gs, axis=1) + gxb_ref[...])
    log_a = (-LRU_C * r) * sp_ref[...]
    a = jnp.exp(log_a)
    a_scr[...] = a
    b_scr[...] = jnp.sqrt(-jnp.tanh(log_a) * (a * a + 1.0)) * (gi * xc)

    row = lax.broadcasted_iota(jnp.int32, (SUBLANES, SCAN_LANES), 0)
    for c in range(c_rnn // SCAN_LANES):
        cs = slice(c * SCAN_LANES, (c + 1) * SCAN_LANES)

        def body(i, carry, cs=cs):
            r0 = pl.multiple_of(i * SUBLANES, SUBLANES)
            av = a_scr[pl.ds(r0, SUBLANES), cs]
            bv = b_scr[pl.ds(r0, SUBLANES), cs]
            for k in (1, 2, 4):
                a_sh = jnp.where(row >= k, pltpu.roll(av, k, 0), 1.0)
                b_sh = jnp.where(row >= k, pltpu.roll(bv, k, 0), 0.0)
                bv = av * b_sh + bv
                av = av * a_sh
            hv = av * carry + bv
            b_scr[pl.ds(r0, SUBLANES), cs] = hv
            return jnp.broadcast_to(hv[SUBLANES - 1:SUBLANES, :], (SUBLANES, SCAN_LANES))

        hcar[:, cs] = lax.fori_loop(0, tt // SUBLANES, body, hcar[:, cs])

    h = b_scr[...]
    hnew_ref[...] = b_scr[valid_last - 1:valid_last, :]
    y_ref[...] = (h * _gelu_tanh(gb_ref[...])).astype(BF16)


def _lru_seq(u, n_seq, n_tiles, tt, row0, seq_stride, valid_last, init, out_row0, cprev, hprev, wts):
    c_rnn = u.shape[1] // 2
    cw, cb, gaw, gab, gxw, gxb, sp = wts
    b0, bs, ob0 = row0 // tt, seq_stride // tt, out_row0 // tt
    bw = c_rnn // N_LRU_BLOCKS
    fix2 = lambda s, j: (0, 0)
    fix3 = lambda s, j: (0, 0, 0)
    per_seq = lambda s, j: (s, 0, 0)
    return pl.pallas_call(
        functools.partial(_lru_kernel, tt=tt, valid_last=valid_last),
        grid=(n_seq, n_tiles),
        in_specs=[pl.BlockSpec((tt, c_rnn), lambda s, j: (b0 + s * bs + j, 0)),
                  pl.BlockSpec((tt, c_rnn), lambda s, j: (b0 + s * bs + j, 1)),
                  pl.BlockSpec((None, CONV_W - 1, c_rnn), per_seq),
                  pl.BlockSpec((None, 1, c_rnn), per_seq),
                  pl.BlockSpec((CONV_W, c_rnn), fix2), pl.BlockSpec((1, c_rnn), fix2),
                  pl.BlockSpec((N_LRU_BLOCKS, bw, bw), fix3), pl.BlockSpec((1, c_rnn), fix2),
                  pl.BlockSpec((N_LRU_BLOCKS, bw, bw), fix3), pl.BlockSpec((1, c_rnn), fix2),
                  pl.BlockSpec((1, c_rnn), fix2), pl.BlockSpec(memory_space=pl.ANY)],
        out_specs=[pl.BlockSpec((tt, c_rnn), lambda s, j: (ob0 + s * bs + j, 0)),
                   pl.BlockSpec((None, CONV_W - 1, c_rnn), per_seq),
                   pl.BlockSpec((None, 1, c_rnn), per_seq)],
        out_shape=[jax.ShapeDtypeStruct(init.shape, BF16),
                   jax.ShapeDtypeStruct((n_seq, CONV_W - 1, c_rnn), F32),
                   jax.ShapeDtypeStruct((n_seq, 1, c_rnn), F32)],
        scratch_shapes=[pltpu.VMEM((SUBLANES + tt, c_rnn), F32), pltpu.VMEM((tt, c_rnn), F32),
                        pltpu.VMEM((tt, c_rnn), F32), pltpu.VMEM((SUBLANES, c_rnn), F32)],
        input_output_aliases={11: 0},
        compiler_params=_cparams("parallel", "arbitrary"),
        name="rglru_seq",
    )(u, u, cprev, hprev, cw, cb, gaw, gab, gxw, gxb, sp, init)


def _moe_kernel(be_ref, nu_ref, x_ref, w1_ref, w3_ref, w2_ref, o_ref, w1b, w3b, w2b):
    i = pl.program_id(0)

    @pl.when(i < nu_ref[0])
    def _():
        prev = be_ref[jnp.maximum(i - 1, 0)]

        @pl.when(jnp.logical_or(i == 0, be_ref[i] != prev))
        def _():
            w1b[...] = w1_ref[...].astype(BF16)
            w3b[...] = w3_ref[...].astype(BF16)
            w2b[...] = w2_ref[...].astype(BF16)

        x = x_ref[...]
        a = jnp.dot(x, w1b[...], preferred_element_type=F32)
        b = jnp.dot(x, w3b[...], preferred_element_type=F32)
        h = (_silu(a) * b).astype(BF16)
        o_ref[...] = jnp.dot(h, w2b[...], preferred_element_type=F32)


def _moe_experts(x_sorted, w1, w3, w2, layer, block_expert, n_used):
    n_rows, hw = x_sorted.shape
    d, de = w1.shape[2], w1.shape[3]
    nb = n_rows // MOE_BLK
    blk = lambda i, be, nu: (jnp.minimum(i, nu[0] - 1), 0)
    wmap = lambda i, be, nu: (layer, be[jnp.minimum(i, nu[0] - 1)], 0, 0)
    grid_spec = pltpu.PrefetchScalarGridSpec(
        num_scalar_prefetch=2,
        grid=(nb,),
        in_specs=[pl.BlockSpec((MOE_BLK, hw), blk),
                  pl.BlockSpec((None, None, d, de), wmap), pl.BlockSpec((None, None, d, de), wmap),
                  pl.BlockSpec((None, None, de, d), wmap)],
        out_specs=pl.BlockSpec((MOE_BLK, d), blk),
        scratch_shapes=[pltpu.VMEM((d, de), BF16), pltpu.VMEM((d, de), BF16), pltpu.VMEM((de, d), BF16)],
    )
    return pl.pallas_call(
        _moe_kernel,
        grid_spec=grid_spec,
        out_shape=jax.ShapeDtypeStruct((n_rows, d), F32),
        compiler_params=_cparams("arbitrary"),
        name="moe_experts",
    )(block_expert, n_used, x_sorted, w1, w3, w2)


def _route(logits, router_bias, valid, n_experts):
    t = logits.shape[0]
    per = n_experts // N_GROUPS
    scores = jax.nn.sigmoid(logits)
    biased = scores + router_bias.astype(F32)
    grp = biased.reshape(t, N_GROUPS, per)
    m1 = jnp.max(grp, -1, keepdims=True)
    is_max = grp == m1
    first = is_max & (jnp.cumsum(is_max.astype(jnp.int32), -1) == 1)
    m2 = jnp.max(jnp.where(first, -jnp.inf, grp), -1)
    grp_score = m1[..., 0] + m2

    def rank(v):
        n = v.shape[-1]
        idx = jnp.arange(n)
        vi, vj = v[:, :, None], v[:, None, :]
        beaten = (vj > vi) | ((vj == vi) & (idx[None, None, :] < idx[None, :, None]))
        return jnp.sum(beaten.astype(jnp.int32), -1)

    gsel = rank(grp_score) < TOPK_GROUPS
    emask = jnp.repeat(gsel, per, axis=-1)
    e_rank = rank(jnp.where(emask, biased, NEG_INF))
    slot = e_rank[:, None, :] == jnp.arange(TOPK)[None, :, None]
    eidx = jnp.sum(jnp.where(slot, jnp.arange(n_experts)[None, None, :], 0), -1)
    g = jnp.sum(jnp.where(slot, scores[:, None, :], 0.0), -1)
    g = g / (jnp.sum(g, -1, keepdims=True) + 1e-20) * ROUTED_SCALE
    g = jnp.where(valid[:, None], g, 0.0)
    member = ((e_rank < TOPK) & valid[:, None]).astype(jnp.int32)
    return eidx.astype(jnp.int32), g, member


def _dispatch_plan(eidx, member, valid, n_experts, n_rows):
    t = eidx.shape[0]
    tk = t * TOPK
    flat_e = jnp.where(valid[:, None], eidx, n_experts).reshape(tk)
    order = jnp.argsort(flat_e).astype(jnp.int32)
    tok_sorted = order // TOPK
    cum_excl = jnp.cumsum(member, axis=0) - member
    counts = jnp.sum(member, axis=0)
    padded = (counts + MOE_BLK - 1) // MOE_BLK * MOE_BLK
    pad_end = jnp.cumsum(padded)
    pad_start = pad_end - padded
    grp_start = jnp.cumsum(counts) - counts
    dest = pad_start[eidx] + jnp.take_along_axis(cum_excl, eidx, axis=1)
    dest = jnp.where(valid[:, None], dest, 0).astype(jnp.int32)
    rows = jnp.arange(n_rows, dtype=jnp.int32)
    row_e = jnp.minimum(jnp.sum((pad_end[None, :] <= rows[:, None]).astype(jnp.int32), axis=1), n_experts - 1)
    within = rows - pad_start[row_e]
    src = jnp.clip(grp_start[row_e] + within, 0, tk - 1)
    row_tok = jnp.where(within < counts[row_e], tok_sorted[src], 0).astype(jnp.int32)
    block_expert = row_e[::MOE_BLK].astype(jnp.int32)
    n_used = jnp.maximum(pad_end[-1] // MOE_BLK, 1).astype(jnp.int32).reshape(1)
    return row_tok, dest, block_expert, n_used


def _split_bf16(w):
    hi = w.astype(BF16)
    lo = (w - hi.astype(F32)).astype(BF16)
    return hi, lo


def kernel(x_prompt, x_sample, cache_meta_k, cache_meta_v, cache_win_k, cache_win_v, state_conv, state_h, meta_tokens, ln_g, ln_b, attn_w_in, attn_w_out, attn_sink, lru_w_in, lru_conv_w, lru_conv_b, lru_gate_a_w, lru_gate_a_b, lru_gate_x_w, lru_gate_x_b, lru_lambda, lru_w_out, moe_router_w, moe_router_bias, moe_w1, moe_w3, moe_w2, moe_shared_w1, moe_shared_w3, moe_shared_w2):
    bsz, seq, d = x_prompt.shape
    dec_b, dec_s, _ = x_sample.shape
    n_meta = meta_tokens.shape[0]
    depth = ln_g.shape[0]
    n_experts = moe_router_w.shape[2]
    past_len = PAST_LEN
    n_heads = d // HEAD_DIM
    n_kv = n_heads // GROUP
    q_w, kv_w = n_heads * HEAD_DIM, n_kv * HEAD_DIM
    c_rnn = lru_w_in.shape[2] // 2
    alpha = (2 * depth) ** 0.25

    t_real = bsz * seq
    s_base, s_rows = t_real, dec_b * dec_s
    m_base = s_base + s_rows
    t_pad = m_base + bsz * META_BLK
    assert seq % LRU_TILE == 0 and seq % QBLK == 0 and s_rows % QBLK == 0 and t_pad % ROW_TILE == 0
    assert n_meta <= META_BLK and dec_s <= QBLK and q_w == d and kv_w % LANES == 0
    assert cache_win_k.shape[2] == WINDOW and WINDOW == QBLK and n_meta + WINDOW + dec_s <= 3 * QBLK
    nqb = seq // QBLK

    pos = np.zeros((t_pad,), np.float32)
    pos[:t_real] = np.tile(np.arange(seq) + n_meta, bsz)
    pos[s_base:m_base] = np.tile(past_len + n_meta + np.arange(dec_s), dec_b)
    valid_np = np.zeros((t_pad,), bool)
    valid_np[:m_base] = True
    for b in range(bsz):
        pos[m_base + b * META_BLK:m_base + b * META_BLK + n_meta] = np.arange(n_meta)
        valid_np[m_base + b * META_BLK:m_base + b * META_BLK + n_meta] = True
    valid = jnp.asarray(valid_np)

    half = ROT_DIM // 2
    freqs = ROPE_THETA ** (-jnp.arange(0, ROT_DIM, 2, dtype=F32) / ROT_DIM)
    ang = jnp.asarray(pos)[:, None] * freqs[None, :]
    cos, sin = jnp.cos(ang), jnp.sin(ang)
    ones = jnp.ones((t_pad, HEAD_DIM - ROT_DIM), F32)
    zeros_h = jnp.zeros((t_pad, half), F32)
    zeros_r = jnp.zeros((t_pad, HEAD_DIM - ROT_DIM), F32)
    reps = LANES // HEAD_DIM
    cos_t = jnp.tile(jnp.concatenate([cos, cos, ones], 1), (1, reps))
    sin_lo_t = jnp.tile(jnp.concatenate([zeros_h, sin, zeros_r], 1), (1, reps))
    sin_hi_t = jnp.tile(jnp.concatenate([-sin, zeros_h, zeros_r], 1), (1, reps))

    kj = np.arange(3 * QBLK)[:, None]
    qc = np.arange(QBLK)[None, :] // CHUNK
    key_chunk = np.where(kj < 2 * QBLK, kj // CHUNK - 2, 0)
    band_ok = (kj < 2 * QBLK) & (qc - key_chunk >= 0) & (qc - key_chunk <= WINDOW // CHUNK)
    meta_ok = (kj >= 2 * QBLK) & (kj < 2 * QBLK + n_meta)
    later = band_ok | meta_ok
    first = (band_ok & (kj >= QBLK)) | meta_ok
    only_meta = np.broadcast_to(meta_ok, later.shape)
    bias_p = jnp.asarray(np.where(np.stack([first, later, only_meta]), 0.0, NEG_INF).astype(np.float32))
    n_keys_s = n_meta + WINDOW + dec_s
    bias_s = jnp.asarray(np.where(np.broadcast_to(kj < n_keys_s, later.shape), 0.0, NEG_INF)
                         .astype(np.float32))[None]

    meta_rows = jnp.concatenate([meta_tokens.astype(F32), jnp.zeros((META_BLK - n_meta, d), F32)], 0)
    x = jnp.concatenate([x_prompt.reshape(t_real, d), x_sample.reshape(s_rows, d),
                         jnp.tile(meta_rows, (bsz, 1))], 0)
    xb = x.astype(BF16)

    n_assign = (t_real + s_rows + bsz * n_meta) * TOPK
    n_rows = -(-(n_assign + n_experts * (MOE_BLK - 1)) // MOE_BLK) * MOE_BLK

    outs = {k: [] for k in ("mk", "mv", "wk", "wv", "ks", "vs", "cp", "hp", "cs", "hs")}
    for l in range(depth):
        idx = l // 2
        if l % 2 == 0:
            w_in = attn_w_in[idx].astype(BF16)
            qkv = _qkv_proj(xb, w_in, cos_t, sin_lo_t, sin_hi_t, q_w + kv_w)
            k_col, v_col = q_w // kv_w, q_w // kv_w + 1
            sink_rows = jnp.repeat(attn_sink[idx].astype(F32).reshape(n_kv, 1, GROUP), QBLK, axis=2)

            q_s = qkv[s_base:m_base, :q_w].reshape(dec_b, dec_s, q_w)
            q_s = jnp.pad(q_s, ((0, 0), (0, QBLK - dec_s), (0, 0))).reshape(dec_b * QBLK, q_w)
            k_new = qkv[s_base:m_base, q_w:q_w + kv_w].reshape(dec_b, dec_s, kv_w)
            v_new = qkv[s_base:m_base, q_w + kv_w:].reshape(dec_b, dec_s, kv_w)
            pad_k = jnp.zeros((dec_b, 3 * QBLK - n_keys_s, kv_w), F32)
            k_s = jnp.concatenate([cache_meta_k[idx].reshape(dec_b, n_meta, kv_w).astype(F32),
                                   cache_win_k[idx].reshape(dec_b, WINDOW, kv_w).astype(F32), k_new, pad_k], 1)
            v_s = jnp.concatenate([cache_meta_v[idx].reshape(dec_b, n_meta, kv_w).astype(F32),
                                   cache_win_v[idx].reshape(dec_b, WINDOW, kv_w).astype(F32), v_new, pad_k], 1)
            kv_s = jnp.concatenate([k_s, v_s], axis=2).reshape(dec_b * 3 * QBLK, 2 * kv_w)
            o_s = _attention(q_s, kv_s, dec_b, lambda i: i,
                             [lambda i: 3 * i, lambda i: 3 * i + 1, lambda i: 3 * i + 2], bias_s, lambda i: 0,
                             sink_rows, jnp.zeros((dec_b * QBLK, q_w), BF16), lambda i: i, q_w, kv_w, 0, 1,
                             "attn_sample")
            o_s = o_s.reshape(dec_b, QBLK, q_w)[:, :dec_s].reshape(s_rows, q_w)

            steps = nqb + 1
            is_meta = lambda i: (i % steps) == nqb
            bidx = lambda i: i // steps
            pblk = lambda i: i % steps
            frame_blk = lambda i: bidx(i) * nqb + jnp.minimum(pblk(i), nqb - 1)
            meta_blk = lambda i: m_base // QBLK + bidx(i)
            q_map = lambda i: jnp.where(is_meta(i), meta_blk(i), frame_blk(i))
            prev_map = lambda i: bidx(i) * nqb + jnp.clip(pblk(i) - 1, 0, nqb - 1)
            bias_map = lambda i: jnp.where(is_meta(i), 2, jnp.minimum(pblk(i), 1))
            o_init = jnp.concatenate([jnp.zeros((t_real, q_w), BF16), o_s,
                                      jnp.zeros((t_pad - m_base, q_w), BF16)], 0)
            o = _attention(qkv, qkv, bsz * steps, q_map, [prev_map, frame_blk, meta_blk], bias_p, bias_map,
                           sink_rows, o_init, q_map, q_w, kv_w, k_col, v_col, "attn_prompt")

            kp = qkv[:t_real, q_w:q_w + kv_w].reshape(bsz, seq, n_kv, HEAD_DIM)
            vp = qkv[:t_real, q_w + kv_w:].reshape(bsz, seq, n_kv, HEAD_DIM)
            km = qkv[m_base:, q_w:q_w + kv_w].reshape(bsz, META_BLK, n_kv, HEAD_DIM)[:, :n_meta]
            vm = qkv[m_base:, q_w + kv_w:].reshape(bsz, META_BLK, n_kv, HEAD_DIM)[:, :n_meta]
            outs["mk"].append(km); outs["mv"].append(vm)
            outs["wk"].append(kp[:, -WINDOW:]); outs["wv"].append(vp[:, -WINDOW:])
            outs["ks"].append(k_new.reshape(dec_b, dec_s, n_kv, HEAD_DIM))
            outs["vs"].append(v_new.reshape(dec_b, dec_s, n_kv, HEAD_DIM))
            w_out = attn_w_out[idx].astype(BF16)
        else:
            u = _matmul(xb, lru_w_in[idx].astype(BF16), 1024, "lru_in_proj")
            sp = jax.nn.softplus(-lru_lambda[idx].astype(F32)).reshape(1, c_rnn)
            wts = (lru_conv_w[idx].astype(F32), lru_conv_b[idx].astype(F32).reshape(1, c_rnn),
                   lru_gate_a_w[idx].astype(BF16), lru_gate_a_b[idx].astype(F32).reshape(1, c_rnn),
                   lru_gate_x_w[idx].astype(BF16), lru_gate_x_b[idx].astype(F32).reshape(1, c_rnn), sp)
            zc = jnp.zeros((bsz, CONV_W - 1, c_rnn), F32)
            zh = jnp.zeros((bsz, 1, c_rnn), F32)
            y_m, c_m, h_m = _lru_seq(u, bsz, 1, META_BLK, m_base, META_BLK, n_meta,
                                     jnp.zeros((bsz * META_BLK, c_rnn), BF16), 0, zc, zh, wts)
            y_s, c_s, h_s = _lru_seq(u, dec_b, 1, dec_s, s_base, dec_s, dec_s,
                                     jnp.zeros((s_rows, c_rnn), BF16), 0,
                                     state_conv[idx].astype(F32), state_h[idx].astype(F32).reshape(dec_b, 1, c_rnn),
                                     wts)
            o_init = jnp.concatenate([jnp.zeros((t_real, c_rnn), BF16), y_s, y_m], 0)
            o, c_p, h_p = _lru_seq(u, bsz, seq // LRU_TILE, LRU_TILE, 0, seq, LRU_TILE, o_init, 0,
                                   c_m, h_m, wts)
            outs["cp"].append(c_p); outs["hp"].append(h_p.reshape(bsz, c_rnn))
            outs["cs"].append(c_s); outs["hs"].append(h_s.reshape(dec_b, c_rnn))
            w_out = lru_w_out[idx].astype(BF16)

        ne_pad = -(-n_experts // LANES) * LANES
        wr = jnp.pad(moe_router_w[l].astype(F32), ((0, 0), (0, ne_pad - n_experts)))
        wr_hi, wr_lo = _split_bf16(wr)
        x1, x1b, logits = _proj_ln(o, w_out, x, ln_g[l, 0].astype(F32).reshape(1, d),
                                   ln_b[l, 0].astype(F32).reshape(1, d), wr_hi, wr_lo, alpha)

        eidx, gate, member = _route(logits[:, :n_experts], moe_router_bias[l], valid, n_experts)
        row_tok, dest, block_expert, n_used = _dispatch_plan(eidx, member, valid, n_experts, n_rows)
        x_sorted = jnp.take(x1b, row_tok, axis=0)
        y_sorted = _moe_experts(x_sorted, moe_w1, moe_w3, moe_w2, l, block_expert, n_used)
        routed = jnp.sum(jnp.take(y_sorted, dest, axis=0) * gate[:, :, None], axis=1)

        w13 = jnp.concatenate([moe_shared_w1[l], moe_shared_w3[l]], axis=1).astype(BF16)
        x, xb = _shared_ln(x1b, x1, routed, w13, moe_shared_w2[l].astype(BF16),
                           ln_g[l, 1].astype(F32).reshape(1, d), ln_b[l, 1].astype(F32).reshape(1, d), alpha)

    y_prompt = x[:t_real].reshape(bsz, seq, d)
    y_sample = x[s_base:m_base].reshape(dec_b, dec_s, d)
    st = lambda k: jnp.stack(outs[k])
    return (y_prompt, y_sample, st("mk"), st("mv"), st("wk"), st("wv"), st("ks"), st("vs"),
            st("cp"), st("hp"), st("cs"), st("hs"))
```

```python
import functools
import math

import numpy as np
import jax
import jax.numpy as jnp
from jax import lax
from jax.experimental import pallas as pl
from jax.experimental.pallas import tpu as pltpu

F32 = jnp.float32
BF16 = jnp.bfloat16

PAST_LEN = 1024
CHUNK = 64
WINDOW = 128
HEAD_DIM = 64
GROUP = 8
ROT_DIM = HEAD_DIM // 4
ROPE_THETA = 500000.0
ATTN_SCALE = HEAD_DIM ** -0.5
N_LRU_BLOCKS = 8
CONV_W = 4
LRU_C = 8.0
N_GROUPS = 8
TOPK_GROUPS = 4
TOPK = 8
ROUTED_SCALE = 2.5
LN_EPS = 1e-5
NEG_INF = -1e30

LANES = 128
SUBLANES = 8
QBLK = 2 * CHUNK
META_BLK = 128
ROW_TILE = 512
LN_TILE = 256
LRU_TILE = 256
SCAN_LANES = 512
MOE_BLK = 256
VMEM_LIMIT = 56 * 1024 * 1024


def _cparams(*sem):
    return pltpu.CompilerParams(dimension_semantics=sem, vmem_limit_bytes=VMEM_LIMIT)


def _mm_kernel(x_ref, w_ref, o_ref):
    o_ref[...] = jnp.dot(x_ref[...], w_ref[...], preferred_element_type=F32)


def _matmul(x, w, tn, name):
    m, k = x.shape
    n = w.shape[1]
    tm = ROW_TILE
    return pl.pallas_call(
        _mm_kernel,
        grid=(n // tn, m // tm),
        in_specs=[pl.BlockSpec((tm, k), lambda j, i: (i, 0)),
                  pl.BlockSpec((k, tn), lambda j, i: (0, j))],
        out_specs=pl.BlockSpec((tm, tn), lambda j, i: (i, j)),
        out_shape=jax.ShapeDtypeStruct((m, n), F32),
        compiler_params=_cparams("parallel", "parallel"),
        name=name,
    )(x, w)


def _qkv_kernel(x_ref, w_ref, c_ref, s1_ref, s2_ref, o_ref, *, tn, rope_cols):
    j = pl.program_id(0)
    acc = jnp.dot(x_ref[...], w_ref[...], preferred_element_type=F32)
    c, s1, s2 = c_ref[...], s1_ref[...], s2_ref[...]
    lane = lax.broadcasted_iota(jnp.int32, (acc.shape[0], LANES), 1)
    for g in range(tn // LANES):
        a = acc[:, g * LANES:(g + 1) * LANES]
        lo = pltpu.roll(a, ROT_DIM // 2, 1)
        hi = pltpu.roll(a, LANES - ROT_DIM // 2, 1)
        roped = a * c + lo * s1 + hi * s2
        col = j * tn + g * LANES + lane
        o_ref[:, g * LANES:(g + 1) * LANES] = jnp.where(col < rope_cols, roped, a)


def _qkv_proj(xb, w, cos_t, sin_lo_t, sin_hi_t, rope_cols):
    m, k = xb.shape
    n = w.shape[1]
    tm = ROW_TILE
    tn = 512 if n % 512 == 0 else 256
    assert n % tn == 0 and m % tm == 0
    tab = pl.BlockSpec((tm, LANES), lambda j, i: (i, 0))
    return pl.pallas_call(
        functools.partial(_qkv_kernel, tn=tn, rope_cols=rope_cols),
        grid=(n // tn, m // tm),
        in_specs=[pl.BlockSpec((tm, k), lambda j, i: (i, 0)),
                  pl.BlockSpec((k, tn), lambda j, i: (0, j)),
                  tab, tab, tab],
        out_specs=pl.BlockSpec((tm, tn), lambda j, i: (i, j)),
        out_shape=jax.ShapeDtypeStruct((m, n), F32),
        compiler_params=_cparams("parallel", "parallel"),
        name="qkv_rope",
    )(xb, w, cos_t, sin_lo_t, sin_hi_t)


def _layer_norm_rows(z, g, b):
    mu = jnp.mean(z, -1, keepdims=True)
    d = z - mu
    var = jnp.mean(d * d, -1, keepdims=True)
    return d * lax.rsqrt(var + LN_EPS) * g + b


def _proj_ln_kernel(a_ref, w_ref, x_ref, g_ref, b_ref, wrh_ref, wrl_ref, o_ref, ob_ref, lg_ref, *, alpha):
    acc = jnp.dot(a_ref[...], w_ref[...], preferred_element_type=F32)
    y = _layer_norm_rows(alpha * x_ref[...] + acc, g_ref[...], b_ref[...])
    o_ref[...] = y
    yb = y.astype(BF16)
    ob_ref[...] = yb
    ylo = (y - yb.astype(F32)).astype(BF16)
    wrh = wrh_ref[...]
    lg_ref[...] = (jnp.dot(yb, wrh, preferred_element_type=F32)
                   + jnp.dot(ylo, wrh, preferred_element_type=F32)
                   + jnp.dot(yb, wrl_ref[...], preferred_element_type=F32))


def _proj_ln(a, w, x, g, b, wr_hi, wr_lo, alpha):
    m, k = a.shape
    d = w.shape[1]
    ne = wr_hi.shape[1]
    tm = LN_TILE
    row = lambda i: (i, 0)
    fix = lambda i: (0, 0)
    return pl.pallas_call(
        functools.partial(_proj_ln_kernel, alpha=alpha),
        grid=(m // tm,),
        in_specs=[pl.BlockSpec((tm, k), row), pl.BlockSpec((k, d), fix), pl.BlockSpec((tm, d), row),
                  pl.BlockSpec((1, d), fix), pl.BlockSpec((1, d), fix),
                  pl.BlockSpec((d, ne), fix), pl.BlockSpec((d, ne), fix)],
        out_specs=[pl.BlockSpec((tm, d), row), pl.BlockSpec((tm, d), row), pl.BlockSpec((tm, ne), row)],
        out_shape=[jax.ShapeDtypeStruct((m, d), F32), jax.ShapeDtypeStruct((m, d), BF16),
                   jax.ShapeDtypeStruct((m, ne), F32)],
        compiler_params=_cparams("parallel"),
        name="proj_ln_router",
    )(a, w, x, g, b, wr_hi, wr_lo)


def _silu(x):
    return x * (1.0 / (1.0 + jnp.exp(-x)))


def _shared_ln_kernel(xb_ref, x_ref, r_ref, w13_ref, w2_ref, g_ref, b_ref, o_ref, ob_ref, *, alpha, dh):
    u = jnp.dot(xb_ref[...], w13_ref[...], preferred_element_type=F32)
    hs = (_silu(u[:, :dh]) * u[:, dh:]).astype(BF16)
    sh = jnp.dot(hs, w2_ref[...], preferred_element_type=F32)
    y = _layer_norm_rows(alpha * x_ref[...] + (r_ref[...] + sh), g_ref[...], b_ref[...])
    o_ref[...] = y
    ob_ref[...] = y.astype(BF16)


def _shared_ln(xb, x, routed, w13, w2, g, b, alpha):
    m, d = x.shape
    dh = w2.shape[0]
    tm = LN_TILE
    row = lambda i: (i, 0)
    fix = lambda i: (0, 0)
    return pl.pallas_call(
        functools.partial(_shared_ln_kernel, alpha=alpha, dh=dh),
        grid=(m // tm,),
        in_specs=[pl.BlockSpec((tm, d), row), pl.BlockSpec((tm, d), row), pl.BlockSpec((tm, d), row),
                  pl.BlockSpec((d, 2 * dh), fix), pl.BlockSpec((dh, d), fix),
                  pl.BlockSpec((1, d), fix), pl.BlockSpec((1, d), fix)],
        out_specs=[pl.BlockSpec((tm, d), row), pl.BlockSpec((tm, d), row)],
        out_shape=[jax.ShapeDtypeStruct((m, d), F32), jax.ShapeDtypeStruct((m, d), BF16)],
        compiler_params=_cparams("parallel"),
        name="shared_ln",
    )(xb, x, routed, w13, w2, g, b)


def _attn_kernel(q_ref, k0_ref, k1_ref, k2_ref, v0_ref, v1_ref, v2_ref, bias_ref, sink_ref, init_ref, o_ref,
                 ot_ref, *, n_kv):
    del init_ref
    hd = HEAD_DIM
    qt = (q_ref[...] * ATTN_SCALE).T.astype(BF16)
    kb = jnp.concatenate([k0_ref[...], k1_ref[...], k2_ref[...]], 0).astype(BF16)
    vt = jnp.concatenate([v0_ref[...], v1_ref[...], v2_ref[...]], 0).T.astype(BF16)
    bias = bias_ref[0]
    bias = jnp.concatenate([bias] * GROUP, axis=1)
    for h in range(n_kv):
        kh = kb[:, h * hd:(h + 1) * hd]
        qth = jnp.concatenate(
            [qt[(h * GROUP + g) * hd:(h * GROUP + g + 1) * hd, :] for g in range(GROUP)], axis=1)
        s = jnp.dot(kh, qth, preferred_element_type=F32) + bias
        sink = sink_ref[h]
        m = jnp.maximum(jnp.max(s, axis=0, keepdims=True), sink)
        p = jnp.exp(s - m)
        den = jnp.sum(p, axis=0, keepdims=True) + jnp.exp(sink - m)
        ot = jnp.dot(vt[h * hd:(h + 1) * hd, :], p.astype(BF16), preferred_element_type=F32)
        ot = ot * (1.0 / den)
        for g in range(GROUP):
            r0 = (h * GROUP + g) * hd
            ot_ref[r0:r0 + hd, :] = ot[:, g * QBLK:(g + 1) * QBLK]
    o_ref[...] = ot_ref[...].T.astype(BF16)


def _attention(q_arr, k_arr, n_steps, q_map, kv_maps, bias, bias_map, sink_rows, init, out_map,
               q_w, kv_w, k_col, v_col, name):
    n_kv = kv_w // HEAD_DIM
    kspecs = [pl.BlockSpec((QBLK, kv_w), (lambda i, f=f: (f(i), k_col))) for f in kv_maps]
    vspecs = [pl.BlockSpec((QBLK, kv_w), (lambda i, f=f: (f(i), v_col))) for f in kv_maps]
    nk = 3 * QBLK
    return pl.pallas_call(
        functools.partial(_attn_kernel, n_kv=n_kv),
        grid=(n_steps,),
        in_specs=[pl.BlockSpec((QBLK, q_w), lambda i: (q_map(i), 0))] + kspecs + vspecs + [
            pl.BlockSpec((1, nk, QBLK), lambda i: (bias_map(i), 0, 0)),
            pl.BlockSpec((n_kv, 1, GROUP * QBLK), lambda i: (0, 0, 0)),
            pl.BlockSpec(memory_space=pl.ANY)],
        out_specs=pl.BlockSpec((QBLK, q_w), lambda i: (out_map(i), 0)),
        out_shape=jax.ShapeDtypeStruct(init.shape, BF16),
        scratch_shapes=[pltpu.VMEM((q_w, QBLK), F32)],
        input_output_aliases={9: 0},
        compiler_params=_cparams("parallel"),
        name=name,
    )(q_arr, k_arr, k_arr, k_arr, k_arr, k_arr, k_arr, bias, sink_rows, init)


def _gelu_tanh(x):
    c = math.sqrt(2.0 / math.pi)
    return x * (0.5 * (1.0 + jnp.tanh(c * (x + 0.044715 * (x * x * x)))))


def _sigmoid(x):
    return 1.0 / (1.0 + jnp.exp(-x))


def _lru_kernel(xb_ref, gb_ref, cprev_ref, hprev_ref, cw_ref, cb_ref, gaw_ref, gab_ref, gxw_ref, gxb_ref,
                sp_ref, init_ref, y_ref, cnew_ref, hnew_ref, xcat, a_scr, b_scr, hcar, *, tt, valid_last):
    del init_ref
    j = pl.program_id(1)
    c_rnn = xb_ref.shape[1]
    bw = c_rnn // N_LRU_BLOCKS
    tail = SUBLANES

    @pl.when(j == 0)
    def _():
        xcat[0:tail, :] = jnp.zeros((tail, c_rnn), F32)
        xcat[tail - (CONV_W - 1):tail, :] = cprev_ref[...]
        hcar[...] = jnp.broadcast_to(hprev_ref[...], (SUBLANES, c_rnn))

    xcat[tail:tail + tt, :] = xb_ref[...]
    cw = cw_ref[...]
    xc = cb_ref[...]
    for tap in range(CONV_W):
        off = tail - (CONV_W - 1) + tap
        xc = xc + xcat[off:off + tt, :] * cw[tap:tap + 1, :]
    cnew_ref[...] = xcat[tail + valid_last - (CONV_W - 1):tail + valid_last, :]
    xcat[0:tail, :] = xcat[tt:tt + tail, :]

    xcb = xc.astype(BF16)
    rs, gs = [], []
    for n in range(N_LRU_BLOCKS):
        xs = xcb[:, n * bw:(n + 1) * bw]
        rs.append(jnp.dot(xs, gaw_ref[n], preferred_element_type=F32))
        gs.append(jnp.dot(xs, gxw_ref[n], preferred_element_type=F32))
    r = _sigmoid(jnp.concatenate(rs, axis=1) + gab_ref[...])
    gi = _sigmoid(jnp.concatenate(gs, axis=1) + gxb_ref[...])
    log_a = (-LRU_C * r) * sp_ref[...]
    a = jnp.exp(log_a)
    a_scr[...] = a
    b_scr[...] = jnp.sqrt(-jnp.tanh(log_a) * (a * a + 1.0)) * (gi * xc)

    row = lax.broadcasted_iota(jnp.int32, (SUBLANES, SCAN_LANES), 0)
    for c in range(c_rnn // SCAN_LANES):
        cs = slice(c * SCAN_LANES, (c + 1) * SCAN_LANES)

        def body(i, carry, cs=cs):
            r0 = pl.multiple_of(i * SUBLANES, SUBLANES)
            av = a_scr[pl.ds(r0, SUBLANES), cs]
            bv = b_scr[pl.ds(r0, SUBLANES), cs]
            for k in (1, 2, 4):
                a_sh = jnp.where(row >= k, pltpu.roll(av, k, 0), 1.0)
                b_sh = jnp.where(row >= k, pltpu.roll(bv, k, 0), 0.0)
                bv = av * b_sh + bv
                av = av * a_sh
            hv = av * carry + bv
            b_scr[pl.ds(r0, SUBLANES), cs] = hv
            return jnp.broadcast_to(hv[SUBLANES - 1:SUBLANES, :], (SUBLANES, SCAN_LANES))

        hcar[:, cs] = lax.fori_loop(0, tt // SUBLANES, body, hcar[:, cs])

    h = b_scr[...]
    hnew_ref[...] = b_scr[valid_last - 1:valid_last, :]
    y_ref[...] = (h * _gelu_tanh(gb_ref[...])).astype(BF16)


def _lru_seq(u, n_seq, n_tiles, tt, row0, seq_stride, valid_last, init, out_row0, cprev, hprev, wts):
    c_rnn = u.shape[1] // 2
    cw, cb, gaw, gab, gxw, gxb, sp = wts
    b0, bs, ob0 = row0 // tt, seq_stride // tt, out_row0 // tt
    bw = c_rnn // N_LRU_BLOCKS
    fix2 = lambda s, j: (0, 0)
    fix3 = lambda s, j: (0, 0, 0)
    per_seq = lambda s, j: (s, 0, 0)
    return pl.pallas_call(
        functools.partial(_lru_kernel, tt=tt, valid_last=valid_last),
        grid=(n_seq, n_tiles),
        in_specs=[pl.BlockSpec((tt, c_rnn), lambda s, j: (b0 + s * bs + j, 0)),
                  pl.BlockSpec((tt, c_rnn), lambda s, j: (b0 + s * bs + j, 1)),
                  pl.BlockSpec((None, CONV_W - 1, c_rnn), per_seq),
                  pl.BlockSpec((None, 1, c_rnn), per_seq),
                  pl.BlockSpec((CONV_W, c_rnn), fix2), pl.BlockSpec((1, c_rnn), fix2),
                  pl.BlockSpec((N_LRU_BLOCKS, bw, bw), fix3), pl.BlockSpec((1, c_rnn), fix2),
                  pl.BlockSpec((N_LRU_BLOCKS, bw, bw), fix3), pl.BlockSpec((1, c_rnn), fix2),
                  pl.BlockSpec((1, c_rnn), fix2), pl.BlockSpec(memory_space=pl.ANY)],
        out_specs=[pl.BlockSpec((tt, c_rnn), lambda s, j: (ob0 + s * bs + j, 0)),
                   pl.BlockSpec((None, CONV_W - 1, c_rnn), per_seq),
                   pl.BlockSpec((None, 1, c_rnn), per_seq)],
        out_shape=[jax.ShapeDtypeStruct(init.shape, BF16),
                   jax.ShapeDtypeStruct((n_seq, CONV_W - 1, c_rnn), F32),
                   jax.ShapeDtypeStruct((n_seq, 1, c_rnn), F32)],
        scratch_shapes=[pltpu.VMEM((SUBLANES + tt, c_rnn), F32), pltpu.VMEM((tt, c_rnn), F32),
                        pltpu.VMEM((tt, c_rnn), F32), pltpu.VMEM((SUBLANES, c_rnn), F32)],
        input_output_aliases={11: 0},
        compiler_params=_cparams("parallel", "arbitrary"),
        name="rglru_seq",
    )(u, u, cprev, hprev, cw, cb, gaw, gab, gxw, gxb, sp, init)


def _moe_kernel(be_ref, nu_ref, x_ref, w1_ref, w3_ref, w2_ref, o_ref, w1b, w3b, w2b):
    i = pl.program_id(0)

    @pl.when(i < nu_ref[0])
    def _():
        prev = be_ref[jnp.maximum(i - 1, 0)]

        @pl.when(jnp.logical_or(i == 0, be_ref[i] != prev))
        def _():
            w1b[...] = w1_ref[...].astype(BF16)
            w3b[...] = w3_ref[...].astype(BF16)
            w2b[...] = w2_ref[...].astype(BF16)

        x = x_ref[...]
        a = jnp.dot(x, w1b[...], preferred_element_type=F32)
        b = jnp.dot(x, w3b[...], preferred_element_type=F32)
        h = (_silu(a) * b).astype(BF16)
        o_ref[...] = jnp.dot(h, w2b[...], preferred_element_type=F32).astype(BF16)


def _moe_experts(x_sorted, w1, w3, w2, layer, block_expert, n_used):
    n_rows, hw = x_sorted.shape
    d, de = w1.shape[2], w1.shape[3]
    nb = n_rows // MOE_BLK
    blk = lambda i, be, nu: (jnp.minimum(i, nu[0] - 1), 0)
    wmap = lambda i, be, nu: (layer, be[jnp.minimum(i, nu[0] - 1)], 0, 0)
    grid_spec = pltpu.PrefetchScalarGridSpec(
        num_scalar_prefetch=2,
        grid=(nb,),
        in_specs=[pl.BlockSpec((MOE_BLK, hw), blk),
                  pl.BlockSpec((None, None, d, de), wmap), pl.BlockSpec((None, None, d, de), wmap),
                  pl.BlockSpec((None, None, de, d), wmap)],
        out_specs=pl.BlockSpec((MOE_BLK, d), blk),
        scratch_shapes=[pltpu.VMEM((d, de), BF16), pltpu.VMEM((d, de), BF16), pltpu.VMEM((de, d), BF16)],
    )
    return pl.pallas_call(
        _moe_kernel,
        grid_spec=grid_spec,
        out_shape=jax.ShapeDtypeStruct((n_rows, d), BF16),
        compiler_params=_cparams("arbitrary"),
        name="moe_experts",
    )(block_expert, n_used, x_sorted, w1, w3, w2)


def _route(logits, router_bias, valid, n_experts):
    t = logits.shape[0]
    per = n_experts // N_GROUPS
    scores = jax.nn.sigmoid(logits)
    biased = scores + router_bias.astype(F32)
    grp = biased.reshape(t, N_GROUPS, per)
    m1 = jnp.max(grp, -1, keepdims=True)
    is_max = grp == m1
    first = is_max & (jnp.cumsum(is_max.astype(jnp.int32), -1) == 1)
    m2 = jnp.max(jnp.where(first, -jnp.inf, grp), -1)
    grp_score = m1[..., 0] + m2

    def rank(v):
        n = v.shape[-1]
        idx = jnp.arange(n)
        vi, vj = v[:, :, None], v[:, None, :]
        beaten = (vj > vi) | ((vj == vi) & (idx[None, None, :] < idx[None, :, None]))
        return jnp.sum(beaten.astype(jnp.int32), -1)

    gsel = rank(grp_score) < TOPK_GROUPS
    emask = jnp.repeat(gsel, per, axis=-1)
    e_rank = rank(jnp.where(emask, biased, NEG_INF))
    slot = e_rank[:, None, :] == jnp.arange(TOPK)[None, :, None]
    eidx = jnp.sum(jnp.where(slot, jnp.arange(n_experts)[None, None, :], 0), -1)
    g = jnp.sum(jnp.where(slot, scores[:, None, :], 0.0), -1)
    g = g / (jnp.sum(g, -1, keepdims=True) + 1e-20) * ROUTED_SCALE
    g = jnp.where(valid[:, None], g, 0.0)
    member = ((e_rank < TOPK) & valid[:, None]).astype(jnp.int32)
    return eidx.astype(jnp.int32), g, member


def _dispatch_plan(eidx, member, valid, n_experts, n_rows):
    t = eidx.shape[0]
    tk = t * TOPK
    flat_e = jnp.where(valid[:, None], eidx, n_experts).reshape(tk)
    order = jnp.argsort(flat_e).astype(jnp.int32)
    tok_sorted = order // TOPK
    cum_excl = jnp.cumsum(member, axis=0) - member
    counts = jnp.sum(member, axis=0)
    padded = (counts + MOE_BLK - 1) // MOE_BLK * MOE_BLK
    pad_end = jnp.cumsum(padded)
    pad_start = pad_end - padded
    grp_start = jnp.cumsum(counts) - counts
    dest = pad_start[eidx] + jnp.take_along_axis(cum_excl, eidx, axis=1)
    dest = jnp.where(valid[:, None], dest, 0).astype(jnp.int32)
    nb = n_rows // MOE_BLK
    blk_row = jnp.arange(nb, dtype=jnp.int32) * MOE_BLK
    block_expert = jnp.minimum(jnp.sum((pad_end[None, :] <= blk_row[:, None]).astype(jnp.int32), axis=1),
                               n_experts - 1).astype(jnp.int32)
    within = (blk_row - pad_start[block_expert])[:, None] + jnp.arange(MOE_BLK, dtype=jnp.int32)[None, :]
    src = jnp.clip(grp_start[block_expert][:, None] + within, 0, tk - 1)
    row_tok = jnp.where(within < counts[block_expert][:, None], tok_sorted[src], 0)
    row_tok = row_tok.reshape(n_rows).astype(jnp.int32)
    n_used = jnp.maximum(pad_end[-1] // MOE_BLK, 1).astype(jnp.int32).reshape(1)
    return row_tok, dest, block_expert, n_used


def _split_bf16(w):
    hi = w.astype(BF16)
    lo = (w - hi.astype(F32)).astype(BF16)
    return hi, lo


def kernel(x_prompt, x_sample, cache_meta_k, cache_meta_v, cache_win_k, cache_win_v, state_conv, state_h, meta_tokens, ln_g, ln_b, attn_w_in, attn_w_out, attn_sink, lru_w_in, lru_conv_w, lru_conv_b, lru_gate_a_w, lru_gate_a_b, lru_gate_x_w, lru_gate_x_b, lru_lambda, lru_w_out, moe_router_w, moe_router_bias, moe_w1, moe_w3, moe_w2, moe_shared_w1, moe_shared_w3, moe_shared_w2):
    bsz, seq, d = x_prompt.shape
    dec_b, dec_s, _ = x_sample.shape
    n_meta = meta_tokens.shape[0]
    depth = ln_g.shape[0]
    n_experts = moe_router_w.shape[2]
    past_len = PAST_LEN
    n_heads = d // HEAD_DIM
    n_kv = n_heads // GROUP
    q_w, kv_w = n_heads * HEAD_DIM, n_kv * HEAD_DIM
    c_rnn = lru_w_in.shape[2] // 2
    alpha = (2 * depth) ** 0.25

    t_real = bsz * seq
    s_base, s_rows = t_real, dec_b * dec_s
    m_base = s_base + s_rows
    t_pad = m_base + bsz * META_BLK
    assert seq % LRU_TILE == 0 and seq % QBLK == 0 and s_rows % QBLK == 0 and t_pad % ROW_TILE == 0
    assert n_meta <= META_BLK and dec_s <= QBLK and q_w == d and kv_w % LANES == 0
    assert cache_win_k.shape[2] == WINDOW and WINDOW == QBLK and n_meta + WINDOW + dec_s <= 3 * QBLK
    nqb = seq // QBLK

    pos = np.zeros((t_pad,), np.float32)
    pos[:t_real] = np.tile(np.arange(seq) + n_meta, bsz)
    pos[s_base:m_base] = np.tile(past_len + n_meta + np.arange(dec_s), dec_b)
    valid_np = np.zeros((t_pad,), bool)
    valid_np[:m_base] = True
    for b in range(bsz):
        pos[m_base + b * META_BLK:m_base + b * META_BLK + n_meta] = np.arange(n_meta)
        valid_np[m_base + b * META_BLK:m_base + b * META_BLK + n_meta] = True
    valid = jnp.asarray(valid_np)

    half = ROT_DIM // 2
    freqs = ROPE_THETA ** (-jnp.arange(0, ROT_DIM, 2, dtype=F32) / ROT_DIM)
    ang = jnp.asarray(pos)[:, None] * freqs[None, :]
    cos, sin = jnp.cos(ang), jnp.sin(ang)
    ones = jnp.ones((t_pad, HEAD_DIM - ROT_DIM), F32)
    zeros_h = jnp.zeros((t_pad, half), F32)
    zeros_r = jnp.zeros((t_pad, HEAD_DIM - ROT_DIM), F32)
    reps = LANES // HEAD_DIM
    cos_t = jnp.tile(jnp.concatenate([cos, cos, ones], 1), (1, reps))
    sin_lo_t = jnp.tile(jnp.concatenate([zeros_h, sin, zeros_r], 1), (1, reps))
    sin_hi_t = jnp.tile(jnp.concatenate([-sin, zeros_h, zeros_r], 1), (1, reps))

    kj = np.arange(3 * QBLK)[:, None]
    qc = np.arange(QBLK)[None, :] // CHUNK
    key_chunk = np.where(kj < 2 * QBLK, kj // CHUNK - 2, 0)
    band_ok = (kj < 2 * QBLK) & (qc - key_chunk >= 0) & (qc - key_chunk <= WINDOW // CHUNK)
    meta_ok = (kj >= 2 * QBLK) & (kj < 2 * QBLK + n_meta)
    later = band_ok | meta_ok
    first = (band_ok & (kj >= QBLK)) | meta_ok
    only_meta = np.broadcast_to(meta_ok, later.shape)
    bias_p = jnp.asarray(np.where(np.stack([first, later, only_meta]), 0.0, NEG_INF).astype(np.float32))
    n_keys_s = n_meta + WINDOW + dec_s
    bias_s = jnp.asarray(np.where(np.broadcast_to(kj < n_keys_s, later.shape), 0.0, NEG_INF)
                         .astype(np.float32))[None]

    meta_rows = jnp.concatenate([meta_tokens.astype(F32), jnp.zeros((META_BLK - n_meta, d), F32)], 0)
    x = jnp.concatenate([x_prompt.reshape(t_real, d), x_sample.reshape(s_rows, d),
                         jnp.tile(meta_rows, (bsz, 1))], 0)
    xb = x.astype(BF16)

    n_assign = (t_real + s_rows + bsz * n_meta) * TOPK
    n_rows = -(-(n_assign + n_experts * (MOE_BLK - 1)) // MOE_BLK) * MOE_BLK

    outs = {k: [] for k in ("mk", "mv", "wk", "wv", "ks", "vs", "cp", "hp", "cs", "hs")}
    for l in range(depth):
        idx = l // 2
        if l % 2 == 0:
            w_in = attn_w_in[idx].astype(BF16)
            qkv = _qkv_proj(xb, w_in, cos_t, sin_lo_t, sin_hi_t, q_w + kv_w)
            k_col, v_col = q_w // kv_w, q_w // kv_w + 1
            sink_rows = jnp.repeat(attn_sink[idx].astype(F32).reshape(n_kv, 1, GROUP), QBLK, axis=2)

            q_s = qkv[s_base:m_base, :q_w].reshape(dec_b, dec_s, q_w)
            q_s = jnp.pad(q_s, ((0, 0), (0, QBLK - dec_s), (0, 0))).reshape(dec_b * QBLK, q_w)
            k_new = qkv[s_base:m_base, q_w:q_w + kv_w].reshape(dec_b, dec_s, kv_w)
            v_new = qkv[s_base:m_base, q_w + kv_w:].reshape(dec_b, dec_s, kv_w)
            pad_k = jnp.zeros((dec_b, 3 * QBLK - n_keys_s, kv_w), F32)
            k_s = jnp.concatenate([cache_meta_k[idx].reshape(dec_b, n_meta, kv_w).astype(F32),
                                   cache_win_k[idx].reshape(dec_b, WINDOW, kv_w).astype(F32), k_new, pad_k], 1)
            v_s = jnp.concatenate([cache_meta_v[idx].reshape(dec_b, n_meta, kv_w).astype(F32),
                                   cache_win_v[idx].reshape(dec_b, WINDOW, kv_w).astype(F32), v_new, pad_k], 1)
            kv_s = jnp.concatenate([k_s, v_s], axis=2).reshape(dec_b * 3 * QBLK, 2 * kv_w)
            o_s = _attention(q_s, kv_s, dec_b, lambda i: i,
                             [lambda i: 3 * i, lambda i: 3 * i + 1, lambda i: 3 * i + 2], bias_s, lambda i: 0,
                             sink_rows, jnp.zeros((dec_b * QBLK, q_w), BF16), lambda i: i, q_w, kv_w, 0, 1,
                             "attn_sample")
            o_s = o_s.reshape(dec_b, QBLK, q_w)[:, :dec_s].reshape(s_rows, q_w)

            steps = nqb + 1
            is_meta = lambda i: (i % steps) == nqb
            bidx = lambda i: i // steps
            pblk = lambda i: i % steps
            frame_blk = lambda i: bidx(i) * nqb + jnp.minimum(pblk(i), nqb - 1)
            meta_blk = lambda i: m_base // QBLK + bidx(i)
            q_map = lambda i: jnp.where(is_meta(i), meta_blk(i), frame_blk(i))
            prev_map = lambda i: bidx(i) * nqb + jnp.clip(pblk(i) - 1, 0, nqb - 1)
            bias_map = lambda i: jnp.where(is_meta(i), 2, jnp.minimum(pblk(i), 1))
            o_init = jnp.concatenate([jnp.zeros((t_real, q_w), BF16), o_s,
                                      jnp.zeros((t_pad - m_base, q_w), BF16)], 0)
            o = _attention(qkv, qkv, bsz * steps, q_map, [prev_map, frame_blk, meta_blk], bias_p, bias_map,
                           sink_rows, o_init, q_map, q_w, kv_w, k_col, v_col, "attn_prompt")

            kp = qkv[:t_real, q_w:q_w + kv_w].reshape(bsz, seq, n_kv, HEAD_DIM)
            vp = qkv[:t_real, q_w + kv_w:].reshape(bsz, seq, n_kv, HEAD_DIM)
            km = qkv[m_base:, q_w:q_w + kv_w].reshape(bsz, META_BLK, n_kv, HEAD_DIM)[:, :n_meta]
            vm = qkv[m_base:, q_w + kv_w:].reshape(bsz, META_BLK, n_kv, HEAD_DIM)[:, :n_meta]
            outs["mk"].append(km); outs["mv"].append(vm)
            outs["wk"].append(kp[:, -WINDOW:]); outs["wv"].append(vp[:, -WINDOW:])
            outs["ks"].append(k_new.reshape(dec_b, dec_s, n_kv, HEAD_DIM))
            outs["vs"].append(v_new.reshape(dec_b, dec_s, n_kv, HEAD_DIM))
            w_out = attn_w_out[idx].astype(BF16)
        else:
            u = _matmul(xb, lru_w_in[idx].astype(BF16), 1024, "lru_in_proj")
            sp = jax.nn.softplus(-lru_lambda[idx].astype(F32)).reshape(1, c_rnn)
            wts = (lru_conv_w[idx].astype(F32), lru_conv_b[idx].astype(F32).reshape(1, c_rnn),
                   lru_gate_a_w[idx].astype(BF16), lru_gate_a_b[idx].astype(F32).reshape(1, c_rnn),
                   lru_gate_x_w[idx].astype(BF16), lru_gate_x_b[idx].astype(F32).reshape(1, c_rnn), sp)
            zc = jnp.zeros((bsz, CONV_W - 1, c_rnn), F32)
            zh = jnp.zeros((bsz, 1, c_rnn), F32)
            y_m, c_m, h_m = _lru_seq(u, bsz, 1, META_BLK, m_base, META_BLK, n_meta,
                                     jnp.zeros((bsz * META_BLK, c_rnn), BF16), 0, zc, zh, wts)
            y_s, c_s, h_s = _lru_seq(u, dec_b, 1, dec_s, s_base, dec_s, dec_s,
                                     jnp.zeros((s_rows, c_rnn), BF16), 0,
                                     state_conv[idx].astype(F32), state_h[idx].astype(F32).reshape(dec_b, 1, c_rnn),
                                     wts)
            o_init = jnp.concatenate([jnp.zeros((t_real, c_rnn), BF16), y_s, y_m], 0)
            o, c_p, h_p = _lru_seq(u, bsz, seq // LRU_TILE, LRU_TILE, 0, seq, LRU_TILE, o_init, 0,
                                   c_m, h_m, wts)
            outs["cp"].append(c_p); outs["hp"].append(h_p.reshape(bsz, c_rnn))
            outs["cs"].append(c_s); outs["hs"].append(h_s.reshape(dec_b, c_rnn))
            w_out = lru_w_out[idx].astype(BF16)

        ne_pad = -(-n_experts // LANES) * LANES
        wr = jnp.pad(moe_router_w[l].astype(F32), ((0, 0), (0, ne_pad - n_experts)))
        wr_hi, wr_lo = _split_bf16(wr)
        x1, x1b, logits = _proj_ln(o, w_out, x, ln_g[l, 0].astype(F32).reshape(1, d),
                                   ln_b[l, 0].astype(F32).reshape(1, d), wr_hi, wr_lo, alpha)

        eidx, gate, member = _route(logits[:, :n_experts], moe_router_bias[l], valid, n_experts)
        row_tok, dest, block_expert, n_used = _dispatch_plan(eidx, member, valid, n_experts, n_rows)
        x_sorted = x1b.at[row_tok].get(mode="promise_in_bounds")
        y_sorted = _moe_experts(x_sorted, moe_w1, moe_w3, moe_w2, l, block_expert, n_used)
        routed = jnp.sum(y_sorted.at[dest].get(mode="promise_in_bounds").astype(F32) * gate[:, :, None], axis=1)

        w13 = jnp.concatenate([moe_shared_w1[l], moe_shared_w3[l]], axis=1).astype(BF16)
        x, xb = _shared_ln(x1b, x1, routed, w13, moe_shared_w2[l].astype(BF16),
                           ln_g[l, 1].astype(F32).reshape(1, d), ln_b[l, 1].astype(F32).reshape(1, d), alpha)

    y_prompt = x[:t_real].reshape(bsz, seq, d)
    y_sample = x[s_base:m_base].reshape(dec_b, dec_s, d)
    st = lambda k: jnp.stack(outs[k])
    return (y_prompt, y_sample, st("mk"), st("mv"), st("wk"), st("wv"), st("ks"), st("vs"),
            st("cp"), st("hp"), st("cs"), st("hs"))
```

```python
import functools
import math

import numpy as np
import jax
import jax.numpy as jnp
from jax import lax
from jax.experimental import pallas as pl
from jax.experimental.pallas import tpu as pltpu

F32 = jnp.float32
BF16 = jnp.bfloat16

PAST_LEN = 1024
CHUNK = 64
WINDOW = 128
HEAD_DIM = 64
GROUP = 8
ROT_DIM = HEAD_DIM // 4
ROPE_THETA = 500000.0
ATTN_SCALE = HEAD_DIM ** -0.5
N_LRU_BLOCKS = 8
CONV_W = 4
LRU_C = 8.0
N_GROUPS = 8
TOPK_GROUPS = 4
TOPK = 8
ROUTED_SCALE = 2.5
LN_EPS = 1e-5
NEG_INF = -1e30

LANES = 128
SUBLANES = 8
QBLK = 2 * CHUNK
META_BLK = 128
ROW_TILE = 512
LN_TILE = 256
LRU_TILE = 256
SCAN_LANES = 512
MOE_BLK = 512
VMEM_LIMIT = 56 * 1024 * 1024


def _cparams(*sem):
    return pltpu.CompilerParams(dimension_semantics=sem, vmem_limit_bytes=VMEM_LIMIT)


def _mm_kernel(x_ref, w_ref, o_ref):
    o_ref[...] = jnp.dot(x_ref[...], w_ref[...], preferred_element_type=F32)


def _matmul(x, w, tn, name):
    m, k = x.shape
    n = w.shape[1]
    tm = ROW_TILE
    return pl.pallas_call(
        _mm_kernel,
        grid=(n // tn, m // tm),
        in_specs=[pl.BlockSpec((tm, k), lambda j, i: (i, 0)),
                  pl.BlockSpec((k, tn), lambda j, i: (0, j))],
        out_specs=pl.BlockSpec((tm, tn), lambda j, i: (i, j)),
        out_shape=jax.ShapeDtypeStruct((m, n), F32),
        compiler_params=_cparams("parallel", "parallel"),
        name=name,
    )(x, w)


def _qkv_kernel(x_ref, w_ref, c_ref, s1_ref, s2_ref, o_ref, *, tn, rope_cols):
    j = pl.program_id(0)
    acc = jnp.dot(x_ref[...], w_ref[...], preferred_element_type=F32)
    c, s1, s2 = c_ref[...], s1_ref[...], s2_ref[...]
    lane = lax.broadcasted_iota(jnp.int32, (acc.shape[0], LANES), 1)
    for g in range(tn // LANES):
        a = acc[:, g * LANES:(g + 1) * LANES]
        lo = pltpu.roll(a, ROT_DIM // 2, 1)
        hi = pltpu.roll(a, LANES - ROT_DIM // 2, 1)
        roped = a * c + lo * s1 + hi * s2
        col = j * tn + g * LANES + lane
        o_ref[:, g * LANES:(g + 1) * LANES] = jnp.where(col < rope_cols, roped, a)


def _qkv_proj(xb, w, cos_t, sin_lo_t, sin_hi_t, rope_cols):
    m, k = xb.shape
    n = w.shape[1]
    tm = ROW_TILE
    tn = 512 if n % 512 == 0 else 256
    assert n % tn == 0 and m % tm == 0
    tab = pl.BlockSpec((tm, LANES), lambda j, i: (i, 0))
    return pl.pallas_call(
        functools.partial(_qkv_kernel, tn=tn, rope_cols=rope_cols),
        grid=(n // tn, m // tm),
        in_specs=[pl.BlockSpec((tm, k), lambda j, i: (i, 0)),
                  pl.BlockSpec((k, tn), lambda j, i: (0, j)),
                  tab, tab, tab],
        out_specs=pl.BlockSpec((tm, tn), lambda j, i: (i, j)),
        out_shape=jax.ShapeDtypeStruct((m, n), F32),
        compiler_params=_cparams("parallel", "parallel"),
        name="qkv_rope",
    )(xb, w, cos_t, sin_lo_t, sin_hi_t)


def _layer_norm_rows(z, g, b):
    mu = jnp.mean(z, -1, keepdims=True)
    d = z - mu
    var = jnp.mean(d * d, -1, keepdims=True)
    return d * lax.rsqrt(var + LN_EPS) * g + b


def _proj_ln_kernel(a_ref, w_ref, x_ref, g_ref, b_ref, wrh_ref, wrl_ref, o_ref, ob_ref, lg_ref, *, alpha):
    acc = jnp.dot(a_ref[...], w_ref[...], preferred_element_type=F32)
    y = _layer_norm_rows(alpha * x_ref[...] + acc, g_ref[...], b_ref[...])
    o_ref[...] = y
    yb = y.astype(BF16)
    ob_ref[...] = yb
    ylo = (y - yb.astype(F32)).astype(BF16)
    wrh = wrh_ref[...]
    lg_ref[...] = (jnp.dot(yb, wrh, preferred_element_type=F32)
                   + jnp.dot(ylo, wrh, preferred_element_type=F32)
                   + jnp.dot(yb, wrl_ref[...], preferred_element_type=F32))


def _proj_ln(a, w, x, g, b, wr_hi, wr_lo, alpha):
    m, k = a.shape
    d = w.shape[1]
    ne = wr_hi.shape[1]
    tm = LN_TILE
    row = lambda i: (i, 0)
    fix = lambda i: (0, 0)
    return pl.pallas_call(
        functools.partial(_proj_ln_kernel, alpha=alpha),
        grid=(m // tm,),
        in_specs=[pl.BlockSpec((tm, k), row), pl.BlockSpec((k, d), fix), pl.BlockSpec((tm, d), row),
                  pl.BlockSpec((1, d), fix), pl.BlockSpec((1, d), fix),
                  pl.BlockSpec((d, ne), fix), pl.BlockSpec((d, ne), fix)],
        out_specs=[pl.BlockSpec((tm, d), row), pl.BlockSpec((tm, d), row), pl.BlockSpec((tm, ne), row)],
        out_shape=[jax.ShapeDtypeStruct((m, d), F32), jax.ShapeDtypeStruct((m, d), BF16),
                   jax.ShapeDtypeStruct((m, ne), F32)],
        compiler_params=_cparams("parallel"),
        name="proj_ln_router",
    )(a, w, x, g, b, wr_hi, wr_lo)


def _silu(x):
    return x * (1.0 / (1.0 + jnp.exp(-x)))


def _shared_ln_kernel(xb_ref, x_ref, yg_ref, gate_ref, w13_ref, w2_ref, g_ref, b_ref, o_ref, ob_ref,
                      *, alpha, dh):
    d = x_ref.shape[1]
    u = jnp.dot(xb_ref[...], w13_ref[...], preferred_element_type=F32)
    hs = (_silu(u[:, :dh]) * u[:, dh:]).astype(BF16)
    f = jnp.dot(hs, w2_ref[...], preferred_element_type=F32)
    gate = gate_ref[...]
    for k in range(TOPK):
        f = f + yg_ref[:, k * d:(k + 1) * d].astype(F32) * gate[:, k:k + 1]
    y = _layer_norm_rows(alpha * x_ref[...] + f, g_ref[...], b_ref[...])
    o_ref[...] = y
    ob_ref[...] = y.astype(BF16)


def _shared_ln(xb, x, y_tok, gate, w13, w2, g, b, alpha):
    m, d = x.shape
    dh = w2.shape[0]
    tm = LN_TILE // 2
    row = lambda i: (i, 0)
    fix = lambda i: (0, 0)
    return pl.pallas_call(
        functools.partial(_shared_ln_kernel, alpha=alpha, dh=dh),
        grid=(m // tm,),
        in_specs=[pl.BlockSpec((tm, d), row), pl.BlockSpec((tm, d), row), pl.BlockSpec((tm, TOPK * d), row),
                  pl.BlockSpec((tm, TOPK), row),
                  pl.BlockSpec((d, 2 * dh), fix), pl.BlockSpec((dh, d), fix),
                  pl.BlockSpec((1, d), fix), pl.BlockSpec((1, d), fix)],
        out_specs=[pl.BlockSpec((tm, d), row), pl.BlockSpec((tm, d), row)],
        out_shape=[jax.ShapeDtypeStruct((m, d), F32), jax.ShapeDtypeStruct((m, d), BF16)],
        compiler_params=_cparams("parallel"),
        name="shared_ln",
    )(xb, x, y_tok, gate, w13, w2, g, b)


def _attn_kernel(q_ref, k0_ref, k1_ref, k2_ref, v0_ref, v1_ref, v2_ref, bias_ref, sink_ref, init_ref, o_ref,
                 ot_ref, *, n_kv):
    del init_ref
    hd = HEAD_DIM
    qt = (q_ref[...] * ATTN_SCALE).T.astype(BF16)
    kb = jnp.concatenate([k0_ref[...], k1_ref[...], k2_ref[...]], 0).astype(BF16)
    vt = jnp.concatenate([v0_ref[...], v1_ref[...], v2_ref[...]], 0).T.astype(BF16)
    bias = bias_ref[0]
    bias = jnp.concatenate([bias] * GROUP, axis=1)
    for h in range(n_kv):
        kh = kb[:, h * hd:(h + 1) * hd]
        qth = jnp.concatenate(
            [qt[(h * GROUP + g) * hd:(h * GROUP + g + 1) * hd, :] for g in range(GROUP)], axis=1)
        s = jnp.dot(kh, qth, preferred_element_type=F32) + bias
        sink = sink_ref[h]
        m = jnp.maximum(jnp.max(s, axis=0, keepdims=True), sink)
        p = jnp.exp(s - m)
        den = jnp.sum(p, axis=0, keepdims=True) + jnp.exp(sink - m)
        ot = jnp.dot(vt[h * hd:(h + 1) * hd, :], p.astype(BF16), preferred_element_type=F32)
        ot = ot * (1.0 / den)
        for g in range(GROUP):
            r0 = (h * GROUP + g) * hd
            ot_ref[r0:r0 + hd, :] = ot[:, g * QBLK:(g + 1) * QBLK]
    o_ref[...] = ot_ref[...].T.astype(BF16)


def _attention(q_arr, k_arr, n_steps, q_map, kv_maps, bias, bias_map, sink_rows, init, out_map,
               q_w, kv_w, k_col, v_col, name):
    n_kv = kv_w // HEAD_DIM
    kspecs = [pl.BlockSpec((QBLK, kv_w), (lambda i, f=f: (f(i), k_col))) for f in kv_maps]
    vspecs = [pl.BlockSpec((QBLK, kv_w), (lambda i, f=f: (f(i), v_col))) for f in kv_maps]
    nk = 3 * QBLK
    return pl.pallas_call(
        functools.partial(_attn_kernel, n_kv=n_kv),
        grid=(n_steps,),
        in_specs=[pl.BlockSpec((QBLK, q_w), lambda i: (q_map(i), 0))] + kspecs + vspecs + [
            pl.BlockSpec((1, nk, QBLK), lambda i: (bias_map(i), 0, 0)),
            pl.BlockSpec((n_kv, 1, GROUP * QBLK), lambda i: (0, 0, 0)),
            pl.BlockSpec(memory_space=pl.ANY)],
        out_specs=pl.BlockSpec((QBLK, q_w), lambda i: (out_map(i), 0)),
        out_shape=jax.ShapeDtypeStruct(init.shape, BF16),
        scratch_shapes=[pltpu.VMEM((q_w, QBLK), F32)],
        input_output_aliases={9: 0},
        compiler_params=_cparams("parallel"),
        name=name,
    )(q_arr, k_arr, k_arr, k_arr, k_arr, k_arr, k_arr, bias, sink_rows, init)


def _gelu_tanh(x):
    c = math.sqrt(2.0 / math.pi)
    return x * (0.5 * (1.0 + jnp.tanh(c * (x + 0.044715 * (x * x * x)))))


def _sigmoid(x):
    return 1.0 / (1.0 + jnp.exp(-x))


def _lru_kernel(xb_ref, gb_ref, cprev_ref, hprev_ref, cw_ref, cb_ref, gaw_ref, gab_ref, gxw_ref, gxb_ref,
                sp_ref, init_ref, y_ref, cnew_ref, hnew_ref, xcat, a_scr, b_scr, hcar, *, tt, valid_last):
    del init_ref
    j = pl.program_id(1)
    c_rnn = xb_ref.shape[1]
    bw = c_rnn // N_LRU_BLOCKS
    tail = SUBLANES

    @pl.when(j == 0)
    def _():
        xcat[0:tail, :] = jnp.zeros((tail, c_rnn), F32)
        xcat[tail - (CONV_W - 1):tail, :] = cprev_ref[...]
        hcar[...] = jnp.broadcast_to(hprev_ref[...], (SUBLANES, c_rnn))

    xcat[tail:tail + tt, :] = xb_ref[...]
    cw = cw_ref[...]
    xc = cb_ref[...]
    for tap in range(CONV_W):
        off = tail - (CONV_W - 1) + tap
        xc = xc + xcat[off:off + tt, :] * cw[tap:tap + 1, :]
    cnew_ref[...] = xcat[tail + valid_last - (CONV_W - 1):tail + valid_last, :]
    xcat[0:tail, :] = xcat[tt:tt + tail, :]

    xcb = xc.astype(BF16)
    rs, gs = [], []
    for n in range(N_LRU_BLOCKS):
        xs = xcb[:, n * bw:(n + 1) * bw]
        rs.append(jnp.dot(xs, gaw_ref[n], preferred_element_type=F32))
        gs.append(jnp.dot(xs, gxw_ref[n], preferred_element_type=F32))
    r = _sigmoid(jnp.concatenate(rs, axis=1) + gab_ref[...])
    gi = _sigmoid(jnp.concatenate(gs, axis=1) + gxb_ref[...])
    log_a = (-LRU_C * r) * sp_ref[...]
    a = jnp.exp(log_a)
    a_scr[...] = a
    b_scr[...] = jnp.sqrt(-jnp.tanh(log_a) * (a * a + 1.0)) * (gi * xc)

    row = lax.broadcasted_iota(jnp.int32, (SUBLANES, SCAN_LANES), 0)
    for c in range(c_rnn // SCAN_LANES):
        cs = slice(c * SCAN_LANES, (c + 1) * SCAN_LANES)

        def body(i, carry, cs=cs):
            r0 = pl.multiple_of(i * SUBLANES, SUBLANES)
            av = a_scr[pl.ds(r0, SUBLANES), cs]
            bv = b_scr[pl.ds(r0, SUBLANES), cs]
            for k in (1, 2, 4):
                a_sh = jnp.where(row >= k, pltpu.roll(av, k, 0), 1.0)
                b_sh = jnp.where(row >= k, pltpu.roll(bv, k, 0), 0.0)
                bv = av * b_sh + bv
                av = av * a_sh
            hv = av * carry + bv
            b_scr[pl.ds(r0, SUBLANES), cs] = hv
            return jnp.broadcast_to(hv[SUBLANES - 1:SUBLANES, :], (SUBLANES, SCAN_LANES))

        hcar[:, cs] = lax.fori_loop(0, tt // SUBLANES, body, hcar[:, cs])

    h = b_scr[...]
    hnew_ref[...] = b_scr[valid_last - 1:valid_last, :]
    y_ref[...] = (h * _gelu_tanh(gb_ref[...])).astype(BF16)


def _lru_seq(u, n_seq, n_tiles, tt, row0, seq_stride, valid_last, init, out_row0, cprev, hprev, wts):
    c_rnn = u.shape[1] // 2
    cw, cb, gaw, gab, gxw, gxb, sp = wts
    b0, bs, ob0 = row0 // tt, seq_stride // tt, out_row0 // tt
    bw = c_rnn // N_LRU_BLOCKS
    fix2 = lambda s, j: (0, 0)
    fix3 = lambda s, j: (0, 0, 0)
    per_seq = lambda s, j: (s, 0, 0)
    return pl.pallas_call(
        functools.partial(_lru_kernel, tt=tt, valid_last=valid_last),
        grid=(n_seq, n_tiles),
        in_specs=[pl.BlockSpec((tt, c_rnn), lambda s, j: (b0 + s * bs + j, 0)),
                  pl.BlockSpec((tt, c_rnn), lambda s, j: (b0 + s * bs + j, 1)),
                  pl.BlockSpec((None, CONV_W - 1, c_rnn), per_seq),
                  pl.BlockSpec((None, 1, c_rnn), per_seq),
                  pl.BlockSpec((CONV_W, c_rnn), fix2), pl.BlockSpec((1, c_rnn), fix2),
                  pl.BlockSpec((N_LRU_BLOCKS, bw, bw), fix3), pl.BlockSpec((1, c_rnn), fix2),
                  pl.BlockSpec((N_LRU_BLOCKS, bw, bw), fix3), pl.BlockSpec((1, c_rnn), fix2),
                  pl.BlockSpec((1, c_rnn), fix2), pl.BlockSpec(memory_space=pl.ANY)],
        out_specs=[pl.BlockSpec((tt, c_rnn), lambda s, j: (ob0 + s * bs + j, 0)),
                   pl.BlockSpec((None, CONV_W - 1, c_rnn), per_seq),
                   pl.BlockSpec((None, 1, c_rnn), per_seq)],
        out_shape=[jax.ShapeDtypeStruct(init.shape, BF16),
                   jax.ShapeDtypeStruct((n_seq, CONV_W - 1, c_rnn), F32),
                   jax.ShapeDtypeStruct((n_seq, 1, c_rnn), F32)],
        scratch_shapes=[pltpu.VMEM((SUBLANES + tt, c_rnn), F32), pltpu.VMEM((tt, c_rnn), F32),
                        pltpu.VMEM((tt, c_rnn), F32), pltpu.VMEM((SUBLANES, c_rnn), F32)],
        input_output_aliases={11: 0},
        compiler_params=_cparams("parallel", "arbitrary"),
        name="rglru_seq",
    )(u, u, cprev, hprev, cw, cb, gaw, gab, gxw, gxb, sp, init)


def _moe_kernel(be_ref, nu_ref, x_ref, w1_ref, w3_ref, w2_ref, o_ref, w1b, w3b, w2b):
    i = pl.program_id(0)

    @pl.when(i < nu_ref[0])
    def _():
        prev = be_ref[jnp.maximum(i - 1, 0)]

        @pl.when(jnp.logical_or(i == 0, be_ref[i] != prev))
        def _():
            w1b[...] = w1_ref[...].astype(BF16)
            w3b[...] = w3_ref[...].astype(BF16)
            w2b[...] = w2_ref[...].astype(BF16)

        x = x_ref[...]
        a = jnp.dot(x, w1b[...], preferred_element_type=F32)
        b = jnp.dot(x, w3b[...], preferred_element_type=F32)
        h = (_silu(a) * b).astype(BF16)
        o_ref[...] = jnp.dot(h, w2b[...], preferred_element_type=F32).astype(BF16)


def _moe_experts(x_sorted, w1, w3, w2, layer, block_expert, n_used):
    n_rows, hw = x_sorted.shape
    d, de = w1.shape[2], w1.shape[3]
    nb = n_rows // MOE_BLK
    blk = lambda i, be, nu: (jnp.minimum(i, nu[0] - 1), 0)
    wmap = lambda i, be, nu: (layer, be[jnp.minimum(i, nu[0] - 1)], 0, 0)
    grid_spec = pltpu.PrefetchScalarGridSpec(
        num_scalar_prefetch=2,
        grid=(nb,),
        in_specs=[pl.BlockSpec((MOE_BLK, hw), blk),
                  pl.BlockSpec((None, None, d, de), wmap), pl.BlockSpec((None, None, d, de), wmap),
                  pl.BlockSpec((None, None, de, d), wmap)],
        out_specs=pl.BlockSpec((MOE_BLK, d), blk),
        scratch_shapes=[pltpu.VMEM((d, de), BF16), pltpu.VMEM((d, de), BF16), pltpu.VMEM((de, d), BF16)],
    )
    return pl.pallas_call(
        _moe_kernel,
        grid_spec=grid_spec,
        out_shape=jax.ShapeDtypeStruct((n_rows, d), BF16),
        compiler_params=_cparams("arbitrary"),
        name="moe_experts",
    )(block_expert, n_used, x_sorted, w1, w3, w2)


def _route(logits, router_bias, valid, n_experts):
    t = logits.shape[0]
    per = n_experts // N_GROUPS
    scores = jax.nn.sigmoid(logits)
    biased = scores + router_bias.astype(F32)
    grp = biased.reshape(t, N_GROUPS, per)
    m1 = jnp.max(grp, -1, keepdims=True)
    is_max = grp == m1
    first = is_max & (jnp.cumsum(is_max.astype(jnp.int32), -1) == 1)
    m2 = jnp.max(jnp.where(first, -jnp.inf, grp), -1)
    grp_score = m1[..., 0] + m2

    def rank(v):
        n = v.shape[-1]
        idx = jnp.arange(n)
        vi, vj = v[:, :, None], v[:, None, :]
        beaten = (vj > vi) | ((vj == vi) & (idx[None, None, :] < idx[None, :, None]))
        return jnp.sum(beaten.astype(jnp.int32), -1)

    gsel = rank(grp_score) < TOPK_GROUPS
    emask = jnp.repeat(gsel, per, axis=-1)
    e_rank = rank(jnp.where(emask, biased, NEG_INF))
    slot = e_rank[:, None, :] == jnp.arange(TOPK)[None, :, None]
    eidx = jnp.sum(jnp.where(slot, jnp.arange(n_experts)[None, None, :], 0), -1)
    g = jnp.sum(jnp.where(slot, scores[:, None, :], 0.0), -1)
    g = g / (jnp.sum(g, -1, keepdims=True) + 1e-20) * ROUTED_SCALE
    g = jnp.where(valid[:, None], g, 0.0)
    member = ((e_rank < TOPK) & valid[:, None]).astype(jnp.int32)
    return eidx.astype(jnp.int32), g, member


def _dispatch_plan(eidx, member, valid, n_experts, n_rows):
    t = eidx.shape[0]
    tk = t * TOPK
    flat_e = jnp.where(valid[:, None], eidx, n_experts).reshape(tk)
    order = jnp.argsort(flat_e).astype(jnp.int32)
    tok_sorted = order // TOPK
    cum_excl = jnp.cumsum(member, axis=0) - member
    counts = jnp.sum(member, axis=0)
    padded = (counts + MOE_BLK - 1) // MOE_BLK * MOE_BLK
    pad_end = jnp.cumsum(padded)
    pad_start = pad_end - padded
    grp_start = jnp.cumsum(counts) - counts
    dest = pad_start[eidx] + jnp.take_along_axis(cum_excl, eidx, axis=1)
    dest = jnp.where(valid[:, None], dest, 0).astype(jnp.int32)
    nb = n_rows // MOE_BLK
    blk_row = jnp.arange(nb, dtype=jnp.int32) * MOE_BLK
    block_expert = jnp.minimum(jnp.sum((pad_end[None, :] <= blk_row[:, None]).astype(jnp.int32), axis=1),
                               n_experts - 1).astype(jnp.int32)
    within = (blk_row - pad_start[block_expert])[:, None] + jnp.arange(MOE_BLK, dtype=jnp.int32)[None, :]
    src = jnp.clip(grp_start[block_expert][:, None] + within, 0, tk - 1)
    row_tok = jnp.where(within < counts[block_expert][:, None], tok_sorted[src], 0)
    row_tok = row_tok.reshape(n_rows).astype(jnp.int32)
    n_used = jnp.maximum(pad_end[-1] // MOE_BLK, 1).astype(jnp.int32).reshape(1)
    return row_tok, dest, block_expert, n_used


def _split_bf16(w):
    hi = w.astype(BF16)
    lo = (w - hi.astype(F32)).astype(BF16)
    return hi, lo


def kernel(x_prompt, x_sample, cache_meta_k, cache_meta_v, cache_win_k, cache_win_v, state_conv, state_h, meta_tokens, ln_g, ln_b, attn_w_in, attn_w_out, attn_sink, lru_w_in, lru_conv_w, lru_conv_b, lru_gate_a_w, lru_gate_a_b, lru_gate_x_w, lru_gate_x_b, lru_lambda, lru_w_out, moe_router_w, moe_router_bias, moe_w1, moe_w3, moe_w2, moe_shared_w1, moe_shared_w3, moe_shared_w2):
    bsz, seq, d = x_prompt.shape
    dec_b, dec_s, _ = x_sample.shape
    n_meta = meta_tokens.shape[0]
    depth = ln_g.shape[0]
    n_experts = moe_router_w.shape[2]
    past_len = PAST_LEN
    n_heads = d // HEAD_DIM
    n_kv = n_heads // GROUP
    q_w, kv_w = n_heads * HEAD_DIM, n_kv * HEAD_DIM
    c_rnn = lru_w_in.shape[2] // 2
    alpha = (2 * depth) ** 0.25

    t_real = bsz * seq
    s_base, s_rows = t_real, dec_b * dec_s
    m_base = s_base + s_rows
    t_pad = m_base + bsz * META_BLK
    assert seq % LRU_TILE == 0 and seq % QBLK == 0 and s_rows % QBLK == 0 and t_pad % ROW_TILE == 0
    assert n_meta <= META_BLK and dec_s <= QBLK and q_w == d and kv_w % LANES == 0
    assert cache_win_k.shape[2] == WINDOW and WINDOW == QBLK and n_meta + WINDOW + dec_s <= 3 * QBLK
    nqb = seq // QBLK

    pos = np.zeros((t_pad,), np.float32)
    pos[:t_real] = np.tile(np.arange(seq) + n_meta, bsz)
    pos[s_base:m_base] = np.tile(past_len + n_meta + np.arange(dec_s), dec_b)
    valid_np = np.zeros((t_pad,), bool)
    valid_np[:m_base] = True
    for b in range(bsz):
        pos[m_base + b * META_BLK:m_base + b * META_BLK + n_meta] = np.arange(n_meta)
        valid_np[m_base + b * META_BLK:m_base + b * META_BLK + n_meta] = True
    valid = jnp.asarray(valid_np)

    half = ROT_DIM // 2
    freqs = ROPE_THETA ** (-jnp.arange(0, ROT_DIM, 2, dtype=F32) / ROT_DIM)
    ang = jnp.asarray(pos)[:, None] * freqs[None, :]
    cos, sin = jnp.cos(ang), jnp.sin(ang)
    ones = jnp.ones((t_pad, HEAD_DIM - ROT_DIM), F32)
    zeros_h = jnp.zeros((t_pad, half), F32)
    zeros_r = jnp.zeros((t_pad, HEAD_DIM - ROT_DIM), F32)
    reps = LANES // HEAD_DIM
    cos_t = jnp.tile(jnp.concatenate([cos, cos, ones], 1), (1, reps))
    sin_lo_t = jnp.tile(jnp.concatenate([zeros_h, sin, zeros_r], 1), (1, reps))
    sin_hi_t = jnp.tile(jnp.concatenate([-sin, zeros_h, zeros_r], 1), (1, reps))

    kj = np.arange(3 * QBLK)[:, None]
    qc = np.arange(QBLK)[None, :] // CHUNK
    key_chunk = np.where(kj < 2 * QBLK, kj // CHUNK - 2, 0)
    band_ok = (kj < 2 * QBLK) & (qc - key_chunk >= 0) & (qc - key_chunk <= WINDOW // CHUNK)
    meta_ok = (kj >= 2 * QBLK) & (kj < 2 * QBLK + n_meta)
    later = band_ok | meta_ok
    first = (band_ok & (kj >= QBLK)) | meta_ok
    only_meta = np.broadcast_to(meta_ok, later.shape)
    bias_p = jnp.asarray(np.where(np.stack([first, later, only_meta]), 0.0, NEG_INF).astype(np.float32))
    n_keys_s = n_meta + WINDOW + dec_s
    bias_s = jnp.asarray(np.where(np.broadcast_to(kj < n_keys_s, later.shape), 0.0, NEG_INF)
                         .astype(np.float32))[None]

    meta_rows = jnp.concatenate([meta_tokens.astype(F32), jnp.zeros((META_BLK - n_meta, d), F32)], 0)
    x = jnp.concatenate([x_prompt.reshape(t_real, d), x_sample.reshape(s_rows, d),
                         jnp.tile(meta_rows, (bsz, 1))], 0)
    xb = x.astype(BF16)

    n_assign = (t_real + s_rows + bsz * n_meta) * TOPK
    n_rows = -(-(n_assign + n_experts * (MOE_BLK - 1)) // MOE_BLK) * MOE_BLK

    outs = {k: [] for k in ("mk", "mv", "wk", "wv", "ks", "vs", "cp", "hp", "cs", "hs")}
    for l in range(depth):
        idx = l // 2
        if l % 2 == 0:
            w_in = attn_w_in[idx].astype(BF16)
            qkv = _qkv_proj(xb, w_in, cos_t, sin_lo_t, sin_hi_t, q_w + kv_w)
            k_col, v_col = q_w // kv_w, q_w // kv_w + 1
            sink_rows = jnp.repeat(attn_sink[idx].astype(F32).reshape(n_kv, 1, GROUP), QBLK, axis=2)

            q_s = qkv[s_base:m_base, :q_w].reshape(dec_b, dec_s, q_w)
            q_s = jnp.pad(q_s, ((0, 0), (0, QBLK - dec_s), (0, 0))).reshape(dec_b * QBLK, q_w)
            k_new = qkv[s_base:m_base, q_w:q_w + kv_w].reshape(dec_b, dec_s, kv_w)
            v_new = qkv[s_base:m_base, q_w + kv_w:].reshape(dec_b, dec_s, kv_w)
            pad_k = jnp.zeros((dec_b, 3 * QBLK - n_keys_s, kv_w), F32)
            k_s = jnp.concatenate([cache_meta_k[idx].reshape(dec_b, n_meta, kv_w).astype(F32),
                                   cache_win_k[idx].reshape(dec_b, WINDOW, kv_w).astype(F32), k_new, pad_k], 1)
            v_s = jnp.concatenate([cache_meta_v[idx].reshape(dec_b, n_meta, kv_w).astype(F32),
                                   cache_win_v[idx].reshape(dec_b, WINDOW, kv_w).astype(F32), v_new, pad_k], 1)
            kv_s = jnp.concatenate([k_s, v_s], axis=2).reshape(dec_b * 3 * QBLK, 2 * kv_w)
            o_s = _attention(q_s, kv_s, dec_b, lambda i: i,
                             [lambda i: 3 * i, lambda i: 3 * i + 1, lambda i: 3 * i + 2], bias_s, lambda i: 0,
                             sink_rows, jnp.zeros((dec_b * QBLK, q_w), BF16), lambda i: i, q_w, kv_w, 0, 1,
                             "attn_sample")
            o_s = o_s.reshape(dec_b, QBLK, q_w)[:, :dec_s].reshape(s_rows, q_w)

            steps = nqb + 1
            is_meta = lambda i: (i % steps) == nqb
            bidx = lambda i: i // steps
            pblk = lambda i: i % steps
            frame_blk = lambda i: bidx(i) * nqb + jnp.minimum(pblk(i), nqb - 1)
            meta_blk = lambda i: m_base // QBLK + bidx(i)
            q_map = lambda i: jnp.where(is_meta(i), meta_blk(i), frame_blk(i))
            prev_map = lambda i: bidx(i) * nqb + jnp.clip(pblk(i) - 1, 0, nqb - 1)
            bias_map = lambda i: jnp.where(is_meta(i), 2, jnp.minimum(pblk(i), 1))
            o_init = jnp.concatenate([jnp.zeros((t_real, q_w), BF16), o_s,
                                      jnp.zeros((t_pad - m_base, q_w), BF16)], 0)
            o = _attention(qkv, qkv, bsz * steps, q_map, [prev_map, frame_blk, meta_blk], bias_p, bias_map,
                           sink_rows, o_init, q_map, q_w, kv_w, k_col, v_col, "attn_prompt")

            kp = qkv[:t_real, q_w:q_w + kv_w].reshape(bsz, seq, n_kv, HEAD_DIM)
            vp = qkv[:t_real, q_w + kv_w:].reshape(bsz, seq, n_kv, HEAD_DIM)
            km = qkv[m_base:, q_w:q_w + kv_w].reshape(bsz, META_BLK, n_kv, HEAD_DIM)[:, :n_meta]
            vm = qkv[m_base:, q_w + kv_w:].reshape(bsz, META_BLK, n_kv, HEAD_DIM)[:, :n_meta]
            outs["mk"].append(km); outs["mv"].append(vm)
            outs["wk"].append(kp[:, -WINDOW:]); outs["wv"].append(vp[:, -WINDOW:])
            outs["ks"].append(k_new.reshape(dec_b, dec_s, n_kv, HEAD_DIM))
            outs["vs"].append(v_new.reshape(dec_b, dec_s, n_kv, HEAD_DIM))
            w_out = attn_w_out[idx].astype(BF16)
        else:
            u = _matmul(xb, lru_w_in[idx].astype(BF16), 1024, "lru_in_proj")
            sp = jax.nn.softplus(-lru_lambda[idx].astype(F32)).reshape(1, c_rnn)
            wts = (lru_conv_w[idx].astype(F32), lru_conv_b[idx].astype(F32).reshape(1, c_rnn),
                   lru_gate_a_w[idx].astype(BF16), lru_gate_a_b[idx].astype(F32).reshape(1, c_rnn),
                   lru_gate_x_w[idx].astype(BF16), lru_gate_x_b[idx].astype(F32).reshape(1, c_rnn), sp)
            zc = jnp.zeros((bsz, CONV_W - 1, c_rnn), F32)
            zh = jnp.zeros((bsz, 1, c_rnn), F32)
            y_m, c_m, h_m = _lru_seq(u, bsz, 1, META_BLK, m_base, META_BLK, n_meta,
                                     jnp.zeros((bsz * META_BLK, c_rnn), BF16), 0, zc, zh, wts)
            y_s, c_s, h_s = _lru_seq(u, dec_b, 1, dec_s, s_base, dec_s, dec_s,
                                     jnp.zeros((s_rows, c_rnn), BF16), 0,
                                     state_conv[idx].astype(F32), state_h[idx].astype(F32).reshape(dec_b, 1, c_rnn),
                                     wts)
            o_init = jnp.concatenate([jnp.zeros((t_real, c_rnn), BF16), y_s, y_m], 0)
            o, c_p, h_p = _lru_seq(u, bsz, seq // LRU_TILE, LRU_TILE, 0, seq, LRU_TILE, o_init, 0,
                                   c_m, h_m, wts)
            outs["cp"].append(c_p); outs["hp"].append(h_p.reshape(bsz, c_rnn))
            outs["cs"].append(c_s); outs["hs"].append(h_s.reshape(dec_b, c_rnn))
            w_out = lru_w_out[idx].astype(BF16)

        ne_pad = -(-n_experts // LANES) * LANES
        wr = jnp.pad(moe_router_w[l].astype(F32), ((0, 0), (0, ne_pad - n_experts)))
        wr_hi, wr_lo = _split_bf16(wr)
        x1, x1b, logits = _proj_ln(o, w_out, x, ln_g[l, 0].astype(F32).reshape(1, d),
                                   ln_b[l, 0].astype(F32).reshape(1, d), wr_hi, wr_lo, alpha)

        eidx, gate, member = _route(logits[:, :n_experts], moe_router_bias[l], valid, n_experts)
        row_tok, dest, block_expert, n_used = _dispatch_plan(eidx, member, valid, n_experts, n_rows)
        x_sorted = x1b.at[row_tok].get(mode="promise_in_bounds")
        y_sorted = _moe_experts(x_sorted, moe_w1, moe_w3, moe_w2, l, block_expert, n_used)
        y_tok = y_sorted.at[dest.reshape(-1)].get(mode="promise_in_bounds").reshape(t_pad, TOPK * d)

        w13 = jnp.concatenate([moe_shared_w1[l], moe_shared_w3[l]], axis=1).astype(BF16)
        x, xb = _shared_ln(x1b, x1, y_tok, gate, w13, moe_shared_w2[l].astype(BF16),
                           ln_g[l, 1].astype(F32).reshape(1, d), ln_b[l, 1].astype(F32).reshape(1, d), alpha)

    y_prompt = x[:t_real].reshape(bsz, seq, d)
    y_sample = x[s_base:m_base].reshape(dec_b, dec_s, d)
    st = lambda k: jnp.stack(outs[k])
    return (y_prompt, y_sample, st("mk"), st("mv"), st("wk"), st("wv"), st("ks"), st("vs"),
            st("cp"), st("hp"), st("cs"), st("hs"))
```

```python
import functools
import math

import numpy as np
import jax
import jax.numpy as jnp
from jax import lax
from jax.experimental import pallas as pl
from jax.experimental.pallas import tpu as pltpu

F32 = jnp.float32
BF16 = jnp.bfloat16

PAST_LEN = 1024
CHUNK = 64
WINDOW = 128
HEAD_DIM = 64
GROUP = 8
ROT_DIM = HEAD_DIM // 4
ROPE_THETA = 500000.0
ATTN_SCALE = HEAD_DIM ** -0.5
N_LRU_BLOCKS = 8
CONV_W = 4
LRU_C = 8.0
N_GROUPS = 8
TOPK_GROUPS = 4
TOPK = 8
ROUTED_SCALE = 2.5
LN_EPS = 1e-5
NEG_INF = -1e30

LANES = 128
SUBLANES = 8
QBLK = 2 * CHUNK
META_BLK = 128
ROW_TILE = 512
LN_TILE = 256
LRU_TILE = 256
SCAN_LANES = 512
MOE_BLK = 512
CMB_TILE = 256
RUN_ALIGN = SUBLANES
VMEM_LIMIT = 56 * 1024 * 1024


def _cparams(*sem):
    return pltpu.CompilerParams(dimension_semantics=sem, vmem_limit_bytes=VMEM_LIMIT)


def _mm_kernel(x_ref, w_ref, o_ref):
    o_ref[...] = jnp.dot(x_ref[...], w_ref[...], preferred_element_type=F32)


def _matmul(x, w, tn, name):
    m, k = x.shape
    n = w.shape[1]
    tm = ROW_TILE
    return pl.pallas_call(
        _mm_kernel,
        grid=(n // tn, m // tm),
        in_specs=[pl.BlockSpec((tm, k), lambda j, i: (i, 0)),
                  pl.BlockSpec((k, tn), lambda j, i: (0, j))],
        out_specs=pl.BlockSpec((tm, tn), lambda j, i: (i, j)),
        out_shape=jax.ShapeDtypeStruct((m, n), F32),
        compiler_params=_cparams("parallel", "parallel"),
        name=name,
    )(x, w)


def _qkv_kernel(x_ref, w_ref, c_ref, s1_ref, s2_ref, o_ref, *, tn, rope_cols):
    j = pl.program_id(0)
    acc = jnp.dot(x_ref[...], w_ref[...], preferred_element_type=F32)
    c, s1, s2 = c_ref[...], s1_ref[...], s2_ref[...]
    lane = lax.broadcasted_iota(jnp.int32, (acc.shape[0], LANES), 1)
    for g in range(tn // LANES):
        a = acc[:, g * LANES:(g + 1) * LANES]
        lo = pltpu.roll(a, ROT_DIM // 2, 1)
        hi = pltpu.roll(a, LANES - ROT_DIM // 2, 1)
        roped = a * c + lo * s1 + hi * s2
        col = j * tn + g * LANES + lane
        o_ref[:, g * LANES:(g + 1) * LANES] = jnp.where(col < rope_cols, roped, a)


def _qkv_proj(xb, w, cos_t, sin_lo_t, sin_hi_t, rope_cols):
    m, k = xb.shape
    n = w.shape[1]
    tm = ROW_TILE
    tn = 512 if n % 512 == 0 else 256
    assert n % tn == 0 and m % tm == 0
    tab = pl.BlockSpec((tm, LANES), lambda j, i: (i, 0))
    return pl.pallas_call(
        functools.partial(_qkv_kernel, tn=tn, rope_cols=rope_cols),
        grid=(n // tn, m // tm),
        in_specs=[pl.BlockSpec((tm, k), lambda j, i: (i, 0)),
                  pl.BlockSpec((k, tn), lambda j, i: (0, j)),
                  tab, tab, tab],
        out_specs=pl.BlockSpec((tm, tn), lambda j, i: (i, j)),
        out_shape=jax.ShapeDtypeStruct((m, n), F32),
        compiler_params=_cparams("parallel", "parallel"),
        name="qkv_rope",
    )(xb, w, cos_t, sin_lo_t, sin_hi_t)


def _layer_norm_rows(z, g, b):
    mu = jnp.mean(z, -1, keepdims=True)
    d = z - mu
    var = jnp.mean(d * d, -1, keepdims=True)
    return d * lax.rsqrt(var + LN_EPS) * g + b


def _proj_ln_kernel(a_ref, w_ref, x_ref, g_ref, b_ref, wrh_ref, wrl_ref, o_ref, ob_ref, lg_ref, *, alpha):
    acc = jnp.dot(a_ref[...], w_ref[...], preferred_element_type=F32)
    y = _layer_norm_rows(alpha * x_ref[...] + acc, g_ref[...], b_ref[...])
    o_ref[...] = y
    yb = y.astype(BF16)
    ob_ref[...] = yb
    ylo = (y - yb.astype(F32)).astype(BF16)
    wrh = wrh_ref[...]
    lg_ref[...] = (jnp.dot(yb, wrh, preferred_element_type=F32)
                   + jnp.dot(ylo, wrh, preferred_element_type=F32)
                   + jnp.dot(yb, wrl_ref[...], preferred_element_type=F32))


def _proj_ln(a, w, x, g, b, wr_hi, wr_lo, alpha):
    m, k = a.shape
    d = w.shape[1]
    ne = wr_hi.shape[1]
    tm = LN_TILE
    row = lambda i: (i, 0)
    fix = lambda i: (0, 0)
    return pl.pallas_call(
        functools.partial(_proj_ln_kernel, alpha=alpha),
        grid=(m // tm,),
        in_specs=[pl.BlockSpec((tm, k), row), pl.BlockSpec((k, d), fix), pl.BlockSpec((tm, d), row),
                  pl.BlockSpec((1, d), fix), pl.BlockSpec((1, d), fix),
                  pl.BlockSpec((d, ne), fix), pl.BlockSpec((d, ne), fix)],
        out_specs=[pl.BlockSpec((tm, d), row), pl.BlockSpec((tm, d), row), pl.BlockSpec((tm, ne), row)],
        out_shape=[jax.ShapeDtypeStruct((m, d), F32), jax.ShapeDtypeStruct((m, d), BF16),
                   jax.ShapeDtypeStruct((m, ne), F32)],
        compiler_params=_cparams("parallel"),
        name="proj_ln_router",
    )(a, w, x, g, b, wr_hi, wr_lo)


def _silu(x):
    return x * (1.0 / (1.0 + jnp.exp(-x)))


def _combine_ln_kernel(rs_ref, nc_ref, lb_ref, tot_ref, xb_ref, x_ref, loc_ref, gate_ref, y_hbm, w13_ref, w2_ref,
                       g_ref, b_ref, o_ref, ob_ref, ybuf, sem, *, alpha, dh, n_experts):
    i = pl.program_id(0)
    rows = ybuf.shape[0]

    def chunk_copy(src_row, dst_row):
        return pltpu.make_async_copy(y_hbm.at[pl.ds(pl.multiple_of(src_row, RUN_ALIGN), RUN_ALIGN)],
                                     ybuf.at[pl.ds(pl.multiple_of(dst_row, RUN_ALIGN), RUN_ALIGN)], sem)

    ybuf[...] = jnp.zeros(ybuf.shape, ybuf.dtype)

    def per_expert(e, c):
        r = i * n_experts + e
        src, dst = rs_ref[r], lb_ref[r]

        def per_chunk(j, c2):
            chunk_copy(src + j * RUN_ALIGN, dst + j * RUN_ALIGN).start()
            return c2

        return lax.fori_loop(0, nc_ref[r], per_chunk, c)

    lax.fori_loop(0, n_experts, per_expert, 0)

    u = jnp.dot(xb_ref[...], w13_ref[...], preferred_element_type=F32)
    hs = (_silu(u[:, :dh]) * u[:, dh:]).astype(BF16)
    f = jnp.dot(hs, w2_ref[...], preferred_element_type=F32)

    loc = loc_ref[...]
    gate = gate_ref[...]
    col = lax.broadcasted_iota(jnp.int32, (loc.shape[0], rows), 1)
    sel = jnp.zeros((loc.shape[0], rows), F32)
    for k in range(TOPK):
        sel = jnp.where(col == loc[:, k:k + 1], gate[:, k:k + 1], sel)
    sel_hi = sel.astype(BF16)
    sel_lo = (sel - sel_hi.astype(F32)).astype(BF16)

    lax.fori_loop(0, tot_ref[i], lambda j, c: (chunk_copy(0, 0).wait(), c)[1], 0)

    y_lo, y_hi = _unpack_halves(ybuf[...])
    r_lo = (jnp.dot(sel_hi, y_lo, preferred_element_type=F32) + jnp.dot(sel_lo, y_lo, preferred_element_type=F32))
    r_hi = (jnp.dot(sel_hi, y_hi, preferred_element_type=F32) + jnp.dot(sel_lo, y_hi, preferred_element_type=F32))
    f = f + jnp.concatenate([r_lo, r_hi], axis=1)
    y = _layer_norm_rows(alpha * x_ref[...] + f, g_ref[...], b_ref[...])
    o_ref[...] = y
    ob_ref[...] = y.astype(BF16)


def _combine_ln(xb, x, loc, gate, y_sorted, run_start, n_chunks, lbase, tile_chunks, w13, w2, g, b, alpha,
                n_experts):
    m, d = x.shape
    dh = w2.shape[0]
    tm = CMB_TILE
    buf_rows = -(-(tm * TOPK + n_experts * (RUN_ALIGN - 1)) // 256) * 256
    row = lambda i, *_: (i, 0)
    fix = lambda i, *_: (0, 0)
    grid_spec = pltpu.PrefetchScalarGridSpec(
        num_scalar_prefetch=4,
        grid=(m // tm,),
        in_specs=[pl.BlockSpec((tm, d), row), pl.BlockSpec((tm, d), row), pl.BlockSpec((tm, TOPK), row),
                  pl.BlockSpec((tm, TOPK), row), pl.BlockSpec(memory_space=pl.ANY),
                  pl.BlockSpec((d, 2 * dh), fix), pl.BlockSpec((dh, d), fix),
                  pl.BlockSpec((1, d), fix), pl.BlockSpec((1, d), fix)],
        out_specs=[pl.BlockSpec((tm, d), row), pl.BlockSpec((tm, d), row)],
        scratch_shapes=[pltpu.VMEM((buf_rows, y_sorted.shape[1]), y_sorted.dtype), pltpu.SemaphoreType.DMA(())],
    )
    return pl.pallas_call(
        functools.partial(_combine_ln_kernel, alpha=alpha, dh=dh, n_experts=n_experts),
        grid_spec=grid_spec,
        out_shape=[jax.ShapeDtypeStruct((m, d), F32), jax.ShapeDtypeStruct((m, d), BF16)],
        compiler_params=_cparams("arbitrary"),
        name="combine_ln",
    )(run_start, n_chunks, lbase, tile_chunks, xb, x, loc, gate, y_sorted, w13, w2, g, b)


def _attn_kernel(q_ref, k0_ref, k1_ref, k2_ref, v0_ref, v1_ref, v2_ref, bias_ref, sink_ref, init_ref, o_ref,
                 ot_ref, *, n_kv):
    del init_ref
    hd = HEAD_DIM
    qt = (q_ref[...] * ATTN_SCALE).T.astype(BF16)
    kb = jnp.concatenate([k0_ref[...], k1_ref[...], k2_ref[...]], 0).astype(BF16)
    vt = jnp.concatenate([v0_ref[...], v1_ref[...], v2_ref[...]], 0).T.astype(BF16)
    bias = bias_ref[0]
    bias = jnp.concatenate([bias] * GROUP, axis=1)
    for h in range(n_kv):
        kh = kb[:, h * hd:(h + 1) * hd]
        qth = jnp.concatenate(
            [qt[(h * GROUP + g) * hd:(h * GROUP + g + 1) * hd, :] for g in range(GROUP)], axis=1)
        s = jnp.dot(kh, qth, preferred_element_type=F32) + bias
        sink = sink_ref[h]
        m = jnp.maximum(jnp.max(s, axis=0, keepdims=True), sink)
        p = jnp.exp(s - m)
        den = jnp.sum(p, axis=0, keepdims=True) + jnp.exp(sink - m)
        ot = jnp.dot(vt[h * hd:(h + 1) * hd, :], p.astype(BF16), preferred_element_type=F32)
        ot = ot * (1.0 / den)
        for g in range(GROUP):
            r0 = (h * GROUP + g) * hd
            ot_ref[r0:r0 + hd, :] = ot[:, g * QBLK:(g + 1) * QBLK]
    o_ref[...] = ot_ref[...].T.astype(BF16)


def _attention(q_arr, k_arr, n_steps, q_map, kv_maps, bias, bias_map, sink_rows, init, out_map,
               q_w, kv_w, k_col, v_col, name):
    n_kv = kv_w // HEAD_DIM
    kspecs = [pl.BlockSpec((QBLK, kv_w), (lambda i, f=f: (f(i), k_col))) for f in kv_maps]
    vspecs = [pl.BlockSpec((QBLK, kv_w), (lambda i, f=f: (f(i), v_col))) for f in kv_maps]
    nk = 3 * QBLK
    return pl.pallas_call(
        functools.partial(_attn_kernel, n_kv=n_kv),
        grid=(n_steps,),
        in_specs=[pl.BlockSpec((QBLK, q_w), lambda i: (q_map(i), 0))] + kspecs + vspecs + [
            pl.BlockSpec((1, nk, QBLK), lambda i: (bias_map(i), 0, 0)),
            pl.BlockSpec((n_kv, 1, GROUP * QBLK), lambda i: (0, 0, 0)),
            pl.BlockSpec(memory_space=pl.ANY)],
        out_specs=pl.BlockSpec((QBLK, q_w), lambda i: (out_map(i), 0)),
        out_shape=jax.ShapeDtypeStruct(init.shape, BF16),
        scratch_shapes=[pltpu.VMEM((q_w, QBLK), F32)],
        input_output_aliases={9: 0},
        compiler_params=_cparams("parallel"),
        name=name,
    )(q_arr, k_arr, k_arr, k_arr, k_arr, k_arr, k_arr, bias, sink_rows, init)


def _gelu_tanh(x):
    c = math.sqrt(2.0 / math.pi)
    return x * (0.5 * (1.0 + jnp.tanh(c * (x + 0.044715 * (x * x * x)))))


def _sigmoid(x):
    return 1.0 / (1.0 + jnp.exp(-x))


def _lru_kernel(xb_ref, gb_ref, cprev_ref, hprev_ref, cw_ref, cb_ref, gaw_ref, gab_ref, gxw_ref, gxb_ref,
                sp_ref, init_ref, y_ref, cnew_ref, hnew_ref, xcat, a_scr, b_scr, hcar, *, tt, valid_last):
    del init_ref
    j = pl.program_id(1)
    c_rnn = xb_ref.shape[1]
    bw = c_rnn // N_LRU_BLOCKS
    tail = SUBLANES

    @pl.when(j == 0)
    def _():
        xcat[0:tail, :] = jnp.zeros((tail, c_rnn), F32)
        xcat[tail - (CONV_W - 1):tail, :] = cprev_ref[...]
        hcar[...] = jnp.broadcast_to(hprev_ref[...], (SUBLANES, c_rnn))

    xcat[tail:tail + tt, :] = xb_ref[...]
    cw = cw_ref[...]
    xc = cb_ref[...]
    for tap in range(CONV_W):
        off = tail - (CONV_W - 1) + tap
        xc = xc + xcat[off:off + tt, :] * cw[tap:tap + 1, :]
    cnew_ref[...] = xcat[tail + valid_last - (CONV_W - 1):tail + valid_last, :]
    xcat[0:tail, :] = xcat[tt:tt + tail, :]

    xcb = xc.astype(BF16)
    rs, gs = [], []
    for n in range(N_LRU_BLOCKS):
        xs = xcb[:, n * bw:(n + 1) * bw]
        rs.append(jnp.dot(xs, gaw_ref[n], preferred_element_type=F32))
        gs.append(jnp.dot(xs, gxw_ref[n], preferred_element_type=F32))
    r = _sigmoid(jnp.concatenate(rs, axis=1) + gab_ref[...])
    gi = _sigmoid(jnp.concatenate(gs, axis=1) + gxb_ref[...])
    log_a = (-LRU_C * r) * sp_ref[...]
    a = jnp.exp(log_a)
    a_scr[...] = a
    b_scr[...] = jnp.sqrt(-jnp.tanh(log_a) * (a * a + 1.0)) * (gi * xc)

    row = lax.broadcasted_iota(jnp.int32, (SUBLANES, SCAN_LANES), 0)
    for c in range(c_rnn // SCAN_LANES):
        cs = slice(c * SCAN_LANES, (c + 1) * SCAN_LANES)

        def body(i, carry, cs=cs):
            r0 = pl.multiple_of(i * SUBLANES, SUBLANES)
            av = a_scr[pl.ds(r0, SUBLANES), cs]
            bv = b_scr[pl.ds(r0, SUBLANES), cs]
            for k in (1, 2, 4):
                a_sh = jnp.where(row >= k, pltpu.roll(av, k, 0), 1.0)
                b_sh = jnp.where(row >= k, pltpu.roll(bv, k, 0), 0.0)
                bv = av * b_sh + bv
                av = av * a_sh
            hv = av * carry + bv
            b_scr[pl.ds(r0, SUBLANES), cs] = hv
            return jnp.broadcast_to(hv[SUBLANES - 1:SUBLANES, :], (SUBLANES, SCAN_LANES))

        hcar[:, cs] = lax.fori_loop(0, tt // SUBLANES, body, hcar[:, cs])

    h = b_scr[...]
    hnew_ref[...] = b_scr[valid_last - 1:valid_last, :]
    y_ref[...] = (h * _gelu_tanh(gb_ref[...])).astype(BF16)


def _lru_seq(u, n_seq, n_tiles, tt, row0, seq_stride, valid_last, init, out_row0, cprev, hprev, wts):
    c_rnn = u.shape[1] // 2
    cw, cb, gaw, gab, gxw, gxb, sp = wts
    b0, bs, ob0 = row0 // tt, seq_stride // tt, out_row0 // tt
    bw = c_rnn // N_LRU_BLOCKS
    fix2 = lambda s, j: (0, 0)
    fix3 = lambda s, j: (0, 0, 0)
    per_seq = lambda s, j: (s, 0, 0)
    return pl.pallas_call(
        functools.partial(_lru_kernel, tt=tt, valid_last=valid_last),
        grid=(n_seq, n_tiles),
        in_specs=[pl.BlockSpec((tt, c_rnn), lambda s, j: (b0 + s * bs + j, 0)),
                  pl.BlockSpec((tt, c_rnn), lambda s, j: (b0 + s * bs + j, 1)),
                  pl.BlockSpec((None, CONV_W - 1, c_rnn), per_seq),
                  pl.BlockSpec((None, 1, c_rnn), per_seq),
                  pl.BlockSpec((CONV_W, c_rnn), fix2), pl.BlockSpec((1, c_rnn), fix2),
                  pl.BlockSpec((N_LRU_BLOCKS, bw, bw), fix3), pl.BlockSpec((1, c_rnn), fix2),
                  pl.BlockSpec((N_LRU_BLOCKS, bw, bw), fix3), pl.BlockSpec((1, c_rnn), fix2),
                  pl.BlockSpec((1, c_rnn), fix2), pl.BlockSpec(memory_space=pl.ANY)],
        out_specs=[pl.BlockSpec((tt, c_rnn), lambda s, j: (ob0 + s * bs + j, 0)),
                   pl.BlockSpec((None, CONV_W - 1, c_rnn), per_seq),
                   pl.BlockSpec((None, 1, c_rnn), per_seq)],
        out_shape=[jax.ShapeDtypeStruct(init.shape, BF16),
                   jax.ShapeDtypeStruct((n_seq, CONV_W - 1, c_rnn), F32),
                   jax.ShapeDtypeStruct((n_seq, 1, c_rnn), F32)],
        scratch_shapes=[pltpu.VMEM((SUBLANES + tt, c_rnn), F32), pltpu.VMEM((tt, c_rnn), F32),
                        pltpu.VMEM((tt, c_rnn), F32), pltpu.VMEM((SUBLANES, c_rnn), F32)],
        input_output_aliases={11: 0},
        compiler_params=_cparams("parallel", "arbitrary"),
        name="rglru_seq",
    )(u, u, cprev, hprev, cw, cb, gaw, gab, gxw, gxb, sp, init)


def _pack_halves(y):
    h = y.shape[1] // 2
    lo = lax.bitcast_convert_type(y[:, :h].astype(BF16).astype(F32), jnp.uint32)
    hi = lax.bitcast_convert_type(y[:, h:].astype(BF16).astype(F32), jnp.uint32)
    return (lo >> 16) | (hi & jnp.uint32(0xFFFF0000))


def _unpack_halves(w):
    lo = lax.bitcast_convert_type(w << 16, F32).astype(BF16)
    hi = lax.bitcast_convert_type(w & jnp.uint32(0xFFFF0000), F32).astype(BF16)
    return lo, hi


def _moe_kernel(be_ref, nu_ref, x_ref, w1_ref, w3_ref, w2_ref, o_ref, w1b, w3b, w2b):
    i = pl.program_id(0)

    @pl.when(i < nu_ref[0])
    def _():
        prev = be_ref[jnp.maximum(i - 1, 0)]

        @pl.when(jnp.logical_or(i == 0, be_ref[i] != prev))
        def _():
            w1b[...] = w1_ref[...].astype(BF16)
            w3b[...] = w3_ref[...].astype(BF16)
            w2b[...] = w2_ref[...].astype(BF16)

        x = x_ref[...]
        a = jnp.dot(x, w1b[...], preferred_element_type=F32)
        b = jnp.dot(x, w3b[...], preferred_element_type=F32)
        h = (_silu(a) * b).astype(BF16)
        o_ref[...] = _pack_halves(jnp.dot(h, w2b[...], preferred_element_type=F32))


def _moe_experts(x_sorted, w1, w3, w2, layer, block_expert, n_used):
    n_rows, hw = x_sorted.shape
    d, de = w1.shape[2], w1.shape[3]
    nb = n_rows // MOE_BLK
    blk = lambda i, be, nu: (jnp.minimum(i, nu[0] - 1), 0)
    wmap = lambda i, be, nu: (layer, be[jnp.minimum(i, nu[0] - 1)], 0, 0)
    grid_spec = pltpu.PrefetchScalarGridSpec(
        num_scalar_prefetch=2,
        grid=(nb,),
        in_specs=[pl.BlockSpec((MOE_BLK, hw), blk),
                  pl.BlockSpec((None, None, d, de), wmap), pl.BlockSpec((None, None, d, de), wmap),
                  pl.BlockSpec((None, None, de, d), wmap)],
        out_specs=pl.BlockSpec((MOE_BLK, d // 2), blk),
        scratch_shapes=[pltpu.VMEM((d, de), BF16), pltpu.VMEM((d, de), BF16), pltpu.VMEM((de, d), BF16)],
    )
    return pl.pallas_call(
        _moe_kernel,
        grid_spec=grid_spec,
        out_shape=jax.ShapeDtypeStruct((n_rows, d // 2), jnp.uint32),
        compiler_params=_cparams("arbitrary"),
        name="moe_experts",
    )(block_expert, n_used, x_sorted, w1, w3, w2)


def _route(logits, router_bias, valid, n_experts):
    t = logits.shape[0]
    per = n_experts // N_GROUPS
    scores = jax.nn.sigmoid(logits)
    biased = scores + router_bias.astype(F32)
    grp = biased.reshape(t, N_GROUPS, per)
    m1 = jnp.max(grp, -1, keepdims=True)
    is_max = grp == m1
    first = is_max & (jnp.cumsum(is_max.astype(jnp.int32), -1) == 1)
    m2 = jnp.max(jnp.where(first, -jnp.inf, grp), -1)
    grp_score = m1[..., 0] + m2

    def rank(v):
        n = v.shape[-1]
        idx = jnp.arange(n)
        vi, vj = v[:, :, None], v[:, None, :]
        beaten = (vj > vi) | ((vj == vi) & (idx[None, None, :] < idx[None, :, None]))
        return jnp.sum(beaten.astype(jnp.int32), -1)

    gsel = rank(grp_score) < TOPK_GROUPS
    emask = jnp.repeat(gsel, per, axis=-1)
    e_rank = rank(jnp.where(emask, biased, NEG_INF))
    slot = e_rank[:, None, :] == jnp.arange(TOPK)[None, :, None]
    eidx = jnp.sum(jnp.where(slot, jnp.arange(n_experts)[None, None, :], 0), -1)
    g = jnp.sum(jnp.where(slot, scores[:, None, :], 0.0), -1)
    g = g / (jnp.sum(g, -1, keepdims=True) + 1e-20) * ROUTED_SCALE
    g = jnp.where(valid[:, None], g, 0.0)
    member = ((e_rank < TOPK) & valid[:, None]).astype(jnp.int32)
    return eidx.astype(jnp.int32), g, member


def _dispatch_plan(eidx, member, valid, n_experts, n_rows):
    t = eidx.shape[0]
    tk = t * TOPK
    nt = t // CMB_TILE
    i32 = jnp.int32
    flat_e = jnp.where(valid[:, None], eidx, n_experts).reshape(tk)
    order = jnp.argsort(flat_e).astype(i32)
    tok_sorted = order // TOPK
    m3 = member.reshape(nt, CMB_TILE, n_experts)
    cnt = jnp.sum(m3, axis=1)
    rlen = (cnt + RUN_ALIGN - 1) // RUN_ALIGN * RUN_ALIGN
    rank = (jnp.cumsum(m3, axis=1) - m3).reshape(t, n_experts)
    seg = jnp.sum(rlen, axis=0)
    padded = (seg + MOE_BLK - 1) // MOE_BLK * MOE_BLK
    pad_end = jnp.cumsum(padded)
    pad_start = pad_end - padded
    run_off = jnp.cumsum(rlen, axis=0) - rlen
    run_start = pad_start[None, :] + run_off
    lbase = jnp.cumsum(rlen, axis=1) - rlen
    tile_of = jnp.arange(t, dtype=i32) // CMB_TILE
    loc = jnp.take_along_axis(lbase[tile_of] + rank, eidx, axis=1)
    loc = jnp.where(valid[:, None], loc, -1).astype(i32)

    counts = jnp.sum(cnt, axis=0)
    grp_start = jnp.cumsum(counts) - counts
    before = jnp.cumsum(cnt, axis=0) - cnt
    nb = n_rows // MOE_BLK
    blk_row = jnp.arange(nb, dtype=i32) * MOE_BLK
    block_expert = jnp.minimum(jnp.sum((pad_end[None, :] <= blk_row[:, None]).astype(i32), axis=1),
                               n_experts - 1).astype(i32)
    g_e = jnp.repeat(block_expert, MOE_BLK // RUN_ALIGN)
    g_off = jnp.arange(n_rows // RUN_ALIGN, dtype=i32) * RUN_ALIGN - pad_start[g_e]
    g_tile = jnp.sum((run_off.T[g_e] <= g_off[:, None]).astype(i32), axis=1) - 1
    flat = g_tile * n_experts + g_e
    within = (g_off - run_off.reshape(-1)[flat])[:, None] + jnp.arange(RUN_ALIGN, dtype=i32)[None, :]
    src = jnp.clip((grp_start[g_e] + before.reshape(-1)[flat])[:, None] + within, 0, tk - 1)
    row_tok = jnp.where(within < cnt.reshape(-1)[flat][:, None], tok_sorted[src], 0)
    row_tok = row_tok.reshape(n_rows).astype(i32)
    n_used = jnp.maximum(pad_end[-1] // MOE_BLK, 1).astype(i32).reshape(1)
    n_chunks = (rlen // RUN_ALIGN).astype(i32)
    return (row_tok, loc, run_start.reshape(-1).astype(i32), n_chunks.reshape(-1),
            lbase.reshape(-1).astype(i32), jnp.sum(n_chunks, axis=1).astype(i32), block_expert, n_used)


def _split_bf16(w):
    hi = w.astype(BF16)
    lo = (w - hi.astype(F32)).astype(BF16)
    return hi, lo


def kernel(x_prompt, x_sample, cache_meta_k, cache_meta_v, cache_win_k, cache_win_v, state_conv, state_h, meta_tokens, ln_g, ln_b, attn_w_in, attn_w_out, attn_sink, lru_w_in, lru_conv_w, lru_conv_b, lru_gate_a_w, lru_gate_a_b, lru_gate_x_w, lru_gate_x_b, lru_lambda, lru_w_out, moe_router_w, moe_router_bias, moe_w1, moe_w3, moe_w2, moe_shared_w1, moe_shared_w3, moe_shared_w2):
    bsz, seq, d = x_prompt.shape
    dec_b, dec_s, _ = x_sample.shape
    n_meta = meta_tokens.shape[0]
    depth = ln_g.shape[0]
    n_experts = moe_router_w.shape[2]
    past_len = PAST_LEN
    n_heads = d // HEAD_DIM
    n_kv = n_heads // GROUP
    q_w, kv_w = n_heads * HEAD_DIM, n_kv * HEAD_DIM
    c_rnn = lru_w_in.shape[2] // 2
    alpha = (2 * depth) ** 0.25

    t_real = bsz * seq
    s_base, s_rows = t_real, dec_b * dec_s
    m_base = s_base + s_rows
    t_pad = m_base + bsz * META_BLK
    assert seq % LRU_TILE == 0 and seq % QBLK == 0 and s_rows % QBLK == 0 and t_pad % ROW_TILE == 0
    assert n_meta <= META_BLK and dec_s <= QBLK and q_w == d and kv_w % LANES == 0
    assert cache_win_k.shape[2] == WINDOW and WINDOW == QBLK and n_meta + WINDOW + dec_s <= 3 * QBLK
    nqb = seq // QBLK

    pos = np.zeros((t_pad,), np.float32)
    pos[:t_real] = np.tile(np.arange(seq) + n_meta, bsz)
    pos[s_base:m_base] = np.tile(past_len + n_meta + np.arange(dec_s), dec_b)
    valid_np = np.zeros((t_pad,), bool)
    valid_np[:m_base] = True
    for b in range(bsz):
        pos[m_base + b * META_BLK:m_base + b * META_BLK + n_meta] = np.arange(n_meta)
        valid_np[m_base + b * META_BLK:m_base + b * META_BLK + n_meta] = True
    valid = jnp.asarray(valid_np)

    half = ROT_DIM // 2
    freqs = ROPE_THETA ** (-jnp.arange(0, ROT_DIM, 2, dtype=F32) / ROT_DIM)
    ang = jnp.asarray(pos)[:, None] * freqs[None, :]
    cos, sin = jnp.cos(ang), jnp.sin(ang)
    ones = jnp.ones((t_pad, HEAD_DIM - ROT_DIM), F32)
    zeros_h = jnp.zeros((t_pad, half), F32)
    zeros_r = jnp.zeros((t_pad, HEAD_DIM - ROT_DIM), F32)
    reps = LANES // HEAD_DIM
    cos_t = jnp.tile(jnp.concatenate([cos, cos, ones], 1), (1, reps))
    sin_lo_t = jnp.tile(jnp.concatenate([zeros_h, sin, zeros_r], 1), (1, reps))
    sin_hi_t = jnp.tile(jnp.concatenate([-sin, zeros_h, zeros_r], 1), (1, reps))

    kj = np.arange(3 * QBLK)[:, None]
    qc = np.arange(QBLK)[None, :] // CHUNK
    key_chunk = np.where(kj < 2 * QBLK, kj // CHUNK - 2, 0)
    band_ok = (kj < 2 * QBLK) & (qc - key_chunk >= 0) & (qc - key_chunk <= WINDOW // CHUNK)
    meta_ok = (kj >= 2 * QBLK) & (kj < 2 * QBLK + n_meta)
    later = band_ok | meta_ok
    first = (band_ok & (kj >= QBLK)) | meta_ok
    only_meta = np.broadcast_to(meta_ok, later.shape)
    bias_p = jnp.asarray(np.where(np.stack([first, later, only_meta]), 0.0, NEG_INF).astype(np.float32))
    n_keys_s = n_meta + WINDOW + dec_s
    bias_s = jnp.asarray(np.where(np.broadcast_to(kj < n_keys_s, later.shape), 0.0, NEG_INF)
                         .astype(np.float32))[None]

    meta_rows = jnp.concatenate([meta_tokens.astype(F32), jnp.zeros((META_BLK - n_meta, d), F32)], 0)
    x = jnp.concatenate([x_prompt.reshape(t_real, d), x_sample.reshape(s_rows, d),
                         jnp.tile(meta_rows, (bsz, 1))], 0)
    xb = x.astype(BF16)

    n_assign = (t_real + s_rows + bsz * n_meta) * TOPK
    n_runs = (t_pad // CMB_TILE) * n_experts
    n_rows = -(-(n_assign + n_runs * (RUN_ALIGN - 1) + n_experts * (MOE_BLK - 1)) // MOE_BLK) * MOE_BLK

    outs = {k: [] for k in ("mk", "mv", "wk", "wv", "ks", "vs", "cp", "hp", "cs", "hs")}
    for l in range(depth):
        idx = l // 2
        if l % 2 == 0:
            w_in = attn_w_in[idx].astype(BF16)
            qkv = _qkv_proj(xb, w_in, cos_t, sin_lo_t, sin_hi_t, q_w + kv_w)
            k_col, v_col = q_w // kv_w, q_w // kv_w + 1
            sink_rows = jnp.repeat(attn_sink[idx].astype(F32).reshape(n_kv, 1, GROUP), QBLK, axis=2)

            q_s = qkv[s_base:m_base, :q_w].reshape(dec_b, dec_s, q_w)
            q_s = jnp.pad(q_s, ((0, 0), (0, QBLK - dec_s), (0, 0))).reshape(dec_b * QBLK, q_w)
            k_new = qkv[s_base:m_base, q_w:q_w + kv_w].reshape(dec_b, dec_s, kv_w)
            v_new = qkv[s_base:m_base, q_w + kv_w:].reshape(dec_b, dec_s, kv_w)
            pad_k = jnp.zeros((dec_b, 3 * QBLK - n_keys_s, kv_w), F32)
            k_s = jnp.concatenate([cache_meta_k[idx].reshape(dec_b, n_meta, kv_w).astype(F32),
                                   cache_win_k[idx].reshape(dec_b, WINDOW, kv_w).astype(F32), k_new, pad_k], 1)
            v_s = jnp.concatenate([cache_meta_v[idx].reshape(dec_b, n_meta, kv_w).astype(F32),
                                   cache_win_v[idx].reshape(dec_b, WINDOW, kv_w).astype(F32), v_new, pad_k], 1)
            kv_s = jnp.concatenate([k_s, v_s], axis=2).reshape(dec_b * 3 * QBLK, 2 * kv_w)
            o_s = _attention(q_s, kv_s, dec_b, lambda i: i,
                             [lambda i: 3 * i, lambda i: 3 * i + 1, lambda i: 3 * i + 2], bias_s, lambda i: 0,
                             sink_rows, jnp.zeros((dec_b * QBLK, q_w), BF16), lambda i: i, q_w, kv_w, 0, 1,
                             "attn_sample")
            o_s = o_s.reshape(dec_b, QBLK, q_w)[:, :dec_s].reshape(s_rows, q_w)

            steps = nqb + 1
            is_meta = lambda i: (i % steps) == nqb
            bidx = lambda i: i // steps
            pblk = lambda i: i % steps
            frame_blk = lambda i: bidx(i) * nqb + jnp.minimum(pblk(i), nqb - 1)
            meta_blk = lambda i: m_base // QBLK + bidx(i)
            q_map = lambda i: jnp.where(is_meta(i), meta_blk(i), frame_blk(i))
            prev_map = lambda i: bidx(i) * nqb + jnp.clip(pblk(i) - 1, 0, nqb - 1)
            bias_map = lambda i: jnp.where(is_meta(i), 2, jnp.minimum(pblk(i), 1))
            o_init = jnp.concatenate([jnp.zeros((t_real, q_w), BF16), o_s,
                                      jnp.zeros((t_pad - m_base, q_w), BF16)], 0)
            o = _attention(qkv, qkv, bsz * steps, q_map, [prev_map, frame_blk, meta_blk], bias_p, bias_map,
                           sink_rows, o_init, q_map, q_w, kv_w, k_col, v_col, "attn_prompt")

            kp = qkv[:t_real, q_w:q_w + kv_w].reshape(bsz, seq, n_kv, HEAD_DIM)
            vp = qkv[:t_real, q_w + kv_w:].reshape(bsz, seq, n_kv, HEAD_DIM)
            km = qkv[m_base:, q_w:q_w + kv_w].reshape(bsz, META_BLK, n_kv, HEAD_DIM)[:, :n_meta]
            vm = qkv[m_base:, q_w + kv_w:].reshape(bsz, META_BLK, n_kv, HEAD_DIM)[:, :n_meta]
            outs["mk"].append(km); outs["mv"].append(vm)
            outs["wk"].append(kp[:, -WINDOW:]); outs["wv"].append(vp[:, -WINDOW:])
            outs["ks"].append(k_new.reshape(dec_b, dec_s, n_kv, HEAD_DIM))
            outs["vs"].append(v_new.reshape(dec_b, dec_s, n_kv, HEAD_DIM))
            w_out = attn_w_out[idx].astype(BF16)
        else:
            u = _matmul(xb, lru_w_in[idx].astype(BF16), 1024, "lru_in_proj")
            sp = jax.nn.softplus(-lru_lambda[idx].astype(F32)).reshape(1, c_rnn)
            wts = (lru_conv_w[idx].astype(F32), lru_conv_b[idx].astype(F32).reshape(1, c_rnn),
                   lru_gate_a_w[idx].astype(BF16), lru_gate_a_b[idx].astype(F32).reshape(1, c_rnn),
                   lru_gate_x_w[idx].astype(BF16), lru_gate_x_b[idx].astype(F32).reshape(1, c_rnn), sp)
            zc = jnp.zeros((bsz, CONV_W - 1, c_rnn), F32)
            zh = jnp.zeros((bsz, 1, c_rnn), F32)
            y_m, c_m, h_m = _lru_seq(u, bsz, 1, META_BLK, m_base, META_BLK, n_meta,
                                     jnp.zeros((bsz * META_BLK, c_rnn), BF16), 0, zc, zh, wts)
            y_s, c_s, h_s = _lru_seq(u, dec_b, 1, dec_s, s_base, dec_s, dec_s,
                                     jnp.zeros((s_rows, c_rnn), BF16), 0,
                                     state_conv[idx].astype(F32), state_h[idx].astype(F32).reshape(dec_b, 1, c_rnn),
                                     wts)
            o_init = jnp.concatenate([jnp.zeros((t_real, c_rnn), BF16), y_s, y_m], 0)
            o, c_p, h_p = _lru_seq(u, bsz, seq // LRU_TILE, LRU_TILE, 0, seq, LRU_TILE, o_init, 0,
                                   c_m, h_m, wts)
            outs["cp"].append(c_p); outs["hp"].append(h_p.reshape(bsz, c_rnn))
            outs["cs"].append(c_s); outs["hs"].append(h_s.reshape(dec_b, c_rnn))
            w_out = lru_w_out[idx].astype(BF16)

        ne_pad = -(-n_experts // LANES) * LANES
        wr = jnp.pad(moe_router_w[l].astype(F32), ((0, 0), (0, ne_pad - n_experts)))
        wr_hi, wr_lo = _split_bf16(wr)
        x1, x1b, logits = _proj_ln(o, w_out, x, ln_g[l, 0].astype(F32).reshape(1, d),
                                   ln_b[l, 0].astype(F32).reshape(1, d), wr_hi, wr_lo, alpha)

        eidx, gate, member = _route(logits[:, :n_experts], moe_router_bias[l], valid, n_experts)
        (row_tok, loc, run_start, n_chunks, lbase, tile_chunks, block_expert, n_used) = _dispatch_plan(
            eidx, member, valid, n_experts, n_rows)
        x_sorted = x1b.at[row_tok].get(mode="promise_in_bounds")
        y_sorted = _moe_experts(x_sorted, moe_w1, moe_w3, moe_w2, l, block_expert, n_used)

        w13 = jnp.concatenate([moe_shared_w1[l], moe_shared_w3[l]], axis=1).astype(BF16)
        x, xb = _combine_ln(x1b, x1, loc, gate, y_sorted, run_start, n_chunks, lbase, tile_chunks, w13,
                            moe_shared_w2[l].astype(BF16), ln_g[l, 1].astype(F32).reshape(1, d),
                            ln_b[l, 1].astype(F32).reshape(1, d), alpha, n_experts)

    y_prompt = x[:t_real].reshape(bsz, seq, d)
    y_sample = x[s_base:m_base].reshape(dec_b, dec_s, d)
    st = lambda k: jnp.stack(outs[k])
    return (y_prompt, y_sample, st("mk"), st("mv"), st("wk"), st("wv"), st("ks"), st("vs"),
            st("cp"), st("hp"), st("cs"), st("hs"))
```

```python
import functools
import math

import numpy as np
import jax
import jax.numpy as jnp
from jax import lax
from jax.experimental import pallas as pl
from jax.experimental.pallas import tpu as pltpu

F32 = jnp.float32
BF16 = jnp.bfloat16

PAST_LEN = 1024
CHUNK = 64
WINDOW = 128
HEAD_DIM = 64
GROUP = 8
ROT_DIM = HEAD_DIM // 4
ROPE_THETA = 500000.0
ATTN_SCALE = HEAD_DIM ** -0.5
N_LRU_BLOCKS = 8
CONV_W = 4
LRU_C = 8.0
N_GROUPS = 8
TOPK_GROUPS = 4
TOPK = 8
ROUTED_SCALE = 2.5
LN_EPS = 1e-5
NEG_INF = -1e30

LANES = 128
SUBLANES = 8
QBLK = 2 * CHUNK
META_BLK = 128
ROW_TILE = 512
LN_TILE = 256
LRU_TILE = 256
SCAN_LANES = 512
MOE_BLK = 512
CMB_TILE = 256
RUN_ALIGN = SUBLANES
VMEM_LIMIT = 56 * 1024 * 1024


def _cparams(*sem):
    return pltpu.CompilerParams(dimension_semantics=sem, vmem_limit_bytes=VMEM_LIMIT)


def _mm_kernel(x_ref, w_ref, o_ref):
    o_ref[...] = jnp.dot(x_ref[...], w_ref[...], preferred_element_type=F32)


def _matmul(x, w, tn, name):
    m, k = x.shape
    n = w.shape[1]
    tm = ROW_TILE
    return pl.pallas_call(
        _mm_kernel,
        grid=(n // tn, m // tm),
        in_specs=[pl.BlockSpec((tm, k), lambda j, i: (i, 0)),
                  pl.BlockSpec((k, tn), lambda j, i: (0, j))],
        out_specs=pl.BlockSpec((tm, tn), lambda j, i: (i, j)),
        out_shape=jax.ShapeDtypeStruct((m, n), F32),
        compiler_params=_cparams("parallel", "parallel"),
        name=name,
    )(x, w)


def _qkv_kernel(x_ref, w_ref, c_ref, s1_ref, s2_ref, o_ref, *, tn, rope_cols):
    j = pl.program_id(0)
    acc = jnp.dot(x_ref[...], w_ref[...], preferred_element_type=F32)
    c, s1, s2 = c_ref[...], s1_ref[...], s2_ref[...]
    lane = lax.broadcasted_iota(jnp.int32, (acc.shape[0], LANES), 1)
    for g in range(tn // LANES):
        a = acc[:, g * LANES:(g + 1) * LANES]
        lo = pltpu.roll(a, ROT_DIM // 2, 1)
        hi = pltpu.roll(a, LANES - ROT_DIM // 2, 1)
        roped = a * c + lo * s1 + hi * s2
        col = j * tn + g * LANES + lane
        o_ref[:, g * LANES:(g + 1) * LANES] = jnp.where(col < rope_cols, roped, a)


def _qkv_proj(xb, w, cos_t, sin_lo_t, sin_hi_t, rope_cols):
    m, k = xb.shape
    n = w.shape[1]
    tm = ROW_TILE
    tn = 512 if n % 512 == 0 else 256
    assert n % tn == 0 and m % tm == 0
    tab = pl.BlockSpec((tm, LANES), lambda j, i: (i, 0))
    return pl.pallas_call(
        functools.partial(_qkv_kernel, tn=tn, rope_cols=rope_cols),
        grid=(n // tn, m // tm),
        in_specs=[pl.BlockSpec((tm, k), lambda j, i: (i, 0)),
                  pl.BlockSpec((k, tn), lambda j, i: (0, j)),
                  tab, tab, tab],
        out_specs=pl.BlockSpec((tm, tn), lambda j, i: (i, j)),
        out_shape=jax.ShapeDtypeStruct((m, n), F32),
        compiler_params=_cparams("parallel", "parallel"),
        name="qkv_rope",
    )(xb, w, cos_t, sin_lo_t, sin_hi_t)


def _layer_norm_rows(z, g, b):
    mu = jnp.mean(z, -1, keepdims=True)
    d = z - mu
    var = jnp.mean(d * d, -1, keepdims=True)
    return d * lax.rsqrt(var + LN_EPS) * g + b


def _proj_ln_kernel(a_ref, w_ref, x_ref, g_ref, b_ref, wrh_ref, wrl_ref, o_ref, ob_ref, lg_ref, *, alpha):
    acc = jnp.dot(a_ref[...], w_ref[...], preferred_element_type=F32)
    y = _layer_norm_rows(alpha * x_ref[...] + acc, g_ref[...], b_ref[...])
    o_ref[...] = y
    yb = y.astype(BF16)
    ob_ref[...] = yb
    ylo = (y - yb.astype(F32)).astype(BF16)
    wrh = wrh_ref[...]
    lg_ref[...] = (jnp.dot(yb, wrh, preferred_element_type=F32)
                   + jnp.dot(ylo, wrh, preferred_element_type=F32)
                   + jnp.dot(yb, wrl_ref[...], preferred_element_type=F32))


def _proj_ln(a, w, x, g, b, wr_hi, wr_lo, alpha):
    m, k = a.shape
    d = w.shape[1]
    ne = wr_hi.shape[1]
    tm = LN_TILE
    row = lambda i: (i, 0)
    fix = lambda i: (0, 0)
    return pl.pallas_call(
        functools.partial(_proj_ln_kernel, alpha=alpha),
        grid=(m // tm,),
        in_specs=[pl.BlockSpec((tm, k), row), pl.BlockSpec((k, d), fix), pl.BlockSpec((tm, d), row),
                  pl.BlockSpec((1, d), fix), pl.BlockSpec((1, d), fix),
                  pl.BlockSpec((d, ne), fix), pl.BlockSpec((d, ne), fix)],
        out_specs=[pl.BlockSpec((tm, d), row), pl.BlockSpec((tm, d), row), pl.BlockSpec((tm, ne), row)],
        out_shape=[jax.ShapeDtypeStruct((m, d), F32), jax.ShapeDtypeStruct((m, d), BF16),
                   jax.ShapeDtypeStruct((m, ne), F32)],
        compiler_params=_cparams("parallel"),
        name="proj_ln_router",
    )(a, w, x, g, b, wr_hi, wr_lo)


def _silu(x):
    return x * (1.0 / (1.0 + jnp.exp(-x)))


def _combine_ln_kernel(rs_ref, nc_ref, lb_ref, tot_ref, xb_ref, x_ref, loc_ref, gate_ref, y_hbm, w13_ref, w2_ref,
                       g_ref, b_ref, o_ref, ob_ref, ybuf, sem, *, alpha, dh, n_experts):
    i = pl.program_id(0)
    rows = ybuf.shape[0]

    def chunk_copy(src_row, dst_row):
        return pltpu.make_async_copy(y_hbm.at[pl.ds(pl.multiple_of(src_row, RUN_ALIGN), RUN_ALIGN)],
                                     ybuf.at[pl.ds(pl.multiple_of(dst_row, RUN_ALIGN), RUN_ALIGN)], sem)

    ybuf[...] = jnp.zeros(ybuf.shape, ybuf.dtype)

    def per_expert(e, c):
        r = i * n_experts + e
        src, dst = rs_ref[r], lb_ref[r]

        def per_chunk(j, c2):
            chunk_copy(src + j * RUN_ALIGN, dst + j * RUN_ALIGN).start()
            return c2

        return lax.fori_loop(0, nc_ref[r], per_chunk, c)

    lax.fori_loop(0, n_experts, per_expert, 0)

    u = jnp.dot(xb_ref[...], w13_ref[...], preferred_element_type=F32)
    hs = (_silu(u[:, :dh]) * u[:, dh:]).astype(BF16)
    f = jnp.dot(hs, w2_ref[...], preferred_element_type=F32)

    loc = loc_ref[...]
    gate = gate_ref[...]
    col = lax.broadcasted_iota(jnp.int32, (loc.shape[0], rows), 1)
    sel = jnp.zeros((loc.shape[0], rows), F32)
    for k in range(TOPK):
        sel = jnp.where(col == loc[:, k:k + 1], gate[:, k:k + 1], sel)
    sel_hi = sel.astype(BF16)
    sel_lo = (sel - sel_hi.astype(F32)).astype(BF16)

    lax.fori_loop(0, tot_ref[i], lambda j, c: (chunk_copy(0, 0).wait(), c)[1], 0)

    y_lo, y_hi = _unpack_halves(ybuf[...])
    r_lo = (jnp.dot(sel_hi, y_lo, preferred_element_type=F32) + jnp.dot(sel_lo, y_lo, preferred_element_type=F32))
    r_hi = (jnp.dot(sel_hi, y_hi, preferred_element_type=F32) + jnp.dot(sel_lo, y_hi, preferred_element_type=F32))
    f = f + jnp.concatenate([r_lo, r_hi], axis=1)
    y = _layer_norm_rows(alpha * x_ref[...] + f, g_ref[...], b_ref[...])
    o_ref[...] = y
    ob_ref[...] = y.astype(BF16)


def _combine_ln(xb, x, loc, gate, y_sorted, run_start, n_chunks, lbase, tile_chunks, w13, w2, g, b, alpha,
                n_experts):
    m, d = x.shape
    dh = w2.shape[0]
    tm = CMB_TILE
    buf_rows = -(-(tm * TOPK + n_experts * (RUN_ALIGN - 1)) // 256) * 256
    row = lambda i, *_: (i, 0)
    fix = lambda i, *_: (0, 0)
    grid_spec = pltpu.PrefetchScalarGridSpec(
        num_scalar_prefetch=4,
        grid=(m // tm,),
        in_specs=[pl.BlockSpec((tm, d), row), pl.BlockSpec((tm, d), row), pl.BlockSpec((tm, TOPK), row),
                  pl.BlockSpec((tm, TOPK), row), pl.BlockSpec(memory_space=pl.ANY),
                  pl.BlockSpec((d, 2 * dh), fix), pl.BlockSpec((dh, d), fix),
                  pl.BlockSpec((1, d), fix), pl.BlockSpec((1, d), fix)],
        out_specs=[pl.BlockSpec((tm, d), row), pl.BlockSpec((tm, d), row)],
        scratch_shapes=[pltpu.VMEM((buf_rows, y_sorted.shape[1]), y_sorted.dtype), pltpu.SemaphoreType.DMA(())],
    )
    return pl.pallas_call(
        functools.partial(_combine_ln_kernel, alpha=alpha, dh=dh, n_experts=n_experts),
        grid_spec=grid_spec,
        out_shape=[jax.ShapeDtypeStruct((m, d), F32), jax.ShapeDtypeStruct((m, d), BF16)],
        compiler_params=_cparams("arbitrary"),
        name="combine_ln",
    )(run_start, n_chunks, lbase, tile_chunks, xb, x, loc, gate, y_sorted, w13, w2, g, b)


def _attn_kernel(q_ref, k0_ref, k1_ref, k2_ref, v0_ref, v1_ref, v2_ref, bias_ref, sink_ref, init_ref, o_ref,
                 ot_ref, *, n_kv):
    del init_ref
    hd = HEAD_DIM
    qt = (q_ref[...] * ATTN_SCALE).T.astype(BF16)
    kb = jnp.concatenate([k0_ref[...], k1_ref[...], k2_ref[...]], 0).astype(BF16)
    vt = jnp.concatenate([v0_ref[...], v1_ref[...], v2_ref[...]], 0).T.astype(BF16)
    bias = bias_ref[0]
    bias = jnp.concatenate([bias] * GROUP, axis=1)
    for h in range(n_kv):
        kh = kb[:, h * hd:(h + 1) * hd]
        qth = jnp.concatenate(
            [qt[(h * GROUP + g) * hd:(h * GROUP + g + 1) * hd, :] for g in range(GROUP)], axis=1)
        s = jnp.dot(kh, qth, preferred_element_type=F32) + bias
        sink = sink_ref[h]
        m = jnp.maximum(jnp.max(s, axis=0, keepdims=True), sink)
        p = jnp.exp(s - m)
        den = jnp.sum(p, axis=0, keepdims=True) + jnp.exp(sink - m)
        ot = jnp.dot(vt[h * hd:(h + 1) * hd, :], p.astype(BF16), preferred_element_type=F32)
        ot = ot * (1.0 / den)
        for g in range(GROUP):
            r0 = (h * GROUP + g) * hd
            ot_ref[r0:r0 + hd, :] = ot[:, g * QBLK:(g + 1) * QBLK]
    o_ref[...] = ot_ref[...].T.astype(BF16)


def _attention(q_arr, k_arr, n_steps, q_map, kv_maps, bias, bias_map, sink_rows, init, out_map,
               q_w, kv_w, k_col, v_col, name):
    n_kv = kv_w // HEAD_DIM
    kspecs = [pl.BlockSpec((QBLK, kv_w), (lambda i, f=f: (f(i), k_col))) for f in kv_maps]
    vspecs = [pl.BlockSpec((QBLK, kv_w), (lambda i, f=f: (f(i), v_col))) for f in kv_maps]
    nk = 3 * QBLK
    return pl.pallas_call(
        functools.partial(_attn_kernel, n_kv=n_kv),
        grid=(n_steps,),
        in_specs=[pl.BlockSpec((QBLK, q_w), lambda i: (q_map(i), 0))] + kspecs + vspecs + [
            pl.BlockSpec((1, nk, QBLK), lambda i: (bias_map(i), 0, 0)),
            pl.BlockSpec((n_kv, 1, GROUP * QBLK), lambda i: (0, 0, 0)),
            pl.BlockSpec(memory_space=pl.ANY)],
        out_specs=pl.BlockSpec((QBLK, q_w), lambda i: (out_map(i), 0)),
        out_shape=jax.ShapeDtypeStruct(init.shape, BF16),
        scratch_shapes=[pltpu.VMEM((q_w, QBLK), F32)],
        input_output_aliases={9: 0},
        compiler_params=_cparams("parallel"),
        name=name,
    )(q_arr, k_arr, k_arr, k_arr, k_arr, k_arr, k_arr, bias, sink_rows, init)


def _gelu_tanh(x):
    c = math.sqrt(2.0 / math.pi)
    return x * (0.5 * (1.0 + jnp.tanh(c * (x + 0.044715 * (x * x * x)))))


def _sigmoid(x):
    return 1.0 / (1.0 + jnp.exp(-x))


def _lru_kernel(xb_ref, gb_ref, cprev_ref, hprev_ref, cw_ref, cb_ref, gaw_ref, gab_ref, gxw_ref, gxb_ref,
                sp_ref, init_ref, y_ref, cnew_ref, hnew_ref, xcat, a_scr, b_scr, hcar, *, tt, valid_last):
    del init_ref
    j = pl.program_id(1)
    c_rnn = xb_ref.shape[1]
    bw = c_rnn // N_LRU_BLOCKS
    tail = SUBLANES

    @pl.when(j == 0)
    def _():
        xcat[0:tail, :] = jnp.zeros((tail, c_rnn), F32)
        xcat[tail - (CONV_W - 1):tail, :] = cprev_ref[...]
        hcar[...] = jnp.broadcast_to(hprev_ref[...], (SUBLANES, c_rnn))

    xcat[tail:tail + tt, :] = xb_ref[...]
    cw = cw_ref[...]
    xc = cb_ref[...]
    for tap in range(CONV_W):
        off = tail - (CONV_W - 1) + tap
        xc = xc + xcat[off:off + tt, :] * cw[tap:tap + 1, :]
    cnew_ref[...] = xcat[tail + valid_last - (CONV_W - 1):tail + valid_last, :]
    xcat[0:tail, :] = xcat[tt:tt + tail, :]

    xcb = xc.astype(BF16)
    rs, gs = [], []
    for n in range(N_LRU_BLOCKS):
        xs = xcb[:, n * bw:(n + 1) * bw]
        rs.append(jnp.dot(xs, gaw_ref[n], preferred_element_type=F32))
        gs.append(jnp.dot(xs, gxw_ref[n], preferred_element_type=F32))
    r = _sigmoid(jnp.concatenate(rs, axis=1) + gab_ref[...])
    gi = _sigmoid(jnp.concatenate(gs, axis=1) + gxb_ref[...])
    log_a = (-LRU_C * r) * sp_ref[...]
    a = jnp.exp(log_a)
    a_scr[...] = a
    b_scr[...] = jnp.sqrt(-jnp.tanh(log_a) * (a * a + 1.0)) * (gi * xc)

    row = lax.broadcasted_iota(jnp.int32, (SUBLANES, SCAN_LANES), 0)
    for c in range(c_rnn // SCAN_LANES):
        cs = slice(c * SCAN_LANES, (c + 1) * SCAN_LANES)

        def body(i, carry, cs=cs):
            r0 = pl.multiple_of(i * SUBLANES, SUBLANES)
            av = a_scr[pl.ds(r0, SUBLANES), cs]
            bv = b_scr[pl.ds(r0, SUBLANES), cs]
            for k in (1, 2, 4):
                a_sh = jnp.where(row >= k, pltpu.roll(av, k, 0), 1.0)
                b_sh = jnp.where(row >= k, pltpu.roll(bv, k, 0), 0.0)
                bv = av * b_sh + bv
                av = av * a_sh
            hv = av * carry + bv
            b_scr[pl.ds(r0, SUBLANES), cs] = hv
            return jnp.broadcast_to(hv[SUBLANES - 1:SUBLANES, :], (SUBLANES, SCAN_LANES))

        hcar[:, cs] = lax.fori_loop(0, tt // SUBLANES, body, hcar[:, cs])

    h = b_scr[...]
    hnew_ref[...] = b_scr[valid_last - 1:valid_last, :]
    y_ref[...] = (h * _gelu_tanh(gb_ref[...])).astype(BF16)


def _lru_seq(u, n_seq, n_tiles, tt, row0, seq_stride, valid_last, init, out_row0, cprev, hprev, wts):
    c_rnn = u.shape[1] // 2
    cw, cb, gaw, gab, gxw, gxb, sp = wts
    b0, bs, ob0 = row0 // tt, seq_stride // tt, out_row0 // tt
    bw = c_rnn // N_LRU_BLOCKS
    fix2 = lambda s, j: (0, 0)
    fix3 = lambda s, j: (0, 0, 0)
    per_seq = lambda s, j: (s, 0, 0)
    return pl.pallas_call(
        functools.partial(_lru_kernel, tt=tt, valid_last=valid_last),
        grid=(n_seq, n_tiles),
        in_specs=[pl.BlockSpec((tt, c_rnn), lambda s, j: (b0 + s * bs + j, 0)),
                  pl.BlockSpec((tt, c_rnn), lambda s, j: (b0 + s * bs + j, 1)),
                  pl.BlockSpec((None, CONV_W - 1, c_rnn), per_seq),
                  pl.BlockSpec((None, 1, c_rnn), per_seq),
                  pl.BlockSpec((CONV_W, c_rnn), fix2), pl.BlockSpec((1, c_rnn), fix2),
                  pl.BlockSpec((N_LRU_BLOCKS, bw, bw), fix3), pl.BlockSpec((1, c_rnn), fix2),
                  pl.BlockSpec((N_LRU_BLOCKS, bw, bw), fix3), pl.BlockSpec((1, c_rnn), fix2),
                  pl.BlockSpec((1, c_rnn), fix2), pl.BlockSpec(memory_space=pl.ANY)],
        out_specs=[pl.BlockSpec((tt, c_rnn), lambda s, j: (ob0 + s * bs + j, 0)),
                   pl.BlockSpec((None, CONV_W - 1, c_rnn), per_seq),
                   pl.BlockSpec((None, 1, c_rnn), per_seq)],
        out_shape=[jax.ShapeDtypeStruct(init.shape, BF16),
                   jax.ShapeDtypeStruct((n_seq, CONV_W - 1, c_rnn), F32),
                   jax.ShapeDtypeStruct((n_seq, 1, c_rnn), F32)],
        scratch_shapes=[pltpu.VMEM((SUBLANES + tt, c_rnn), F32), pltpu.VMEM((tt, c_rnn), F32),
                        pltpu.VMEM((tt, c_rnn), F32), pltpu.VMEM((SUBLANES, c_rnn), F32)],
        input_output_aliases={11: 0},
        compiler_params=_cparams("parallel", "arbitrary"),
        name="rglru_seq",
    )(u, u, cprev, hprev, cw, cb, gaw, gab, gxw, gxb, sp, init)


def _pack_halves(y):
    h = y.shape[1] // 2
    lo = lax.bitcast_convert_type(y[:, :h].astype(BF16).astype(F32), jnp.uint32)
    hi = lax.bitcast_convert_type(y[:, h:].astype(BF16).astype(F32), jnp.uint32)
    return (lo >> 16) | (hi & jnp.uint32(0xFFFF0000))


def _unpack_halves(w):
    lo = lax.bitcast_convert_type(w << 16, F32).astype(BF16)
    hi = lax.bitcast_convert_type(w & jnp.uint32(0xFFFF0000), F32).astype(BF16)
    return lo, hi


DISPATCH_ROWS = 512


def _dispatch_kernel(rs_ref, nc_ref, lb_ref, tot_ref, zs_ref, zc_ref, x_ref, loct_ref, o_hbm, obuf, zbuf, sem, zsem,
                     *, n_experts):
    i = pl.program_id(0)
    _, rows, hw = obuf.shape
    tm = x_ref.shape[0]
    slot = i % 2

    def chunk_copy(s, src_row, dst_row):
        return pltpu.make_async_copy(obuf.at[s, pl.ds(pl.multiple_of(src_row, RUN_ALIGN), RUN_ALIGN)],
                                     o_hbm.at[pl.ds(pl.multiple_of(dst_row, RUN_ALIGN), RUN_ALIGN)], sem.at[s])

    def drain(s, n):
        lax.fori_loop(0, n, lambda j, c: (chunk_copy(s, 0, 0).wait(), c)[1], 0)

    @pl.when(i >= 2)
    def _():
        drain(slot, tot_ref[jnp.maximum(i - 2, 0)])

    loct = loct_ref[...]
    x = x_ref[...]
    x_lo, x_hi = x[:, :hw], x[:, hw:]
    for r0 in range(0, rows, DISPATCH_ROWS):
        rowi = r0 + lax.broadcasted_iota(jnp.int32, (DISPATCH_ROWS, tm), 0)
        p = jnp.zeros((DISPATCH_ROWS, tm), F32)
        for k in range(TOPK):
            p = jnp.where(rowi == loct[k:k + 1, :], 1.0, p)
        pb = p.astype(BF16)
        lo = jnp.dot(pb, x_lo, preferred_element_type=F32)
        hi = jnp.dot(pb, x_hi, preferred_element_type=F32)
        obuf[slot, r0:r0 + DISPATCH_ROWS, :] = _pack_halves(jnp.concatenate([lo, hi], axis=1))

    def per_expert(e, c):
        r = i * n_experts + e
        src, dst = lb_ref[r], rs_ref[r]

        def per_chunk(j, c2):
            chunk_copy(slot, src + j * RUN_ALIGN, dst + j * RUN_ALIGN).start()
            return c2

        return lax.fori_loop(0, nc_ref[r], per_chunk, c)

    lax.fori_loop(0, n_experts, per_expert, 0)

    @pl.when(i == pl.num_programs(0) - 1)
    def _():
        drain(slot, tot_ref[i])

        @pl.when(i >= 1)
        def _():
            drain(1 - slot, tot_ref[jnp.maximum(i - 1, 0)])

        zbuf[...] = jnp.zeros(zbuf.shape, zbuf.dtype)

        def zero_copy(dst_row):
            return pltpu.make_async_copy(zbuf, o_hbm.at[pl.ds(pl.multiple_of(dst_row, RUN_ALIGN), RUN_ALIGN)], zsem)

        def zero_expert(e, c):
            return lax.fori_loop(0, zc_ref[e], lambda j, c2: (zero_copy(zs_ref[e] + j * RUN_ALIGN).start(), c2)[1], c)

        def zero_wait(e, c):
            return lax.fori_loop(0, zc_ref[e], lambda j, c2: (zero_copy(0).wait(), c2)[1], c)

        lax.fori_loop(0, n_experts, zero_expert, 0)
        lax.fori_loop(0, n_experts, zero_wait, 0)


def _dispatch(xb, loc_t, run_start, n_chunks, lbase, tile_chunks, zero_start, zero_chunks, n_rows, n_experts):
    m, d = xb.shape
    tm = CMB_TILE
    buf_rows = -(-(tm * TOPK + n_experts * (RUN_ALIGN - 1)) // DISPATCH_ROWS) * DISPATCH_ROWS
    grid_spec = pltpu.PrefetchScalarGridSpec(
        num_scalar_prefetch=6,
        grid=(m // tm,),
        in_specs=[pl.BlockSpec((tm, d), lambda i, *_: (i, 0)),
                  pl.BlockSpec((None, TOPK, tm), lambda i, *_: (i, 0, 0))],
        out_specs=pl.BlockSpec(memory_space=pl.ANY),
        scratch_shapes=[pltpu.VMEM((2, buf_rows, d // 2), jnp.uint32), pltpu.VMEM((RUN_ALIGN, d // 2), jnp.uint32),
                        pltpu.SemaphoreType.DMA((2,)), pltpu.SemaphoreType.DMA(())],
    )
    return pl.pallas_call(
        functools.partial(_dispatch_kernel, n_experts=n_experts),
        grid_spec=grid_spec,
        out_shape=jax.ShapeDtypeStruct((n_rows, d // 2), jnp.uint32),
        compiler_params=_cparams("arbitrary"),
        name="moe_dispatch",
    )(run_start, n_chunks, lbase, tile_chunks, zero_start, zero_chunks, xb, loc_t)


def _moe_kernel(be_ref, nu_ref, x_ref, w1_ref, w3_ref, w2_ref, o_ref, w1b, w3b, w2b):
    i = pl.program_id(0)

    @pl.when(i < nu_ref[0])
    def _():
        prev = be_ref[jnp.maximum(i - 1, 0)]

        @pl.when(jnp.logical_or(i == 0, be_ref[i] != prev))
        def _():
            w1b[...] = w1_ref[...].astype(BF16)
            w3b[...] = w3_ref[...].astype(BF16)
            w2b[...] = w2_ref[...].astype(BF16)

        xlo, xhi = _unpack_halves(x_ref[...])
        hk = xlo.shape[1]
        a = (jnp.dot(xlo, w1b[0:hk, :], preferred_element_type=F32)
             + jnp.dot(xhi, w1b[hk:2 * hk, :], preferred_element_type=F32))
        b = (jnp.dot(xlo, w3b[0:hk, :], preferred_element_type=F32)
             + jnp.dot(xhi, w3b[hk:2 * hk, :], preferred_element_type=F32))
        h = (_silu(a) * b).astype(BF16)
        o_ref[...] = _pack_halves(jnp.dot(h, w2b[...], preferred_element_type=F32))


def _moe_experts(x_sorted, w1, w3, w2, layer, block_expert, n_used):
    n_rows, hw = x_sorted.shape
    d, de = w1.shape[2], w1.shape[3]
    nb = n_rows // MOE_BLK
    blk = lambda i, be, nu: (jnp.minimum(i, nu[0] - 1), 0)
    wmap = lambda i, be, nu: (layer, be[jnp.minimum(i, nu[0] - 1)], 0, 0)
    grid_spec = pltpu.PrefetchScalarGridSpec(
        num_scalar_prefetch=2,
        grid=(nb,),
        in_specs=[pl.BlockSpec((MOE_BLK, hw), blk),
                  pl.BlockSpec((None, None, d, de), wmap), pl.BlockSpec((None, None, d, de), wmap),
                  pl.BlockSpec((None, None, de, d), wmap)],
        out_specs=pl.BlockSpec((MOE_BLK, d // 2), blk),
        scratch_shapes=[pltpu.VMEM((d, de), BF16), pltpu.VMEM((d, de), BF16), pltpu.VMEM((de, d), BF16)],
    )
    return pl.pallas_call(
        _moe_kernel,
        grid_spec=grid_spec,
        out_shape=jax.ShapeDtypeStruct((n_rows, d // 2), jnp.uint32),
        compiler_params=_cparams("arbitrary"),
        name="moe_experts",
    )(block_expert, n_used, x_sorted, w1, w3, w2)


def _route(logits, router_bias, valid, n_experts):
    t = logits.shape[0]
    per = n_experts // N_GROUPS
    scores = jax.nn.sigmoid(logits)
    biased = scores + router_bias.astype(F32)
    grp = biased.reshape(t, N_GROUPS, per)
    m1 = jnp.max(grp, -1, keepdims=True)
    is_max = grp == m1
    first = is_max & (jnp.cumsum(is_max.astype(jnp.int32), -1) == 1)
    m2 = jnp.max(jnp.where(first, -jnp.inf, grp), -1)
    grp_score = m1[..., 0] + m2

    def rank(v):
        n = v.shape[-1]
        idx = jnp.arange(n)
        vi, vj = v[:, :, None], v[:, None, :]
        beaten = (vj > vi) | ((vj == vi) & (idx[None, None, :] < idx[None, :, None]))
        return jnp.sum(beaten.astype(jnp.int32), -1)

    gsel = rank(grp_score) < TOPK_GROUPS
    emask = jnp.repeat(gsel, per, axis=-1)
    e_rank = rank(jnp.where(emask, biased, NEG_INF))
    slot = e_rank[:, None, :] == jnp.arange(TOPK)[None, :, None]
    eidx = jnp.sum(jnp.where(slot, jnp.arange(n_experts)[None, None, :], 0), -1)
    g = jnp.sum(jnp.where(slot, scores[:, None, :], 0.0), -1)
    g = g / (jnp.sum(g, -1, keepdims=True) + 1e-20) * ROUTED_SCALE
    g = jnp.where(valid[:, None], g, 0.0)
    member = ((e_rank < TOPK) & valid[:, None]).astype(jnp.int32)
    return eidx.astype(jnp.int32), g, member


def _dispatch_plan(eidx, member, valid, n_experts, n_rows):
    t = eidx.shape[0]
    nt = t // CMB_TILE
    i32 = jnp.int32
    m3 = member.reshape(nt, CMB_TILE, n_experts)
    cnt = jnp.sum(m3, axis=1)
    rlen = (cnt + RUN_ALIGN - 1) // RUN_ALIGN * RUN_ALIGN
    rank = (jnp.cumsum(m3, axis=1) - m3).reshape(t, n_experts)
    seg = jnp.sum(rlen, axis=0)
    padded = (seg + MOE_BLK - 1) // MOE_BLK * MOE_BLK
    pad_end = jnp.cumsum(padded)
    pad_start = pad_end - padded
    run_off = jnp.cumsum(rlen, axis=0) - rlen
    run_start = pad_start[None, :] + run_off
    lbase = jnp.cumsum(rlen, axis=1) - rlen
    pos = (jnp.repeat(lbase, CMB_TILE, axis=0) + rank)
    hit = eidx[:, :, None] == jnp.arange(n_experts, dtype=i32)[None, None, :]
    loc = jnp.sum(jnp.where(hit, pos[:, None, :], 0), axis=-1)
    loc = jnp.where(valid[:, None], loc, -1).astype(i32)
    loc_t = loc.reshape(nt, CMB_TILE, TOPK).transpose(0, 2, 1)
    nb = n_rows // MOE_BLK
    blk_row = jnp.arange(nb, dtype=i32) * MOE_BLK
    block_expert = jnp.minimum(jnp.sum((pad_end[None, :] <= blk_row[:, None]).astype(i32), axis=1),
                               n_experts - 1).astype(i32)
    n_used = jnp.maximum(pad_end[-1] // MOE_BLK, 1).astype(i32).reshape(1)
    n_chunks = (rlen // RUN_ALIGN).astype(i32)
    return (loc, loc_t, run_start.reshape(-1).astype(i32), n_chunks.reshape(-1), lbase.reshape(-1).astype(i32),
            jnp.sum(n_chunks, axis=1).astype(i32), (pad_start + seg).astype(i32),
            ((padded - seg) // RUN_ALIGN).astype(i32), block_expert, n_used)


def _split_bf16(w):
    hi = w.astype(BF16)
    lo = (w - hi.astype(F32)).astype(BF16)
    return hi, lo


def kernel(x_prompt, x_sample, cache_meta_k, cache_meta_v, cache_win_k, cache_win_v, state_conv, state_h, meta_tokens, ln_g, ln_b, attn_w_in, attn_w_out, attn_sink, lru_w_in, lru_conv_w, lru_conv_b, lru_gate_a_w, lru_gate_a_b, lru_gate_x_w, lru_gate_x_b, lru_lambda, lru_w_out, moe_router_w, moe_router_bias, moe_w1, moe_w3, moe_w2, moe_shared_w1, moe_shared_w3, moe_shared_w2):
    bsz, seq, d = x_prompt.shape
    dec_b, dec_s, _ = x_sample.shape
    n_meta = meta_tokens.shape[0]
    depth = ln_g.shape[0]
    n_experts = moe_router_w.shape[2]
    past_len = PAST_LEN
    n_heads = d // HEAD_DIM
    n_kv = n_heads // GROUP
    q_w, kv_w = n_heads * HEAD_DIM, n_kv * HEAD_DIM
    c_rnn = lru_w_in.shape[2] // 2
    alpha = (2 * depth) ** 0.25

    t_real = bsz * seq
    s_base, s_rows = t_real, dec_b * dec_s
    m_base = s_base + s_rows
    t_pad = m_base + bsz * META_BLK
    assert seq % LRU_TILE == 0 and seq % QBLK == 0 and s_rows % QBLK == 0 and t_pad % ROW_TILE == 0
    assert n_meta <= META_BLK and dec_s <= QBLK and q_w == d and kv_w % LANES == 0
    assert cache_win_k.shape[2] == WINDOW and WINDOW == QBLK and n_meta + WINDOW + dec_s <= 3 * QBLK
    nqb = seq // QBLK

    pos = np.zeros((t_pad,), np.float32)
    pos[:t_real] = np.tile(np.arange(seq) + n_meta, bsz)
    pos[s_base:m_base] = np.tile(past_len + n_meta + np.arange(dec_s), dec_b)
    valid_np = np.zeros((t_pad,), bool)
    valid_np[:m_base] = True
    for b in range(bsz):
        pos[m_base + b * META_BLK:m_base + b * META_BLK + n_meta] = np.arange(n_meta)
        valid_np[m_base + b * META_BLK:m_base + b * META_BLK + n_meta] = True
    valid = jnp.asarray(valid_np)

    half = ROT_DIM // 2
    freqs = ROPE_THETA ** (-jnp.arange(0, ROT_DIM, 2, dtype=F32) / ROT_DIM)
    ang = jnp.asarray(pos)[:, None] * freqs[None, :]
    cos, sin = jnp.cos(ang), jnp.sin(ang)
    ones = jnp.ones((t_pad, HEAD_DIM - ROT_DIM), F32)
    zeros_h = jnp.zeros((t_pad, half), F32)
    zeros_r = jnp.zeros((t_pad, HEAD_DIM - ROT_DIM), F32)
    reps = LANES // HEAD_DIM
    cos_t = jnp.tile(jnp.concatenate([cos, cos, ones], 1), (1, reps))
    sin_lo_t = jnp.tile(jnp.concatenate([zeros_h, sin, zeros_r], 1), (1, reps))
    sin_hi_t = jnp.tile(jnp.concatenate([-sin, zeros_h, zeros_r], 1), (1, reps))

    kj = np.arange(3 * QBLK)[:, None]
    qc = np.arange(QBLK)[None, :] // CHUNK
    key_chunk = np.where(kj < 2 * QBLK, kj // CHUNK - 2, 0)
    band_ok = (kj < 2 * QBLK) & (qc - key_chunk >= 0) & (qc - key_chunk <= WINDOW // CHUNK)
    meta_ok = (kj >= 2 * QBLK) & (kj < 2 * QBLK + n_meta)
    later = band_ok | meta_ok
    first = (band_ok & (kj >= QBLK)) | meta_ok
    only_meta = np.broadcast_to(meta_ok, later.shape)
    bias_p = jnp.asarray(np.where(np.stack([first, later, only_meta]), 0.0, NEG_INF).astype(np.float32))
    n_keys_s = n_meta + WINDOW + dec_s
    bias_s = jnp.asarray(np.where(np.broadcast_to(kj < n_keys_s, later.shape), 0.0, NEG_INF)
                         .astype(np.float32))[None]

    meta_rows = jnp.concatenate([meta_tokens.astype(F32), jnp.zeros((META_BLK - n_meta, d), F32)], 0)
    x = jnp.concatenate([x_prompt.reshape(t_real, d), x_sample.reshape(s_rows, d),
                         jnp.tile(meta_rows, (bsz, 1))], 0)
    xb = x.astype(BF16)

    n_assign = (t_real + s_rows + bsz * n_meta) * TOPK
    n_runs = (t_pad // CMB_TILE) * n_experts
    n_rows = -(-(n_assign + n_runs * (RUN_ALIGN - 1) + n_experts * (MOE_BLK - 1)) // MOE_BLK) * MOE_BLK

    outs = {k: [] for k in ("mk", "mv", "wk", "wv", "ks", "vs", "cp", "hp", "cs", "hs")}
    for l in range(depth):
        idx = l // 2
        if l % 2 == 0:
            w_in = attn_w_in[idx].astype(BF16)
            qkv = _qkv_proj(xb, w_in, cos_t, sin_lo_t, sin_hi_t, q_w + kv_w)
            k_col, v_col = q_w // kv_w, q_w // kv_w + 1
            sink_rows = jnp.repeat(attn_sink[idx].astype(F32).reshape(n_kv, 1, GROUP), QBLK, axis=2)

            q_s = qkv[s_base:m_base, :q_w].reshape(dec_b, dec_s, q_w)
            q_s = jnp.pad(q_s, ((0, 0), (0, QBLK - dec_s), (0, 0))).reshape(dec_b * QBLK, q_w)
            k_new = qkv[s_base:m_base, q_w:q_w + kv_w].reshape(dec_b, dec_s, kv_w)
            v_new = qkv[s_base:m_base, q_w + kv_w:].reshape(dec_b, dec_s, kv_w)
            pad_k = jnp.zeros((dec_b, 3 * QBLK - n_keys_s, kv_w), F32)
            k_s = jnp.concatenate([cache_meta_k[idx].reshape(dec_b, n_meta, kv_w).astype(F32),
                                   cache_win_k[idx].reshape(dec_b, WINDOW, kv_w).astype(F32), k_new, pad_k], 1)
            v_s = jnp.concatenate([cache_meta_v[idx].reshape(dec_b, n_meta, kv_w).astype(F32),
                                   cache_win_v[idx].reshape(dec_b, WINDOW, kv_w).astype(F32), v_new, pad_k], 1)
            kv_s = jnp.concatenate([k_s, v_s], axis=2).reshape(dec_b * 3 * QBLK, 2 * kv_w)
            o_s = _attention(q_s, kv_s, dec_b, lambda i: i,
                             [lambda i: 3 * i, lambda i: 3 * i + 1, lambda i: 3 * i + 2], bias_s, lambda i: 0,
                             sink_rows, jnp.zeros((dec_b * QBLK, q_w), BF16), lambda i: i, q_w, kv_w, 0, 1,
                             "attn_sample")
            o_s = o_s.reshape(dec_b, QBLK, q_w)[:, :dec_s].reshape(s_rows, q_w)

            steps = nqb + 1
            is_meta = lambda i: (i % steps) == nqb
            bidx = lambda i: i // steps
            pblk = lambda i: i % steps
            frame_blk = lambda i: bidx(i) * nqb + jnp.minimum(pblk(i), nqb - 1)
            meta_blk = lambda i: m_base // QBLK + bidx(i)
            q_map = lambda i: jnp.where(is_meta(i), meta_blk(i), frame_blk(i))
            prev_map = lambda i: bidx(i) * nqb + jnp.clip(pblk(i) - 1, 0, nqb - 1)
            bias_map = lambda i: jnp.where(is_meta(i), 2, jnp.minimum(pblk(i), 1))
            o_init = jnp.concatenate([jnp.zeros((t_real, q_w), BF16), o_s,
                                      jnp.zeros((t_pad - m_base, q_w), BF16)], 0)
            o = _attention(qkv, qkv, bsz * steps, q_map, [prev_map, frame_blk, meta_blk], bias_p, bias_map,
                           sink_rows, o_init, q_map, q_w, kv_w, k_col, v_col, "attn_prompt")

            kp = qkv[:t_real, q_w:q_w + kv_w].reshape(bsz, seq, n_kv, HEAD_DIM)
            vp = qkv[:t_real, q_w + kv_w:].reshape(bsz, seq, n_kv, HEAD_DIM)
            km = qkv[m_base:, q_w:q_w + kv_w].reshape(bsz, META_BLK, n_kv, HEAD_DIM)[:, :n_meta]
            vm = qkv[m_base:, q_w + kv_w:].reshape(bsz, META_BLK, n_kv, HEAD_DIM)[:, :n_meta]
            outs["mk"].append(km); outs["mv"].append(vm)
            outs["wk"].append(kp[:, -WINDOW:]); outs["wv"].append(vp[:, -WINDOW:])
            outs["ks"].append(k_new.reshape(dec_b, dec_s, n_kv, HEAD_DIM))
            outs["vs"].append(v_new.reshape(dec_b, dec_s, n_kv, HEAD_DIM))
            w_out = attn_w_out[idx].astype(BF16)
        else:
            u = _matmul(xb, lru_w_in[idx].astype(BF16), 1024, "lru_in_proj")
            sp = jax.nn.softplus(-lru_lambda[idx].astype(F32)).reshape(1, c_rnn)
            wts = (lru_conv_w[idx].astype(F32), lru_conv_b[idx].astype(F32).reshape(1, c_rnn),
                   lru_gate_a_w[idx].astype(BF16), lru_gate_a_b[idx].astype(F32).reshape(1, c_rnn),
                   lru_gate_x_w[idx].astype(BF16), lru_gate_x_b[idx].astype(F32).reshape(1, c_rnn), sp)
            zc = jnp.zeros((bsz, CONV_W - 1, c_rnn), F32)
            zh = jnp.zeros((bsz, 1, c_rnn), F32)
            y_m, c_m, h_m = _lru_seq(u, bsz, 1, META_BLK, m_base, META_BLK, n_meta,
                                     jnp.zeros((bsz * META_BLK, c_rnn), BF16), 0, zc, zh, wts)
            y_s, c_s, h_s = _lru_seq(u, dec_b, 1, dec_s, s_base, dec_s, dec_s,
                                     jnp.zeros((s_rows, c_rnn), BF16), 0,
                                     state_conv[idx].astype(F32), state_h[idx].astype(F32).reshape(dec_b, 1, c_rnn),
                                     wts)
            o_init = jnp.concatenate([jnp.zeros((t_real, c_rnn), BF16), y_s, y_m], 0)
            o, c_p, h_p = _lru_seq(u, bsz, seq // LRU_TILE, LRU_TILE, 0, seq, LRU_TILE, o_init, 0,
                                   c_m, h_m, wts)
            outs["cp"].append(c_p); outs["hp"].append(h_p.reshape(bsz, c_rnn))
            outs["cs"].append(c_s); outs["hs"].append(h_s.reshape(dec_b, c_rnn))
            w_out = lru_w_out[idx].astype(BF16)

        ne_pad = -(-n_experts // LANES) * LANES
        wr = jnp.pad(moe_router_w[l].astype(F32), ((0, 0), (0, ne_pad - n_experts)))
        wr_hi, wr_lo = _split_bf16(wr)
        x1, x1b, logits = _proj_ln(o, w_out, x, ln_g[l, 0].astype(F32).reshape(1, d),
                                   ln_b[l, 0].astype(F32).reshape(1, d), wr_hi, wr_lo, alpha)

        eidx, gate, member = _route(logits[:, :n_experts], moe_router_bias[l], valid, n_experts)
        (loc, loc_t, run_start, n_chunks, lbase, tile_chunks, zero_start, zero_chunks, block_expert,
         n_used) = _dispatch_plan(eidx, member, valid, n_experts, n_rows)
        x_sorted = _dispatch(x1b, loc_t, run_start, n_chunks, lbase, tile_chunks, zero_start, zero_chunks,
                             n_rows, n_experts)
        y_sorted = _moe_experts(x_sorted, moe_w1, moe_w3, moe_w2, l, block_expert, n_used)

        w13 = jnp.concatenate([moe_shared_w1[l], moe_shared_w3[l]], axis=1).astype(BF16)
        x, xb = _combine_ln(x1b, x1, loc, gate, y_sorted, run_start, n_chunks, lbase, tile_chunks, w13,
                            moe_shared_w2[l].astype(BF16), ln_g[l, 1].astype(F32).reshape(1, d),
                            ln_b[l, 1].astype(F32).reshape(1, d), alpha, n_experts)

    y_prompt = x[:t_real].reshape(bsz, seq, d)
    y_sample = x[s_base:m_base].reshape(dec_b, dec_s, d)
    st = lambda k: jnp.stack(outs[k])
    return (y_prompt, y_sample, st("mk"), st("mv"), st("wk"), st("wv"), st("ks"), st("vs"),
            st("cp"), st("hp"), st("cs"), st("hs"))
```

```python
import functools
import math

import numpy as np
import jax
import jax.numpy as jnp
from jax import lax
from jax.experimental import pallas as pl
from jax.experimental.pallas import tpu as pltpu

F32 = jnp.float32
BF16 = jnp.bfloat16

PAST_LEN = 1024
CHUNK = 64
WINDOW = 128
HEAD_DIM = 64
GROUP = 8
ROT_DIM = HEAD_DIM // 4
ROPE_THETA = 500000.0
ATTN_SCALE = HEAD_DIM ** -0.5
N_LRU_BLOCKS = 8
CONV_W = 4
LRU_C = 8.0
N_GROUPS = 8
TOPK_GROUPS = 4
TOPK = 8
ROUTED_SCALE = 2.5
LN_EPS = 1e-5
NEG_INF = -1e30

LANES = 128
SUBLANES = 8
QBLK = 2 * CHUNK
META_BLK = 128
ROW_TILE = 512
LN_TILE = 256
LRU_TILE = 256
SCAN_LANES = 512
MOE_BLK = 512
CMB_TILE = 256
RUN_ALIGN = SUBLANES
VMEM_LIMIT = 56 * 1024 * 1024


def _cparams(*sem):
    return pltpu.CompilerParams(dimension_semantics=sem, vmem_limit_bytes=VMEM_LIMIT)


def _mm_kernel(x_ref, w_ref, o_ref):
    o_ref[...] = jnp.dot(x_ref[...], w_ref[...], preferred_element_type=F32)


def _matmul(x, w, tn, name):
    m, k = x.shape
    n = w.shape[1]
    tm = ROW_TILE
    return pl.pallas_call(
        _mm_kernel,
        grid=(n // tn, m // tm),
        in_specs=[pl.BlockSpec((tm, k), lambda j, i: (i, 0)),
                  pl.BlockSpec((k, tn), lambda j, i: (0, j))],
        out_specs=pl.BlockSpec((tm, tn), lambda j, i: (i, j)),
        out_shape=jax.ShapeDtypeStruct((m, n), F32),
        compiler_params=_cparams("parallel", "parallel"),
        name=name,
    )(x, w)


def _qkv_kernel(x_ref, w_ref, c_ref, s1_ref, s2_ref, o_ref, *, tn, rope_cols):
    j = pl.program_id(0)
    acc = jnp.dot(x_ref[...], w_ref[...], preferred_element_type=F32)
    c, s1, s2 = c_ref[...], s1_ref[...], s2_ref[...]
    lane = lax.broadcasted_iota(jnp.int32, (acc.shape[0], LANES), 1)
    for g in range(tn // LANES):
        a = acc[:, g * LANES:(g + 1) * LANES]
        lo = pltpu.roll(a, ROT_DIM // 2, 1)
        hi = pltpu.roll(a, LANES - ROT_DIM // 2, 1)
        roped = a * c + lo * s1 + hi * s2
        col = j * tn + g * LANES + lane
        o_ref[:, g * LANES:(g + 1) * LANES] = jnp.where(col < rope_cols, roped, a)


def _qkv_proj(xb, w, cos_t, sin_lo_t, sin_hi_t, rope_cols):
    m, k = xb.shape
    n = w.shape[1]
    tm = ROW_TILE
    tn = 512 if n % 512 == 0 else 256
    assert n % tn == 0 and m % tm == 0
    tab = pl.BlockSpec((tm, LANES), lambda j, i: (i, 0))
    return pl.pallas_call(
        functools.partial(_qkv_kernel, tn=tn, rope_cols=rope_cols),
        grid=(n // tn, m // tm),
        in_specs=[pl.BlockSpec((tm, k), lambda j, i: (i, 0)),
                  pl.BlockSpec((k, tn), lambda j, i: (0, j)),
                  tab, tab, tab],
        out_specs=pl.BlockSpec((tm, tn), lambda j, i: (i, j)),
        out_shape=jax.ShapeDtypeStruct((m, n), F32),
        compiler_params=_cparams("parallel", "parallel"),
        name="qkv_rope",
    )(xb, w, cos_t, sin_lo_t, sin_hi_t)


def _layer_norm_rows(z, g, b):
    mu = jnp.mean(z, -1, keepdims=True)
    d = z - mu
    var = jnp.mean(d * d, -1, keepdims=True)
    return d * lax.rsqrt(var + LN_EPS) * g + b


def _proj_ln_kernel(a_ref, w_ref, x_ref, g_ref, b_ref, wrh_ref, wrl_ref, o_ref, ob_ref, lg_ref, *, alpha):
    acc = jnp.dot(a_ref[...], w_ref[...], preferred_element_type=F32)
    y = _layer_norm_rows(alpha * x_ref[...] + acc, g_ref[...], b_ref[...])
    o_ref[...] = y
    yb = y.astype(BF16)
    ob_ref[...] = yb
    ylo = (y - yb.astype(F32)).astype(BF16)
    wrh = wrh_ref[...]
    lg_ref[...] = (jnp.dot(yb, wrh, preferred_element_type=F32)
                   + jnp.dot(ylo, wrh, preferred_element_type=F32)
                   + jnp.dot(yb, wrl_ref[...], preferred_element_type=F32))


def _proj_ln(a, w, x, g, b, wr_hi, wr_lo, alpha):
    m, k = a.shape
    d = w.shape[1]
    ne = wr_hi.shape[1]
    tm = LN_TILE
    row = lambda i: (i, 0)
    fix = lambda i: (0, 0)
    return pl.pallas_call(
        functools.partial(_proj_ln_kernel, alpha=alpha),
        grid=(m // tm,),
        in_specs=[pl.BlockSpec((tm, k), row), pl.BlockSpec((k, d), fix), pl.BlockSpec((tm, d), row),
                  pl.BlockSpec((1, d), fix), pl.BlockSpec((1, d), fix),
                  pl.BlockSpec((d, ne), fix), pl.BlockSpec((d, ne), fix)],
        out_specs=[pl.BlockSpec((tm, d), row), pl.BlockSpec((tm, d), row), pl.BlockSpec((tm, ne), row)],
        out_shape=[jax.ShapeDtypeStruct((m, d), F32), jax.ShapeDtypeStruct((m, d), BF16),
                   jax.ShapeDtypeStruct((m, ne), F32)],
        compiler_params=_cparams("parallel"),
        name="proj_ln_router",
    )(a, w, x, g, b, wr_hi, wr_lo)


def _sigmoid(x):
    return 0.5 * (jnp.tanh(0.5 * x) + 1.0)


def _silu(x):
    return x * _sigmoid(x)


COMBINE_ROWS = 512


def _combine_ln_kernel(rs_ref, nc_ref, lb_ref, tot_ref, xb_ref, x_ref, loc_ref, gate_ref, y_hbm, w13_ref, w2_ref,
                       g_ref, b_ref, o_ref, ob_ref, ybuf, sem, *, alpha, dh, n_experts):
    i = pl.program_id(0)
    rows = ybuf.shape[0]

    def chunk_copy(src_row, dst_row):
        return pltpu.make_async_copy(y_hbm.at[pl.ds(pl.multiple_of(src_row, RUN_ALIGN), RUN_ALIGN)],
                                     ybuf.at[pl.ds(pl.multiple_of(dst_row, RUN_ALIGN), RUN_ALIGN)], sem)

    ybuf[...] = jnp.zeros(ybuf.shape, ybuf.dtype)

    def per_expert(e, c):
        r = i * n_experts + e
        src, dst = rs_ref[r], lb_ref[r]

        def per_chunk(j, c2):
            chunk_copy(src + j * RUN_ALIGN, dst + j * RUN_ALIGN).start()
            return c2

        return lax.fori_loop(0, nc_ref[r], per_chunk, c)

    lax.fori_loop(0, n_experts, per_expert, 0)

    u = jnp.dot(xb_ref[...], w13_ref[...], preferred_element_type=F32)
    hs = (_silu(u[:, :dh]) * u[:, dh:]).astype(BF16)
    f = jnp.dot(hs, w2_ref[...], preferred_element_type=F32)

    lax.fori_loop(0, tot_ref[i], lambda j, c: (chunk_copy(0, 0).wait(), c)[1], 0)

    loc = loc_ref[...]
    gate = gate_ref[...]
    tm = loc.shape[0]
    r_lo = jnp.zeros((tm, ybuf.shape[1]), F32)
    r_hi = jnp.zeros((tm, ybuf.shape[1]), F32)
    for c0 in range(0, rows, COMBINE_ROWS):
        col = c0 + lax.broadcasted_iota(jnp.int32, (tm, COMBINE_ROWS), 1)
        sel = jnp.zeros((tm, COMBINE_ROWS), F32)
        for k in range(TOPK):
            sel = jnp.where(col == loc[:, k:k + 1], gate[:, k:k + 1], sel)
        sel_hi = sel.astype(BF16)
        sel_lo = (sel - sel_hi.astype(F32)).astype(BF16)
        y_lo, y_hi = _unpack_halves(ybuf[c0:c0 + COMBINE_ROWS, :])
        r_lo = r_lo + (jnp.dot(sel_hi, y_lo, preferred_element_type=F32)
                       + jnp.dot(sel_lo, y_lo, preferred_element_type=F32))
        r_hi = r_hi + (jnp.dot(sel_hi, y_hi, preferred_element_type=F32)
                       + jnp.dot(sel_lo, y_hi, preferred_element_type=F32))
    f = f + jnp.concatenate([r_lo, r_hi], axis=1)
    y = _layer_norm_rows(alpha * x_ref[...] + f, g_ref[...], b_ref[...])
    o_ref[...] = y
    ob_ref[...] = y.astype(BF16)


def _combine_ln(xb, x, loc, gate, y_sorted, run_start, n_chunks, lbase, tile_chunks, w13, w2, g, b, alpha,
                n_experts):
    m, d = x.shape
    dh = w2.shape[0]
    tm = CMB_TILE
    buf_rows = -(-(tm * TOPK + n_experts * (RUN_ALIGN - 1)) // 256) * 256
    row = lambda i, *_: (i, 0)
    fix = lambda i, *_: (0, 0)
    grid_spec = pltpu.PrefetchScalarGridSpec(
        num_scalar_prefetch=4,
        grid=(m // tm,),
        in_specs=[pl.BlockSpec((tm, d), row), pl.BlockSpec((tm, d), row), pl.BlockSpec((tm, TOPK), row),
                  pl.BlockSpec((tm, TOPK), row), pl.BlockSpec(memory_space=pl.ANY),
                  pl.BlockSpec((d, 2 * dh), fix), pl.BlockSpec((dh, d), fix),
                  pl.BlockSpec((1, d), fix), pl.BlockSpec((1, d), fix)],
        out_specs=[pl.BlockSpec((tm, d), row), pl.BlockSpec((tm, d), row)],
        scratch_shapes=[pltpu.VMEM((buf_rows, y_sorted.shape[1]), y_sorted.dtype), pltpu.SemaphoreType.DMA(())],
    )
    return pl.pallas_call(
        functools.partial(_combine_ln_kernel, alpha=alpha, dh=dh, n_experts=n_experts),
        grid_spec=grid_spec,
        out_shape=[jax.ShapeDtypeStruct((m, d), F32), jax.ShapeDtypeStruct((m, d), BF16)],
        compiler_params=_cparams("arbitrary"),
        name="combine_ln",
    )(run_start, n_chunks, lbase, tile_chunks, xb, x, loc, gate, y_sorted, w13, w2, g, b)


def _attn_kernel(q_ref, k0_ref, k1_ref, k2_ref, v0_ref, v1_ref, v2_ref, bias_ref, sink_ref, init_ref, o_ref,
                 ot_ref, *, n_kv):
    del init_ref
    hd = HEAD_DIM
    qt = (q_ref[...] * ATTN_SCALE).T.astype(BF16)
    kb = jnp.concatenate([k0_ref[...], k1_ref[...], k2_ref[...]], 0).astype(BF16)
    vt = jnp.concatenate([v0_ref[...], v1_ref[...], v2_ref[...]], 0).T.astype(BF16)
    bias = bias_ref[0]
    bias = jnp.concatenate([bias] * GROUP, axis=1)
    for h in range(n_kv):
        kh = kb[:, h * hd:(h + 1) * hd]
        qth = jnp.concatenate(
            [qt[(h * GROUP + g) * hd:(h * GROUP + g + 1) * hd, :] for g in range(GROUP)], axis=1)
        s = jnp.dot(kh, qth, preferred_element_type=F32) + bias
        sink = sink_ref[h]
        m = jnp.maximum(jnp.max(s, axis=0, keepdims=True), sink)
        p = jnp.exp(s - m)
        den = jnp.sum(p, axis=0, keepdims=True) + jnp.exp(sink - m)
        ot = jnp.dot(vt[h * hd:(h + 1) * hd, :], p.astype(BF16), preferred_element_type=F32)
        ot = ot * (1.0 / den)
        for g in range(GROUP):
            r0 = (h * GROUP + g) * hd
            ot_ref[r0:r0 + hd, :] = ot[:, g * QBLK:(g + 1) * QBLK]
    o_ref[...] = ot_ref[...].T.astype(BF16)


def _attention(q_arr, k_arr, n_steps, q_map, kv_maps, bias, bias_map, sink_rows, init, out_map,
               q_w, kv_w, k_col, v_col, name):
    n_kv = kv_w // HEAD_DIM
    kspecs = [pl.BlockSpec((QBLK, kv_w), (lambda i, f=f: (f(i), k_col))) for f in kv_maps]
    vspecs = [pl.BlockSpec((QBLK, kv_w), (lambda i, f=f: (f(i), v_col))) for f in kv_maps]
    nk = 3 * QBLK
    return pl.pallas_call(
        functools.partial(_attn_kernel, n_kv=n_kv),
        grid=(n_steps,),
        in_specs=[pl.BlockSpec((QBLK, q_w), lambda i: (q_map(i), 0))] + kspecs + vspecs + [
            pl.BlockSpec((1, nk, QBLK), lambda i: (bias_map(i), 0, 0)),
            pl.BlockSpec((n_kv, 1, GROUP * QBLK), lambda i: (0, 0, 0)),
            pl.BlockSpec(memory_space=pl.ANY)],
        out_specs=pl.BlockSpec((QBLK, q_w), lambda i: (out_map(i), 0)),
        out_shape=jax.ShapeDtypeStruct(init.shape, BF16),
        scratch_shapes=[pltpu.VMEM((q_w, QBLK), F32)],
        input_output_aliases={9: 0},
        compiler_params=_cparams("parallel"),
        name=name,
    )(q_arr, k_arr, k_arr, k_arr, k_arr, k_arr, k_arr, bias, sink_rows, init)


def _gelu_tanh(x):
    c = math.sqrt(2.0 / math.pi)
    return x * (0.5 * (1.0 + jnp.tanh(c * (x + 0.044715 * (x * x * x)))))


def _lru_kernel(xb_ref, gb_ref, cprev_ref, hprev_ref, cw_ref, cb_ref, gaw_ref, gab_ref, gxw_ref, gxb_ref,
                sp_ref, init_ref, y_ref, cnew_ref, hnew_ref, xcat, a_scr, b_scr, hcar, *, tt, valid_last):
    del init_ref
    j = pl.program_id(1)
    c_rnn = xb_ref.shape[1]
    bw = c_rnn // N_LRU_BLOCKS
    tail = SUBLANES

    @pl.when(j == 0)
    def _():
        xcat[0:tail, :] = jnp.zeros((tail, c_rnn), F32)
        xcat[tail - (CONV_W - 1):tail, :] = cprev_ref[...]
        hcar[...] = jnp.broadcast_to(hprev_ref[...], (SUBLANES, c_rnn))

    xcat[tail:tail + tt, :] = xb_ref[...]
    cw = cw_ref[...]
    xc = cb_ref[...]
    for tap in range(CONV_W):
        off = tail - (CONV_W - 1) + tap
        xc = xc + xcat[off:off + tt, :] * cw[tap:tap + 1, :]
    cnew_ref[...] = xcat[tail + valid_last - (CONV_W - 1):tail + valid_last, :]
    xcat[0:tail, :] = xcat[tt:tt + tail, :]

    xcb = xc.astype(BF16)
    rs, gs = [], []
    for n in range(N_LRU_BLOCKS):
        xs = xcb[:, n * bw:(n + 1) * bw]
        rs.append(jnp.dot(xs, gaw_ref[n], preferred_element_type=F32))
        gs.append(jnp.dot(xs, gxw_ref[n], preferred_element_type=F32))
    r = _sigmoid(jnp.concatenate(rs, axis=1) + gab_ref[...])
    gi = _sigmoid(jnp.concatenate(gs, axis=1) + gxb_ref[...])
    log_a = (-LRU_C * r) * sp_ref[...]
    a = jnp.exp(log_a)
    a_scr[...] = a
    b_scr[...] = jnp.sqrt(-jnp.tanh(log_a) * (a * a + 1.0)) * (gi * xc)

    row = lax.broadcasted_iota(jnp.int32, (SUBLANES, SCAN_LANES), 0)
    for c in range(c_rnn // SCAN_LANES):
        cs = slice(c * SCAN_LANES, (c + 1) * SCAN_LANES)

        def body(i, carry, cs=cs):
            r0 = pl.multiple_of(i * SUBLANES, SUBLANES)
            av = a_scr[pl.ds(r0, SUBLANES), cs]
            bv = b_scr[pl.ds(r0, SUBLANES), cs]
            for k in (1, 2, 4):
                a_sh = jnp.where(row >= k, pltpu.roll(av, k, 0), 1.0)
                b_sh = jnp.where(row >= k, pltpu.roll(bv, k, 0), 0.0)
                bv = av * b_sh + bv
                av = av * a_sh
            hv = av * carry + bv
            b_scr[pl.ds(r0, SUBLANES), cs] = hv
            return jnp.broadcast_to(hv[SUBLANES - 1:SUBLANES, :], (SUBLANES, SCAN_LANES))

        hcar[:, cs] = lax.fori_loop(0, tt // SUBLANES, body, hcar[:, cs])

    h = b_scr[...]
    hnew_ref[...] = b_scr[valid_last - 1:valid_last, :]
    y_ref[...] = (h * _gelu_tanh(gb_ref[...])).astype(BF16)


def _lru_seq(u, n_seq, n_tiles, tt, row0, seq_stride, valid_last, init, out_row0, cprev, hprev, wts):
    c_rnn = u.shape[1] // 2
    cw, cb, gaw, gab, gxw, gxb, sp = wts
    b0, bs, ob0 = row0 // tt, seq_stride // tt, out_row0 // tt
    bw = c_rnn // N_LRU_BLOCKS
    fix2 = lambda s, j: (0, 0)
    fix3 = lambda s, j: (0, 0, 0)
    per_seq = lambda s, j: (s, 0, 0)
    return pl.pallas_call(
        functools.partial(_lru_kernel, tt=tt, valid_last=valid_last),
        grid=(n_seq, n_tiles),
        in_specs=[pl.BlockSpec((tt, c_rnn), lambda s, j: (b0 + s * bs + j, 0)),
                  pl.BlockSpec((tt, c_rnn), lambda s, j: (b0 + s * bs + j, 1)),
                  pl.BlockSpec((None, CONV_W - 1, c_rnn), per_seq),
                  pl.BlockSpec((None, 1, c_rnn), per_seq),
                  pl.BlockSpec((CONV_W, c_rnn), fix2), pl.BlockSpec((1, c_rnn), fix2),
                  pl.BlockSpec((N_LRU_BLOCKS, bw, bw), fix3), pl.BlockSpec((1, c_rnn), fix2),
                  pl.BlockSpec((N_LRU_BLOCKS, bw, bw), fix3), pl.BlockSpec((1, c_rnn), fix2),
                  pl.BlockSpec((1, c_rnn), fix2), pl.BlockSpec(memory_space=pl.ANY)],
        out_specs=[pl.BlockSpec((tt, c_rnn), lambda s, j: (ob0 + s * bs + j, 0)),
                   pl.BlockSpec((None, CONV_W - 1, c_rnn), per_seq),
                   pl.BlockSpec((None, 1, c_rnn), per_seq)],
        out_shape=[jax.ShapeDtypeStruct(init.shape, BF16),
                   jax.ShapeDtypeStruct((n_seq, CONV_W - 1, c_rnn), F32),
                   jax.ShapeDtypeStruct((n_seq, 1, c_rnn), F32)],
        scratch_shapes=[pltpu.VMEM((SUBLANES + tt, c_rnn), F32), pltpu.VMEM((tt, c_rnn), F32),
                        pltpu.VMEM((tt, c_rnn), F32), pltpu.VMEM((SUBLANES, c_rnn), F32)],
        input_output_aliases={11: 0},
        compiler_params=_cparams("parallel", "arbitrary"),
        name="rglru_seq",
    )(u, u, cprev, hprev, cw, cb, gaw, gab, gxw, gxb, sp, init)


def _pack_halves(y):
    h = y.shape[1] // 2
    lo = lax.bitcast_convert_type(y[:, :h].astype(BF16).astype(F32), jnp.uint32)
    hi = lax.bitcast_convert_type(y[:, h:].astype(BF16).astype(F32), jnp.uint32)
    return (lo >> 16) | (hi & jnp.uint32(0xFFFF0000))


def _unpack_halves(w):
    lo = lax.bitcast_convert_type(w << 16, F32).astype(BF16)
    hi = lax.bitcast_convert_type(w & jnp.uint32(0xFFFF0000), F32).astype(BF16)
    return lo, hi


DISPATCH_ROWS = 512


def _dispatch_kernel(rs_ref, nc_ref, lb_ref, tot_ref, zs_ref, zc_ref, x_ref, loct_ref, o_hbm, obuf, zbuf, sem, zsem,
                     *, n_experts):
    i = pl.program_id(0)
    _, rows, hw = obuf.shape
    tm = x_ref.shape[0]
    slot = i % 2

    def chunk_copy(s, src_row, dst_row):
        return pltpu.make_async_copy(obuf.at[s, pl.ds(pl.multiple_of(src_row, RUN_ALIGN), RUN_ALIGN)],
                                     o_hbm.at[pl.ds(pl.multiple_of(dst_row, RUN_ALIGN), RUN_ALIGN)], sem.at[s])

    def drain(s, n):
        lax.fori_loop(0, n, lambda j, c: (chunk_copy(s, 0, 0).wait(), c)[1], 0)

    @pl.when(i >= 2)
    def _():
        drain(slot, tot_ref[jnp.maximum(i - 2, 0)])

    loct = loct_ref[...]
    x = x_ref[...]
    x_lo, x_hi = x[:, :hw], x[:, hw:]
    for r0 in range(0, rows, DISPATCH_ROWS):
        rowi = r0 + lax.broadcasted_iota(jnp.int32, (DISPATCH_ROWS, tm), 0)
        p = jnp.zeros((DISPATCH_ROWS, tm), F32)
        for k in range(TOPK):
            p = jnp.where(rowi == loct[k:k + 1, :], 1.0, p)
        pb = p.astype(BF16)
        lo = jnp.dot(pb, x_lo, preferred_element_type=F32)
        hi = jnp.dot(pb, x_hi, preferred_element_type=F32)
        obuf[slot, r0:r0 + DISPATCH_ROWS, :] = ((lax.bitcast_convert_type(lo, jnp.uint32) >> 16)
                                                | lax.bitcast_convert_type(hi, jnp.uint32))

    def per_expert(e, c):
        r = i * n_experts + e
        src, dst = lb_ref[r], rs_ref[r]

        def per_chunk(j, c2):
            chunk_copy(slot, src + j * RUN_ALIGN, dst + j * RUN_ALIGN).start()
            return c2

        return lax.fori_loop(0, nc_ref[r], per_chunk, c)

    lax.fori_loop(0, n_experts, per_expert, 0)

    @pl.when(i == pl.num_programs(0) - 1)
    def _():
        drain(slot, tot_ref[i])

        @pl.when(i >= 1)
        def _():
            drain(1 - slot, tot_ref[jnp.maximum(i - 1, 0)])

        zbuf[...] = jnp.zeros(zbuf.shape, zbuf.dtype)

        def zero_copy(dst_row):
            return pltpu.make_async_copy(zbuf, o_hbm.at[pl.ds(pl.multiple_of(dst_row, RUN_ALIGN), RUN_ALIGN)], zsem)

        def zero_expert(e, c):
            return lax.fori_loop(0, zc_ref[e], lambda j, c2: (zero_copy(zs_ref[e] + j * RUN_ALIGN).start(), c2)[1], c)

        def zero_wait(e, c):
            return lax.fori_loop(0, zc_ref[e], lambda j, c2: (zero_copy(0).wait(), c2)[1], c)

        lax.fori_loop(0, n_experts, zero_expert, 0)
        lax.fori_loop(0, n_experts, zero_wait, 0)


def _dispatch(xb, loc_t, run_start, n_chunks, lbase, tile_chunks, zero_start, zero_chunks, n_rows, n_experts):
    m, d = xb.shape
    tm = CMB_TILE
    buf_rows = -(-(tm * TOPK + n_experts * (RUN_ALIGN - 1)) // DISPATCH_ROWS) * DISPATCH_ROWS
    grid_spec = pltpu.PrefetchScalarGridSpec(
        num_scalar_prefetch=6,
        grid=(m // tm,),
        in_specs=[pl.BlockSpec((tm, d), lambda i, *_: (i, 0)),
                  pl.BlockSpec((None, TOPK, tm), lambda i, *_: (i, 0, 0))],
        out_specs=pl.BlockSpec(memory_space=pl.ANY),
        scratch_shapes=[pltpu.VMEM((2, buf_rows, d // 2), jnp.uint32), pltpu.VMEM((RUN_ALIGN, d // 2), jnp.uint32),
                        pltpu.SemaphoreType.DMA((2,)), pltpu.SemaphoreType.DMA(())],
    )
    return pl.pallas_call(
        functools.partial(_dispatch_kernel, n_experts=n_experts),
        grid_spec=grid_spec,
        out_shape=jax.ShapeDtypeStruct((n_rows, d // 2), jnp.uint32),
        compiler_params=_cparams("arbitrary"),
        name="moe_dispatch",
    )(run_start, n_chunks, lbase, tile_chunks, zero_start, zero_chunks, xb, loc_t)


def _moe_kernel(be_ref, nu_ref, x_ref, w1_ref, w3_ref, w2_ref, o_ref, w1b, w3b, w2b):
    i = pl.program_id(0)

    @pl.when(i < nu_ref[0])
    def _():
        prev = be_ref[jnp.maximum(i - 1, 0)]

        @pl.when(jnp.logical_or(i == 0, be_ref[i] != prev))
        def _():
            w1b[...] = w1_ref[...].astype(BF16)
            w3b[...] = w3_ref[...].astype(BF16)
            w2b[...] = w2_ref[...].astype(BF16)

        xlo, xhi = _unpack_halves(x_ref[...])
        hk = xlo.shape[1]
        a = (jnp.dot(xlo, w1b[0:hk, :], preferred_element_type=F32)
             + jnp.dot(xhi, w1b[hk:2 * hk, :], preferred_element_type=F32))
        b = (jnp.dot(xlo, w3b[0:hk, :], preferred_element_type=F32)
             + jnp.dot(xhi, w3b[hk:2 * hk, :], preferred_element_type=F32))
        h = (_silu(a) * b).astype(BF16)
        o_ref[...] = _pack_halves(jnp.dot(h, w2b[...], preferred_element_type=F32))


def _moe_experts(x_sorted, w1, w3, w2, layer, block_expert, n_used):
    n_rows, hw = x_sorted.shape
    d, de = w1.shape[2], w1.shape[3]
    nb = n_rows // MOE_BLK
    blk = lambda i, be, nu: (jnp.minimum(i, nu[0] - 1), 0)
    wmap = lambda i, be, nu: (layer, be[jnp.minimum(i, nu[0] - 1)], 0, 0)
    grid_spec = pltpu.PrefetchScalarGridSpec(
        num_scalar_prefetch=2,
        grid=(nb,),
        in_specs=[pl.BlockSpec((MOE_BLK, hw), blk),
                  pl.BlockSpec((None, None, d, de), wmap), pl.BlockSpec((None, None, d, de), wmap),
                  pl.BlockSpec((None, None, de, d), wmap)],
        out_specs=pl.BlockSpec((MOE_BLK, d // 2), blk),
        scratch_shapes=[pltpu.VMEM((d, de), BF16), pltpu.VMEM((d, de), BF16), pltpu.VMEM((de, d), BF16)],
    )
    return pl.pallas_call(
        _moe_kernel,
        grid_spec=grid_spec,
        out_shape=jax.ShapeDtypeStruct((n_rows, d // 2), jnp.uint32),
        compiler_params=_cparams("arbitrary"),
        name="moe_experts",
    )(block_expert, n_used, x_sorted, w1, w3, w2)


def _route(logits, router_bias, valid, n_experts):
    t = logits.shape[0]
    per = n_experts // N_GROUPS
    scores = jax.nn.sigmoid(logits)
    biased = scores + router_bias.astype(F32)
    grp = biased.reshape(t, N_GROUPS, per)
    m1 = jnp.max(grp, -1, keepdims=True)
    is_max = grp == m1
    first = is_max & (jnp.cumsum(is_max.astype(jnp.int32), -1) == 1)
    m2 = jnp.max(jnp.where(first, -jnp.inf, grp), -1)
    grp_score = m1[..., 0] + m2

    def rank(v):
        n = v.shape[-1]
        idx = jnp.arange(n)
        vi, vj = v[:, :, None], v[:, None, :]
        beaten = (vj > vi) | ((vj == vi) & (idx[None, None, :] < idx[None, :, None]))
        return jnp.sum(beaten.astype(jnp.int32), -1)

    gsel = rank(grp_score) < TOPK_GROUPS
    emask = jnp.repeat(gsel, per, axis=-1)
    e_rank = rank(jnp.where(emask, biased, NEG_INF))
    slot = e_rank[:, None, :] == jnp.arange(TOPK)[None, :, None]
    eidx = jnp.sum(jnp.where(slot, jnp.arange(n_experts)[None, None, :], 0), -1)
    g = jnp.sum(jnp.where(slot, scores[:, None, :], 0.0), -1)
    g = g / (jnp.sum(g, -1, keepdims=True) + 1e-20) * ROUTED_SCALE
    g = jnp.where(valid[:, None], g, 0.0)
    member = ((e_rank < TOPK) & valid[:, None]).astype(jnp.int32)
    return eidx.astype(jnp.int32), g, member


def _dispatch_plan(eidx, member, valid, n_experts, n_rows):
    t = eidx.shape[0]
    nt = t // CMB_TILE
    i32 = jnp.int32
    m3 = member.reshape(nt, CMB_TILE, n_experts)
    cnt = jnp.sum(m3, axis=1)
    rlen = (cnt + RUN_ALIGN - 1) // RUN_ALIGN * RUN_ALIGN
    tri = (jnp.arange(CMB_TILE)[:, None] > jnp.arange(CMB_TILE)[None, :]).astype(BF16)
    rank = jnp.einsum("ts,nse->nte", tri, m3.astype(BF16), preferred_element_type=F32)
    rank = rank.astype(i32).reshape(t, n_experts)
    seg = jnp.sum(rlen, axis=0)
    padded = (seg + MOE_BLK - 1) // MOE_BLK * MOE_BLK
    pad_end = jnp.cumsum(padded)
    pad_start = pad_end - padded
    run_off = jnp.cumsum(rlen, axis=0) - rlen
    run_start = pad_start[None, :] + run_off
    lbase = jnp.cumsum(rlen, axis=1) - rlen
    pos = (jnp.repeat(lbase, CMB_TILE, axis=0) + rank)
    hit = eidx[:, :, None] == jnp.arange(n_experts, dtype=i32)[None, None, :]
    loc = jnp.sum(jnp.where(hit, pos[:, None, :], 0), axis=-1)
    loc = jnp.where(valid[:, None], loc, -1).astype(i32)
    loc_t = loc.reshape(nt, CMB_TILE, TOPK).transpose(0, 2, 1)
    nb = n_rows // MOE_BLK
    blk_row = jnp.arange(nb, dtype=i32) * MOE_BLK
    block_expert = jnp.minimum(jnp.sum((pad_end[None, :] <= blk_row[:, None]).astype(i32), axis=1),
                               n_experts - 1).astype(i32)
    n_used = jnp.maximum(pad_end[-1] // MOE_BLK, 1).astype(i32).reshape(1)
    n_chunks = (rlen // RUN_ALIGN).astype(i32)
    return (loc, loc_t, run_start.reshape(-1).astype(i32), n_chunks.reshape(-1), lbase.reshape(-1).astype(i32),
            jnp.sum(n_chunks, axis=1).astype(i32), (pad_start + seg).astype(i32),
            ((padded - seg) // RUN_ALIGN).astype(i32), block_expert, n_used)


def _split_bf16(w):
    hi = w.astype(BF16)
    lo = (w - hi.astype(F32)).astype(BF16)
    return hi, lo


def kernel(x_prompt, x_sample, cache_meta_k, cache_meta_v, cache_win_k, cache_win_v, state_conv, state_h, meta_tokens, ln_g, ln_b, attn_w_in, attn_w_out, attn_sink, lru_w_in, lru_conv_w, lru_conv_b, lru_gate_a_w, lru_gate_a_b, lru_gate_x_w, lru_gate_x_b, lru_lambda, lru_w_out, moe_router_w, moe_router_bias, moe_w1, moe_w3, moe_w2, moe_shared_w1, moe_shared_w3, moe_shared_w2):
    bsz, seq, d = x_prompt.shape
    dec_b, dec_s, _ = x_sample.shape
    n_meta = meta_tokens.shape[0]
    depth = ln_g.shape[0]
    n_experts = moe_router_w.shape[2]
    past_len = PAST_LEN
    n_heads = d // HEAD_DIM
    n_kv = n_heads // GROUP
    q_w, kv_w = n_heads * HEAD_DIM, n_kv * HEAD_DIM
    c_rnn = lru_w_in.shape[2] // 2
    alpha = (2 * depth) ** 0.25

    t_real = bsz * seq
    s_base, s_rows = t_real, dec_b * dec_s
    m_base = s_base + s_rows
    t_pad = m_base + bsz * META_BLK
    assert seq % LRU_TILE == 0 and seq % QBLK == 0 and s_rows % QBLK == 0 and t_pad % ROW_TILE == 0
    assert n_meta <= META_BLK and dec_s <= QBLK and q_w == d and kv_w % LANES == 0
    assert cache_win_k.shape[2] == WINDOW and WINDOW == QBLK and n_meta + WINDOW + dec_s <= 3 * QBLK
    nqb = seq // QBLK

    pos = np.zeros((t_pad,), np.float32)
    pos[:t_real] = np.tile(np.arange(seq) + n_meta, bsz)
    pos[s_base:m_base] = np.tile(past_len + n_meta + np.arange(dec_s), dec_b)
    valid_np = np.zeros((t_pad,), bool)
    valid_np[:m_base] = True
    for b in range(bsz):
        pos[m_base + b * META_BLK:m_base + b * META_BLK + n_meta] = np.arange(n_meta)
        valid_np[m_base + b * META_BLK:m_base + b * META_BLK + n_meta] = True
    valid = jnp.asarray(valid_np)

    half = ROT_DIM // 2
    freqs = ROPE_THETA ** (-jnp.arange(0, ROT_DIM, 2, dtype=F32) / ROT_DIM)
    ang = jnp.asarray(pos)[:, None] * freqs[None, :]
    cos, sin = jnp.cos(ang), jnp.sin(ang)
    ones = jnp.ones((t_pad, HEAD_DIM - ROT_DIM), F32)
    zeros_h = jnp.zeros((t_pad, half), F32)
    zeros_r = jnp.zeros((t_pad, HEAD_DIM - ROT_DIM), F32)
    reps = LANES // HEAD_DIM
    cos_t = jnp.tile(jnp.concatenate([cos, cos, ones], 1), (1, reps))
    sin_lo_t = jnp.tile(jnp.concatenate([zeros_h, sin, zeros_r], 1), (1, reps))
    sin_hi_t = jnp.tile(jnp.concatenate([-sin, zeros_h, zeros_r], 1), (1, reps))

    kj = np.arange(3 * QBLK)[:, None]
    qc = np.arange(QBLK)[None, :] // CHUNK
    key_chunk = np.where(kj < 2 * QBLK, kj // CHUNK - 2, 0)
    band_ok = (kj < 2 * QBLK) & (qc - key_chunk >= 0) & (qc - key_chunk <= WINDOW // CHUNK)
    meta_ok = (kj >= 2 * QBLK) & (kj < 2 * QBLK + n_meta)
    later = band_ok | meta_ok
    first = (band_ok & (kj >= QBLK)) | meta_ok
    only_meta = np.broadcast_to(meta_ok, later.shape)
    bias_p = jnp.asarray(np.where(np.stack([first, later, only_meta]), 0.0, NEG_INF).astype(np.float32))
    n_keys_s = n_meta + WINDOW + dec_s
    bias_s = jnp.asarray(np.where(np.broadcast_to(kj < n_keys_s, later.shape), 0.0, NEG_INF)
                         .astype(np.float32))[None]

    meta_rows = jnp.concatenate([meta_tokens.astype(F32), jnp.zeros((META_BLK - n_meta, d), F32)], 0)
    x = jnp.concatenate([x_prompt.reshape(t_real, d), x_sample.reshape(s_rows, d),
                         jnp.tile(meta_rows, (bsz, 1))], 0)
    xb = x.astype(BF16)

    n_assign = (t_real + s_rows + bsz * n_meta) * TOPK
    n_runs = (t_pad // CMB_TILE) * n_experts
    n_rows = -(-(n_assign + n_runs * (RUN_ALIGN - 1) + n_experts * (MOE_BLK - 1)) // MOE_BLK) * MOE_BLK

    outs = {k: [] for k in ("mk", "mv", "wk", "wv", "ks", "vs", "cp", "hp", "cs", "hs")}
    for l in range(depth):
        idx = l // 2
        if l % 2 == 0:
            w_in = attn_w_in[idx].astype(BF16)
            qkv = _qkv_proj(xb, w_in, cos_t, sin_lo_t, sin_hi_t, q_w + kv_w)
            k_col, v_col = q_w // kv_w, q_w // kv_w + 1
            sink_rows = jnp.repeat(attn_sink[idx].astype(F32).reshape(n_kv, 1, GROUP), QBLK, axis=2)

            q_s = qkv[s_base:m_base, :q_w].reshape(dec_b, dec_s, q_w)
            q_s = jnp.pad(q_s, ((0, 0), (0, QBLK - dec_s), (0, 0))).reshape(dec_b * QBLK, q_w)
            k_new = qkv[s_base:m_base, q_w:q_w + kv_w].reshape(dec_b, dec_s, kv_w)
            v_new = qkv[s_base:m_base, q_w + kv_w:].reshape(dec_b, dec_s, kv_w)
            pad_k = jnp.zeros((dec_b, 3 * QBLK - n_keys_s, kv_w), F32)
            k_s = jnp.concatenate([cache_meta_k[idx].reshape(dec_b, n_meta, kv_w).astype(F32),
                                   cache_win_k[idx].reshape(dec_b, WINDOW, kv_w).astype(F32), k_new, pad_k], 1)
            v_s = jnp.concatenate([cache_meta_v[idx].reshape(dec_b, n_meta, kv_w).astype(F32),
                                   cache_win_v[idx].reshape(dec_b, WINDOW, kv_w).astype(F32), v_new, pad_k], 1)
            kv_s = jnp.concatenate([k_s, v_s], axis=2).reshape(dec_b * 3 * QBLK, 2 * kv_w)
            o_s = _attention(q_s, kv_s, dec_b, lambda i: i,
                             [lambda i: 3 * i, lambda i: 3 * i + 1, lambda i: 3 * i + 2], bias_s, lambda i: 0,
                             sink_rows, jnp.zeros((dec_b * QBLK, q_w), BF16), lambda i: i, q_w, kv_w, 0, 1,
                             "attn_sample")
            o_s = o_s.reshape(dec_b, QBLK, q_w)[:, :dec_s].reshape(s_rows, q_w)

            steps = nqb + 1
            is_meta = lambda i: (i % steps) == nqb
            bidx = lambda i: i // steps
            pblk = lambda i: i % steps
            frame_blk = lambda i: bidx(i) * nqb + jnp.minimum(pblk(i), nqb - 1)
            meta_blk = lambda i: m_base // QBLK + bidx(i)
            q_map = lambda i: jnp.where(is_meta(i), meta_blk(i), frame_blk(i))
            prev_map = lambda i: bidx(i) * nqb + jnp.clip(pblk(i) - 1, 0, nqb - 1)
            bias_map = lambda i: jnp.where(is_meta(i), 2, jnp.minimum(pblk(i), 1))
            o_init = jnp.concatenate([jnp.zeros((t_real, q_w), BF16), o_s,
                                      jnp.zeros((t_pad - m_base, q_w), BF16)], 0)
            o = _attention(qkv, qkv, bsz * steps, q_map, [prev_map, frame_blk, meta_blk], bias_p, bias_map,
                           sink_rows, o_init, q_map, q_w, kv_w, k_col, v_col, "attn_prompt")

            kp = qkv[:t_real, q_w:q_w + kv_w].reshape(bsz, seq, n_kv, HEAD_DIM)
            vp = qkv[:t_real, q_w + kv_w:].reshape(bsz, seq, n_kv, HEAD_DIM)
            km = qkv[m_base:, q_w:q_w + kv_w].reshape(bsz, META_BLK, n_kv, HEAD_DIM)[:, :n_meta]
            vm = qkv[m_base:, q_w + kv_w:].reshape(bsz, META_BLK, n_kv, HEAD_DIM)[:, :n_meta]
            outs["mk"].append(km); outs["mv"].append(vm)
            outs["wk"].append(kp[:, -WINDOW:]); outs["wv"].append(vp[:, -WINDOW:])
            outs["ks"].append(k_new.reshape(dec_b, dec_s, n_kv, HEAD_DIM))
            outs["vs"].append(v_new.reshape(dec_b, dec_s, n_kv, HEAD_DIM))
            w_out = attn_w_out[idx].astype(BF16)
        else:
            u = _matmul(xb, lru_w_in[idx].astype(BF16), 1024, "lru_in_proj")
            sp = jax.nn.softplus(-lru_lambda[idx].astype(F32)).reshape(1, c_rnn)
            wts = (lru_conv_w[idx].astype(F32), lru_conv_b[idx].astype(F32).reshape(1, c_rnn),
                   lru_gate_a_w[idx].astype(BF16), lru_gate_a_b[idx].astype(F32).reshape(1, c_rnn),
                   lru_gate_x_w[idx].astype(BF16), lru_gate_x_b[idx].astype(F32).reshape(1, c_rnn), sp)
            zc = jnp.zeros((bsz, CONV_W - 1, c_rnn), F32)
            zh = jnp.zeros((bsz, 1, c_rnn), F32)
            y_m, c_m, h_m = _lru_seq(u, bsz, 1, META_BLK, m_base, META_BLK, n_meta,
                                     jnp.zeros((bsz * META_BLK, c_rnn), BF16), 0, zc, zh, wts)
            y_s, c_s, h_s = _lru_seq(u, dec_b, 1, dec_s, s_base, dec_s, dec_s,
                                     jnp.zeros((s_rows, c_rnn), BF16), 0,
                                     state_conv[idx].astype(F32), state_h[idx].astype(F32).reshape(dec_b, 1, c_rnn),
                                     wts)
            o_init = jnp.concatenate([jnp.zeros((t_real, c_rnn), BF16), y_s, y_m], 0)
            o, c_p, h_p = _lru_seq(u, bsz, seq // LRU_TILE, LRU_TILE, 0, seq, LRU_TILE, o_init, 0,
                                   c_m, h_m, wts)
            outs["cp"].append(c_p); outs["hp"].append(h_p.reshape(bsz, c_rnn))
            outs["cs"].append(c_s); outs["hs"].append(h_s.reshape(dec_b, c_rnn))
            w_out = lru_w_out[idx].astype(BF16)

        ne_pad = -(-n_experts // LANES) * LANES
        wr = jnp.pad(moe_router_w[l].astype(F32), ((0, 0), (0, ne_pad - n_experts)))
        wr_hi, wr_lo = _split_bf16(wr)
        x1, x1b, logits = _proj_ln(o, w_out, x, ln_g[l, 0].astype(F32).reshape(1, d),
                                   ln_b[l, 0].astype(F32).reshape(1, d), wr_hi, wr_lo, alpha)

        eidx, gate, member = _route(logits[:, :n_experts], moe_router_bias[l], valid, n_experts)
        (loc, loc_t, run_start, n_chunks, lbase, tile_chunks, zero_start, zero_chunks, block_expert,
         n_used) = _dispatch_plan(eidx, member, valid, n_experts, n_rows)
        x_sorted = _dispatch(x1b, loc_t, run_start, n_chunks, lbase, tile_chunks, zero_start, zero_chunks,
                             n_rows, n_experts)
        y_sorted = _moe_experts(x_sorted, moe_w1, moe_w3, moe_w2, l, block_expert, n_used)

        w13 = jnp.concatenate([moe_shared_w1[l], moe_shared_w3[l]], axis=1).astype(BF16)
        x, xb = _combine_ln(x1b, x1, loc, gate, y_sorted, run_start, n_chunks, lbase, tile_chunks, w13,
                            moe_shared_w2[l].astype(BF16), ln_g[l, 1].astype(F32).reshape(1, d),
                            ln_b[l, 1].astype(F32).reshape(1, d), alpha, n_experts)

    y_prompt = x[:t_real].reshape(bsz, seq, d)
    y_sample = x[s_base:m_base].reshape(dec_b, dec_s, d)
    st = lambda k: jnp.stack(outs[k])
    return (y_prompt, y_sample, st("mk"), st("mv"), st("wk"), st("wv"), st("ks"), st("vs"),
            st("cp"), st("hp"), st("cs"), st("hs"))
```

```python
import functools
import math

import numpy as np
import jax
import jax.numpy as jnp
from jax import lax
from jax.experimental import pallas as pl
from jax.experimental.pallas import tpu as pltpu

F32 = jnp.float32
BF16 = jnp.bfloat16

PAST_LEN = 1024
CHUNK = 64
WINDOW = 128
HEAD_DIM = 64
GROUP = 8
ROT_DIM = HEAD_DIM // 4
ROPE_THETA = 500000.0
ATTN_SCALE = HEAD_DIM ** -0.5
N_LRU_BLOCKS = 8
CONV_W = 4
LRU_C = 8.0
N_GROUPS = 8
TOPK_GROUPS = 4
TOPK = 8
ROUTED_SCALE = 2.5
LN_EPS = 1e-5
NEG_INF = -1e30

LANES = 128
SUBLANES = 8
QBLK = 2 * CHUNK
META_BLK = 128
ROW_TILE = 512
LN_TILE = 256
LRU_TILE = 256
SCAN_LANES = 512
MOE_BLK = 512
CMB_TILE = 256
RUN_ALIGN = SUBLANES
VMEM_LIMIT = 56 * 1024 * 1024


def _cparams(*sem):
    return pltpu.CompilerParams(dimension_semantics=sem, vmem_limit_bytes=VMEM_LIMIT)


def _mm_kernel(x_ref, w_ref, o_ref):
    o_ref[...] = jnp.dot(x_ref[...], w_ref[...], preferred_element_type=F32)


def _matmul(x, w, tn, name):
    m, k = x.shape
    n = w.shape[1]
    tm = ROW_TILE
    return pl.pallas_call(
        _mm_kernel,
        grid=(n // tn, m // tm),
        in_specs=[pl.BlockSpec((tm, k), lambda j, i: (i, 0)),
                  pl.BlockSpec((k, tn), lambda j, i: (0, j))],
        out_specs=pl.BlockSpec((tm, tn), lambda j, i: (i, j)),
        out_shape=jax.ShapeDtypeStruct((m, n), F32),
        compiler_params=_cparams("parallel", "parallel"),
        name=name,
    )(x, w)


def _qkv_kernel(x_ref, w_ref, c_ref, s1_ref, s2_ref, o_ref, *, tn, rope_cols):
    j = pl.program_id(0)
    acc = jnp.dot(x_ref[...], w_ref[...], preferred_element_type=F32)
    c, s1, s2 = c_ref[...], s1_ref[...], s2_ref[...]
    lane = lax.broadcasted_iota(jnp.int32, (acc.shape[0], LANES), 1)
    for g in range(tn // LANES):
        a = acc[:, g * LANES:(g + 1) * LANES]
        lo = pltpu.roll(a, ROT_DIM // 2, 1)
        hi = pltpu.roll(a, LANES - ROT_DIM // 2, 1)
        roped = a * c + lo * s1 + hi * s2
        col = j * tn + g * LANES + lane
        o_ref[:, g * LANES:(g + 1) * LANES] = jnp.where(col < rope_cols, roped, a)


def _qkv_proj(xb, w, cos_t, sin_lo_t, sin_hi_t, rope_cols):
    m, k = xb.shape
    n = w.shape[1]
    tm = ROW_TILE
    tn = 512 if n % 512 == 0 else 256
    assert n % tn == 0 and m % tm == 0
    tab = pl.BlockSpec((tm, LANES), lambda j, i: (i, 0))
    return pl.pallas_call(
        functools.partial(_qkv_kernel, tn=tn, rope_cols=rope_cols),
        grid=(n // tn, m // tm),
        in_specs=[pl.BlockSpec((tm, k), lambda j, i: (i, 0)),
                  pl.BlockSpec((k, tn), lambda j, i: (0, j)),
                  tab, tab, tab],
        out_specs=pl.BlockSpec((tm, tn), lambda j, i: (i, j)),
        out_shape=jax.ShapeDtypeStruct((m, n), F32),
        compiler_params=_cparams("parallel", "parallel"),
        name="qkv_rope",
    )(xb, w, cos_t, sin_lo_t, sin_hi_t)


def _layer_norm_rows(z, g, b):
    mu = jnp.mean(z, -1, keepdims=True)
    d = z - mu
    var = jnp.mean(d * d, -1, keepdims=True)
    return d * lax.rsqrt(var + LN_EPS) * g + b


def _proj_ln_kernel(a_ref, w_ref, x_ref, g_ref, b_ref, wrh_ref, wrl_ref, o_ref, ob_ref, lg_ref, *, alpha):
    acc = jnp.dot(a_ref[...], w_ref[...], preferred_element_type=F32)
    y = _layer_norm_rows(alpha * x_ref[...] + acc, g_ref[...], b_ref[...])
    o_ref[...] = y
    yb = y.astype(BF16)
    ob_ref[...] = yb
    ylo = (y - yb.astype(F32)).astype(BF16)
    wrh = wrh_ref[...]
    lg_ref[...] = (jnp.dot(yb, wrh, preferred_element_type=F32)
                   + jnp.dot(ylo, wrh, preferred_element_type=F32)
                   + jnp.dot(yb, wrl_ref[...], preferred_element_type=F32))


def _proj_ln(a, w, x, g, b, wr_hi, wr_lo, alpha):
    m, k = a.shape
    d = w.shape[1]
    ne = wr_hi.shape[1]
    tm = LN_TILE
    row = lambda i: (i, 0)
    fix = lambda i: (0, 0)
    return pl.pallas_call(
        functools.partial(_proj_ln_kernel, alpha=alpha),
        grid=(m // tm,),
        in_specs=[pl.BlockSpec((tm, k), row), pl.BlockSpec((k, d), fix), pl.BlockSpec((tm, d), row),
                  pl.BlockSpec((1, d), fix), pl.BlockSpec((1, d), fix),
                  pl.BlockSpec((d, ne), fix), pl.BlockSpec((d, ne), fix)],
        out_specs=[pl.BlockSpec((tm, d), row), pl.BlockSpec((tm, d), row), pl.BlockSpec((tm, ne), row)],
        out_shape=[jax.ShapeDtypeStruct((m, d), F32), jax.ShapeDtypeStruct((m, d), BF16),
                   jax.ShapeDtypeStruct((m, ne), F32)],
        compiler_params=_cparams("parallel"),
        name="proj_ln_router",
    )(a, w, x, g, b, wr_hi, wr_lo)


def _sigmoid(x):
    return 0.5 * (jnp.tanh(0.5 * x) + 1.0)


def _silu(x):
    return x * _sigmoid(x)


COMBINE_ROWS = 512
WAIT_GROUPS = (32, 4, 1)
CHUNK_LIST = 512


def _wait_chunks(make_copy, n):
    for g in WAIT_GROUPS:
        cnt = n // g
        n = n - cnt * g
        lax.fori_loop(0, cnt, lambda j, c, g=g: (make_copy(g * RUN_ALIGN).wait(), c)[1], 0)


def _combine_ln_kernel(tot_ref, buf_rows_ref, hbm_rows_ref, xb_ref, x_ref, loc_ref, gate_ref, y_hbm, w13_ref, w2_ref,
                       g_ref, b_ref, o_ref, ob_ref, ybuf, sem, *, alpha, dh):
    i = pl.program_id(0)
    rows = ybuf.shape[0]

    ybuf[...] = jnp.zeros(ybuf.shape, ybuf.dtype)

    def fetch(c, carry):
        src = pl.multiple_of(hbm_rows_ref[c], RUN_ALIGN)
        dst = pl.multiple_of(buf_rows_ref[c], RUN_ALIGN)
        pltpu.make_async_copy(y_hbm.at[pl.ds(src, RUN_ALIGN)], ybuf.at[pl.ds(dst, RUN_ALIGN)], sem).start()
        return carry

    lax.fori_loop(0, tot_ref[i], fetch, 0)

    u = jnp.dot(xb_ref[...], w13_ref[...], preferred_element_type=F32)
    hs = (_silu(u[:, :dh]) * u[:, dh:]).astype(BF16)
    f = jnp.dot(hs, w2_ref[...], preferred_element_type=F32)

    _wait_chunks(lambda n: pltpu.make_async_copy(y_hbm.at[pl.ds(0, n)], ybuf.at[pl.ds(0, n)], sem), tot_ref[i])

    loc = loc_ref[...]
    gate = gate_ref[...]
    tm = loc.shape[0]
    r_lo = jnp.zeros((tm, ybuf.shape[1]), F32)
    r_hi = jnp.zeros((tm, ybuf.shape[1]), F32)
    for c0 in range(0, rows, COMBINE_ROWS):
        col = c0 + lax.broadcasted_iota(jnp.int32, (tm, COMBINE_ROWS), 1)
        sel = jnp.zeros((tm, COMBINE_ROWS), F32)
        for k in range(TOPK):
            sel = jnp.where(col == loc[:, k:k + 1], gate[:, k:k + 1], sel)
        sel = sel.astype(BF16)
        y_lo, y_hi = _unpack_halves(ybuf[c0:c0 + COMBINE_ROWS, :])
        r_lo = r_lo + jnp.dot(sel, y_lo, preferred_element_type=F32)
        r_hi = r_hi + jnp.dot(sel, y_hi, preferred_element_type=F32)
    f = f + jnp.concatenate([r_lo, r_hi], axis=1)
    y = _layer_norm_rows(alpha * x_ref[...] + f, g_ref[...], b_ref[...])
    o_ref[...] = y
    ob_ref[...] = y.astype(BF16)


def _combine_ln(xb, x, loc, gate, y_sorted, tile_chunks, chunk_buf, chunk_hbm, w13, w2, g, b, alpha, n_experts):
    m, d = x.shape
    dh = w2.shape[0]
    tm = CMB_TILE
    buf_rows = -(-(tm * TOPK + n_experts * (RUN_ALIGN - 1)) // 256) * 256
    row = lambda i, *_: (i, 0)
    fix = lambda i, *_: (0, 0)
    lst = pl.BlockSpec((CHUNK_LIST,), lambda i, *_: (i,), memory_space=pltpu.SMEM)
    grid_spec = pltpu.PrefetchScalarGridSpec(
        num_scalar_prefetch=1,
        grid=(m // tm,),
        in_specs=[lst, lst, pl.BlockSpec((tm, d), row), pl.BlockSpec((tm, d), row), pl.BlockSpec((tm, TOPK), row),
                  pl.BlockSpec((tm, TOPK), row), pl.BlockSpec(memory_space=pl.ANY),
                  pl.BlockSpec((d, 2 * dh), fix), pl.BlockSpec((dh, d), fix),
                  pl.BlockSpec((1, d), fix), pl.BlockSpec((1, d), fix)],
        out_specs=[pl.BlockSpec((tm, d), row), pl.BlockSpec((tm, d), row)],
        scratch_shapes=[pltpu.VMEM((buf_rows, y_sorted.shape[1]), y_sorted.dtype), pltpu.SemaphoreType.DMA(())],
    )
    return pl.pallas_call(
        functools.partial(_combine_ln_kernel, alpha=alpha, dh=dh),
        grid_spec=grid_spec,
        out_shape=[jax.ShapeDtypeStruct((m, d), F32), jax.ShapeDtypeStruct((m, d), BF16)],
        compiler_params=_cparams("arbitrary"),
        name="combine_ln",
    )(tile_chunks, chunk_buf, chunk_hbm, xb, x, loc, gate, y_sorted, w13, w2, g, b)


def _attn_kernel(q_ref, k0_ref, k1_ref, k2_ref, v0_ref, v1_ref, v2_ref, bias_ref, sink_ref, init_ref, o_ref,
                 ot_ref, *, n_kv):
    del init_ref
    hd = HEAD_DIM
    qt = (q_ref[...] * ATTN_SCALE).T.astype(BF16)
    kb = jnp.concatenate([k0_ref[...], k1_ref[...], k2_ref[...]], 0).astype(BF16)
    vt = jnp.concatenate([v0_ref[...], v1_ref[...], v2_ref[...]], 0).T.astype(BF16)
    bias = bias_ref[0]
    bias = jnp.concatenate([bias] * GROUP, axis=1)
    for h in range(n_kv):
        kh = kb[:, h * hd:(h + 1) * hd]
        qth = jnp.concatenate(
            [qt[(h * GROUP + g) * hd:(h * GROUP + g + 1) * hd, :] for g in range(GROUP)], axis=1)
        s = jnp.dot(kh, qth, preferred_element_type=F32) + bias
        sink = sink_ref[h]
        m = jnp.maximum(jnp.max(s, axis=0, keepdims=True), sink)
        p = jnp.exp(s - m)
        den = jnp.sum(p, axis=0, keepdims=True) + jnp.exp(sink - m)
        ot = jnp.dot(vt[h * hd:(h + 1) * hd, :], p.astype(BF16), preferred_element_type=F32)
        ot = ot * (1.0 / den)
        for g in range(GROUP):
            r0 = (h * GROUP + g) * hd
            ot_ref[r0:r0 + hd, :] = ot[:, g * QBLK:(g + 1) * QBLK]
    o_ref[...] = ot_ref[...].T.astype(BF16)


def _attention(q_arr, k_arr, n_steps, q_map, kv_maps, bias, bias_map, sink_rows, init, out_map,
               q_w, kv_w, k_col, v_col, name):
    n_kv = kv_w // HEAD_DIM
    kspecs = [pl.BlockSpec((QBLK, kv_w), (lambda i, f=f: (f(i), k_col))) for f in kv_maps]
    vspecs = [pl.BlockSpec((QBLK, kv_w), (lambda i, f=f: (f(i), v_col))) for f in kv_maps]
    nk = 3 * QBLK
    return pl.pallas_call(
        functools.partial(_attn_kernel, n_kv=n_kv),
        grid=(n_steps,),
        in_specs=[pl.BlockSpec((QBLK, q_w), lambda i: (q_map(i), 0))] + kspecs + vspecs + [
            pl.BlockSpec((1, nk, QBLK), lambda i: (bias_map(i), 0, 0)),
            pl.BlockSpec((n_kv, 1, GROUP * QBLK), lambda i: (0, 0, 0)),
            pl.BlockSpec(memory_space=pl.ANY)],
        out_specs=pl.BlockSpec((QBLK, q_w), lambda i: (out_map(i), 0)),
        out_shape=jax.ShapeDtypeStruct(init.shape, BF16),
        scratch_shapes=[pltpu.VMEM((q_w, QBLK), F32)],
        input_output_aliases={9: 0},
        compiler_params=_cparams("parallel"),
        name=name,
    )(q_arr, k_arr, k_arr, k_arr, k_arr, k_arr, k_arr, bias, sink_rows, init)


def _gelu_tanh(x):
    c = math.sqrt(2.0 / math.pi)
    return x * (0.5 * (1.0 + jnp.tanh(c * (x + 0.044715 * (x * x * x)))))


def _lru_kernel(xb_ref, gb_ref, cprev_ref, hprev_ref, cw_ref, cb_ref, gaw_ref, gab_ref, gxw_ref, gxb_ref,
                sp_ref, init_ref, y_ref, cnew_ref, hnew_ref, xcat, a_scr, b_scr, hcar, *, tt, valid_last):
    del init_ref
    j = pl.program_id(1)
    c_rnn = xb_ref.shape[1]
    bw = c_rnn // N_LRU_BLOCKS
    tail = SUBLANES

    @pl.when(j == 0)
    def _():
        xcat[0:tail, :] = jnp.zeros((tail, c_rnn), F32)
        xcat[tail - (CONV_W - 1):tail, :] = cprev_ref[...]
        hcar[...] = jnp.broadcast_to(hprev_ref[...], (SUBLANES, c_rnn))

    xcat[tail:tail + tt, :] = xb_ref[...]
    cw = cw_ref[...]
    xc = cb_ref[...]
    for tap in range(CONV_W):
        off = tail - (CONV_W - 1) + tap
        xc = xc + xcat[off:off + tt, :] * cw[tap:tap + 1, :]
    cnew_ref[...] = xcat[tail + valid_last - (CONV_W - 1):tail + valid_last, :]
    xcat[0:tail, :] = xcat[tt:tt + tail, :]

    xcb = xc.astype(BF16)
    rs, gs = [], []
    for n in range(N_LRU_BLOCKS):
        xs = xcb[:, n * bw:(n + 1) * bw]
        rs.append(jnp.dot(xs, gaw_ref[n], preferred_element_type=F32))
        gs.append(jnp.dot(xs, gxw_ref[n], preferred_element_type=F32))
    r = _sigmoid(jnp.concatenate(rs, axis=1) + gab_ref[...])
    gi = _sigmoid(jnp.concatenate(gs, axis=1) + gxb_ref[...])
    log_a = (-LRU_C * r) * sp_ref[...]
    a = jnp.exp(log_a)
    a_scr[...] = a
    b_scr[...] = jnp.sqrt(-jnp.tanh(log_a) * (a * a + 1.0)) * (gi * xc)

    row = lax.broadcasted_iota(jnp.int32, (SUBLANES, SCAN_LANES), 0)
    for c in range(c_rnn // SCAN_LANES):
        cs = slice(c * SCAN_LANES, (c + 1) * SCAN_LANES)

        def body(i, carry, cs=cs):
            r0 = pl.multiple_of(i * SUBLANES, SUBLANES)
            av = a_scr[pl.ds(r0, SUBLANES), cs]
            bv = b_scr[pl.ds(r0, SUBLANES), cs]
            for k in (1, 2, 4):
                a_sh = jnp.where(row >= k, pltpu.roll(av, k, 0), 1.0)
                b_sh = jnp.where(row >= k, pltpu.roll(bv, k, 0), 0.0)
                bv = av * b_sh + bv
                av = av * a_sh
            hv = av * carry + bv
            b_scr[pl.ds(r0, SUBLANES), cs] = hv
            return jnp.broadcast_to(hv[SUBLANES - 1:SUBLANES, :], (SUBLANES, SCAN_LANES))

        hcar[:, cs] = lax.fori_loop(0, tt // SUBLANES, body, hcar[:, cs])

    h = b_scr[...]
    hnew_ref[...] = b_scr[valid_last - 1:valid_last, :]
    y_ref[...] = (h * _gelu_tanh(gb_ref[...])).astype(BF16)


def _lru_seq(u, n_seq, n_tiles, tt, row0, seq_stride, valid_last, init, out_row0, cprev, hprev, wts):
    c_rnn = u.shape[1] // 2
    cw, cb, gaw, gab, gxw, gxb, sp = wts
    b0, bs, ob0 = row0 // tt, seq_stride // tt, out_row0 // tt
    bw = c_rnn // N_LRU_BLOCKS
    fix2 = lambda s, j: (0, 0)
    fix3 = lambda s, j: (0, 0, 0)
    per_seq = lambda s, j: (s, 0, 0)
    return pl.pallas_call(
        functools.partial(_lru_kernel, tt=tt, valid_last=valid_last),
        grid=(n_seq, n_tiles),
        in_specs=[pl.BlockSpec((tt, c_rnn), lambda s, j: (b0 + s * bs + j, 0)),
                  pl.BlockSpec((tt, c_rnn), lambda s, j: (b0 + s * bs + j, 1)),
                  pl.BlockSpec((None, CONV_W - 1, c_rnn), per_seq),
                  pl.BlockSpec((None, 1, c_rnn), per_seq),
                  pl.BlockSpec((CONV_W, c_rnn), fix2), pl.BlockSpec((1, c_rnn), fix2),
                  pl.BlockSpec((N_LRU_BLOCKS, bw, bw), fix3), pl.BlockSpec((1, c_rnn), fix2),
                  pl.BlockSpec((N_LRU_BLOCKS, bw, bw), fix3), pl.BlockSpec((1, c_rnn), fix2),
                  pl.BlockSpec((1, c_rnn), fix2), pl.BlockSpec(memory_space=pl.ANY)],
        out_specs=[pl.BlockSpec((tt, c_rnn), lambda s, j: (ob0 + s * bs + j, 0)),
                   pl.BlockSpec((None, CONV_W - 1, c_rnn), per_seq),
                   pl.BlockSpec((None, 1, c_rnn), per_seq)],
        out_shape=[jax.ShapeDtypeStruct(init.shape, BF16),
                   jax.ShapeDtypeStruct((n_seq, CONV_W - 1, c_rnn), F32),
                   jax.ShapeDtypeStruct((n_seq, 1, c_rnn), F32)],
        scratch_shapes=[pltpu.VMEM((SUBLANES + tt, c_rnn), F32), pltpu.VMEM((tt, c_rnn), F32),
                        pltpu.VMEM((tt, c_rnn), F32), pltpu.VMEM((SUBLANES, c_rnn), F32)],
        input_output_aliases={11: 0},
        compiler_params=_cparams("parallel", "arbitrary"),
        name="rglru_seq",
    )(u, u, cprev, hprev, cw, cb, gaw, gab, gxw, gxb, sp, init)


def _pack_halves(y):
    h = y.shape[1] // 2
    lo = lax.bitcast_convert_type(y[:, :h].astype(BF16).astype(F32), jnp.uint32)
    hi = lax.bitcast_convert_type(y[:, h:].astype(BF16).astype(F32), jnp.uint32)
    return (lo >> 16) | (hi & jnp.uint32(0xFFFF0000))


def _unpack_halves(w):
    lo = lax.bitcast_convert_type(w << 16, F32).astype(BF16)
    hi = lax.bitcast_convert_type(w & jnp.uint32(0xFFFF0000), F32).astype(BF16)
    return lo, hi


DISPATCH_ROWS = 512


def _dispatch_kernel(tot_ref, zs_ref, zc_ref, buf_rows_ref, hbm_rows_ref, x_ref, loct_ref, o_hbm, obuf, zbuf, sem,
                     zsem, *, n_experts):
    i = pl.program_id(0)
    _, rows, hw = obuf.shape
    tm = x_ref.shape[0]
    slot = i % 2

    def drain(s, n):
        _wait_chunks(lambda m: pltpu.make_async_copy(obuf.at[s, pl.ds(0, m)], o_hbm.at[pl.ds(0, m)], sem.at[s]), n)

    @pl.when(i >= 2)
    def _():
        drain(slot, tot_ref[jnp.maximum(i - 2, 0)])

    loct = loct_ref[...]
    x = x_ref[...]
    x_lo, x_hi = x[:, :hw], x[:, hw:]
    for r0 in range(0, rows, DISPATCH_ROWS):
        rowi = r0 + lax.broadcasted_iota(jnp.int32, (DISPATCH_ROWS, tm), 0)
        p = jnp.zeros((DISPATCH_ROWS, tm), F32)
        for k in range(TOPK):
            p = jnp.where(rowi == loct[k:k + 1, :], 1.0, p)
        pb = p.astype(BF16)
        lo = jnp.dot(pb, x_lo, preferred_element_type=F32)
        hi = jnp.dot(pb, x_hi, preferred_element_type=F32)
        obuf[slot, r0:r0 + DISPATCH_ROWS, :] = ((lax.bitcast_convert_type(lo, jnp.uint32) >> 16)
                                                | lax.bitcast_convert_type(hi, jnp.uint32))

    def send(c, carry):
        src = pl.multiple_of(buf_rows_ref[c], RUN_ALIGN)
        dst = pl.multiple_of(hbm_rows_ref[c], RUN_ALIGN)
        pltpu.make_async_copy(obuf.at[slot, pl.ds(src, RUN_ALIGN)], o_hbm.at[pl.ds(dst, RUN_ALIGN)],
                              sem.at[slot]).start()
        return carry

    lax.fori_loop(0, tot_ref[i], send, 0)

    @pl.when(i == pl.num_programs(0) - 1)
    def _():
        drain(slot, tot_ref[i])

        @pl.when(i >= 1)
        def _():
            drain(1 - slot, tot_ref[jnp.maximum(i - 1, 0)])

        zbuf[...] = jnp.zeros(zbuf.shape, zbuf.dtype)

        def zero_copy(dst_row):
            return pltpu.make_async_copy(zbuf, o_hbm.at[pl.ds(pl.multiple_of(dst_row, RUN_ALIGN), RUN_ALIGN)], zsem)

        def zero_expert(e, c):
            return lax.fori_loop(0, zc_ref[e], lambda j, c2: (zero_copy(zs_ref[e] + j * RUN_ALIGN).start(), c2)[1], c)

        def zero_wait(e, c):
            return lax.fori_loop(0, zc_ref[e], lambda j, c2: (zero_copy(0).wait(), c2)[1], c)

        lax.fori_loop(0, n_experts, zero_expert, 0)
        lax.fori_loop(0, n_experts, zero_wait, 0)


def _dispatch(xb, loc_t, tile_chunks, chunk_buf, chunk_hbm, zero_start, zero_chunks, n_rows, n_experts):
    m, d = xb.shape
    tm = CMB_TILE
    buf_rows = -(-(tm * TOPK + n_experts * (RUN_ALIGN - 1)) // DISPATCH_ROWS) * DISPATCH_ROWS
    lst = pl.BlockSpec((CHUNK_LIST,), lambda i, *_: (i,), memory_space=pltpu.SMEM)
    grid_spec = pltpu.PrefetchScalarGridSpec(
        num_scalar_prefetch=3,
        grid=(m // tm,),
        in_specs=[lst, lst, pl.BlockSpec((tm, d), lambda i, *_: (i, 0)),
                  pl.BlockSpec((None, TOPK, tm), lambda i, *_: (i, 0, 0))],
        out_specs=pl.BlockSpec(memory_space=pl.ANY),
        scratch_shapes=[pltpu.VMEM((2, buf_rows, d // 2), jnp.uint32), pltpu.VMEM((RUN_ALIGN, d // 2), jnp.uint32),
                        pltpu.SemaphoreType.DMA((2,)), pltpu.SemaphoreType.DMA(())],
    )
    return pl.pallas_call(
        functools.partial(_dispatch_kernel, n_experts=n_experts),
        grid_spec=grid_spec,
        out_shape=jax.ShapeDtypeStruct((n_rows, d // 2), jnp.uint32),
        compiler_params=_cparams("arbitrary"),
        name="moe_dispatch",
    )(tile_chunks, zero_start, zero_chunks, chunk_buf, chunk_hbm, xb, loc_t)


def _moe_kernel(be_ref, nu_ref, x_ref, w1_ref, w3_ref, w2_ref, o_ref, w1b, w3b, w2b):
    i = pl.program_id(0)

    @pl.when(i < nu_ref[0])
    def _():
        prev = be_ref[jnp.maximum(i - 1, 0)]

        @pl.when(jnp.logical_or(i == 0, be_ref[i] != prev))
        def _():
            w1b[...] = w1_ref[...].astype(BF16)
            w3b[...] = w3_ref[...].astype(BF16)
            w2b[...] = w2_ref[...].astype(BF16)

        xlo, xhi = _unpack_halves(x_ref[...])
        hk = xlo.shape[1]
        a = (jnp.dot(xlo, w1b[0:hk, :], preferred_element_type=F32)
             + jnp.dot(xhi, w1b[hk:2 * hk, :], preferred_element_type=F32))
        b = (jnp.dot(xlo, w3b[0:hk, :], preferred_element_type=F32)
             + jnp.dot(xhi, w3b[hk:2 * hk, :], preferred_element_type=F32))
        h = (_silu(a) * b).astype(BF16)
        o_ref[...] = _pack_halves(jnp.dot(h, w2b[...], preferred_element_type=F32))


def _moe_experts(x_sorted, w1, w3, w2, layer, block_expert, n_used):
    n_rows, hw = x_sorted.shape
    d, de = w1.shape[2], w1.shape[3]
    nb = n_rows // MOE_BLK
    blk = lambda i, be, nu: (jnp.minimum(i, nu[0] - 1), 0)
    wmap = lambda i, be, nu: (layer, be[jnp.minimum(i, nu[0] - 1)], 0, 0)
    grid_spec = pltpu.PrefetchScalarGridSpec(
        num_scalar_prefetch=2,
        grid=(nb,),
        in_specs=[pl.BlockSpec((MOE_BLK, hw), blk),
                  pl.BlockSpec((None, None, d, de), wmap), pl.BlockSpec((None, None, d, de), wmap),
                  pl.BlockSpec((None, None, de, d), wmap)],
        out_specs=pl.BlockSpec((MOE_BLK, d // 2), blk),
        scratch_shapes=[pltpu.VMEM((d, de), BF16), pltpu.VMEM((d, de), BF16), pltpu.VMEM((de, d), BF16)],
    )
    return pl.pallas_call(
        _moe_kernel,
        grid_spec=grid_spec,
        out_shape=jax.ShapeDtypeStruct((n_rows, d // 2), jnp.uint32),
        compiler_params=_cparams("arbitrary"),
        name="moe_experts",
    )(block_expert, n_used, x_sorted, w1, w3, w2)


def _route(logits, router_bias, valid, n_experts):
    t = logits.shape[0]
    per = n_experts // N_GROUPS
    scores = jax.nn.sigmoid(logits)
    biased = scores + router_bias.astype(F32)
    grp = biased.reshape(t, N_GROUPS, per)
    m1 = jnp.max(grp, -1, keepdims=True)
    is_max = grp == m1
    first = is_max & (jnp.cumsum(is_max.astype(jnp.int32), -1) == 1)
    m2 = jnp.max(jnp.where(first, -jnp.inf, grp), -1)
    grp_score = m1[..., 0] + m2

    def rank(v):
        n = v.shape[-1]
        idx = jnp.arange(n)
        vi, vj = v[:, :, None], v[:, None, :]
        beaten = (vj > vi) | ((vj == vi) & (idx[None, None, :] < idx[None, :, None]))
        return jnp.sum(beaten.astype(jnp.int32), -1)

    gsel = rank(grp_score) < TOPK_GROUPS
    emask = jnp.repeat(gsel, per, axis=-1)
    e_rank = rank(jnp.where(emask, biased, NEG_INF))
    slot = e_rank[:, None, :] == jnp.arange(TOPK)[None, :, None]
    eidx = jnp.sum(jnp.where(slot, jnp.arange(n_experts)[None, None, :], 0), -1)
    g = jnp.sum(jnp.where(slot, scores[:, None, :], 0.0), -1)
    g = g / (jnp.sum(g, -1, keepdims=True) + 1e-20) * ROUTED_SCALE
    g = jnp.where(valid[:, None], g, 0.0)
    member = ((e_rank < TOPK) & valid[:, None]).astype(jnp.int32)
    return eidx.astype(jnp.int32), g, member


def _dispatch_plan(eidx, member, valid, n_experts, n_rows):
    t = eidx.shape[0]
    nt = t // CMB_TILE
    i32 = jnp.int32
    m3 = member.reshape(nt, CMB_TILE, n_experts)
    cnt = jnp.sum(m3, axis=1)
    rlen = (cnt + RUN_ALIGN - 1) // RUN_ALIGN * RUN_ALIGN
    tri = (jnp.arange(CMB_TILE)[:, None] > jnp.arange(CMB_TILE)[None, :]).astype(BF16)
    rank = jnp.einsum("ts,nse->nte", tri, m3.astype(BF16), preferred_element_type=F32)
    rank = rank.astype(i32).reshape(t, n_experts)
    seg = jnp.sum(rlen, axis=0)
    padded = (seg + MOE_BLK - 1) // MOE_BLK * MOE_BLK
    pad_end = jnp.cumsum(padded)
    pad_start = pad_end - padded
    run_off = jnp.cumsum(rlen, axis=0) - rlen
    run_start = pad_start[None, :] + run_off
    lbase = jnp.cumsum(rlen, axis=1) - rlen
    pos = (jnp.repeat(lbase, CMB_TILE, axis=0) + rank)
    hit = eidx[:, :, None] == jnp.arange(n_experts, dtype=i32)[None, None, :]
    loc = jnp.sum(jnp.where(hit, pos[:, None, :], 0), axis=-1)
    loc = jnp.where(valid[:, None], loc, -1).astype(i32)
    loc_t = loc.reshape(nt, CMB_TILE, TOPK).transpose(0, 2, 1)
    nb = n_rows // MOE_BLK
    blk_row = jnp.arange(nb, dtype=i32) * MOE_BLK
    block_expert = jnp.minimum(jnp.sum((pad_end[None, :] <= blk_row[:, None]).astype(i32), axis=1),
                               n_experts - 1).astype(i32)
    n_used = jnp.maximum(pad_end[-1] // MOE_BLK, 1).astype(i32).reshape(1)

    n_chunks = (rlen // RUN_ALIGN).astype(i32)
    cum = jnp.cumsum(n_chunks, axis=1)
    c = jnp.arange(CHUNK_LIST, dtype=i32)
    e_c = jnp.minimum(jnp.sum((cum[:, None, :] <= c[None, :, None]).astype(i32), axis=-1), n_experts - 1)
    own = e_c[:, :, None] == jnp.arange(n_experts, dtype=i32)[None, None, :]
    pick = lambda tab: jnp.sum(jnp.where(own, tab[:, None, :], 0), axis=-1)
    off = (c[None, :] - pick(cum - n_chunks)) * RUN_ALIGN
    chunk_buf = (pick(lbase) + off).reshape(-1).astype(i32)
    chunk_hbm = (pick(run_start) + off).reshape(-1).astype(i32)
    return (loc, loc_t, cum[:, -1].astype(i32), chunk_buf, chunk_hbm, (pad_start + seg).astype(i32),
            ((padded - seg) // RUN_ALIGN).astype(i32), block_expert, n_used)


def _split_bf16(w):
    hi = w.astype(BF16)
    lo = (w - hi.astype(F32)).astype(BF16)
    return hi, lo


def kernel(x_prompt, x_sample, cache_meta_k, cache_meta_v, cache_win_k, cache_win_v, state_conv, state_h, meta_tokens, ln_g, ln_b, attn_w_in, attn_w_out, attn_sink, lru_w_in, lru_conv_w, lru_conv_b, lru_gate_a_w, lru_gate_a_b, lru_gate_x_w, lru_gate_x_b, lru_lambda, lru_w_out, moe_router_w, moe_router_bias, moe_w1, moe_w3, moe_w2, moe_shared_w1, moe_shared_w3, moe_shared_w2):
    bsz, seq, d = x_prompt.shape
    dec_b, dec_s, _ = x_sample.shape
    n_meta = meta_tokens.shape[0]
    depth = ln_g.shape[0]
    n_experts = moe_router_w.shape[2]
    past_len = PAST_LEN
    n_heads = d // HEAD_DIM
    n_kv = n_heads // GROUP
    q_w, kv_w = n_heads * HEAD_DIM, n_kv * HEAD_DIM
    c_rnn = lru_w_in.shape[2] // 2
    alpha = (2 * depth) ** 0.25

    t_real = bsz * seq
    s_base, s_rows = t_real, dec_b * dec_s
    m_base = s_base + s_rows
    t_pad = m_base + bsz * META_BLK
    assert seq % LRU_TILE == 0 and seq % QBLK == 0 and s_rows % QBLK == 0 and t_pad % ROW_TILE == 0
    assert n_meta <= META_BLK and dec_s <= QBLK and q_w == d and kv_w % LANES == 0
    assert (CMB_TILE * TOPK + n_experts * (RUN_ALIGN - 1)) // RUN_ALIGN <= CHUNK_LIST and t_pad % CMB_TILE == 0
    assert cache_win_k.shape[2] == WINDOW and WINDOW == QBLK and n_meta + WINDOW + dec_s <= 3 * QBLK
    nqb = seq // QBLK

    pos = np.zeros((t_pad,), np.float32)
    pos[:t_real] = np.tile(np.arange(seq) + n_meta, bsz)
    pos[s_base:m_base] = np.tile(past_len + n_meta + np.arange(dec_s), dec_b)
    valid_np = np.zeros((t_pad,), bool)
    valid_np[:m_base] = True
    for b in range(bsz):
        pos[m_base + b * META_BLK:m_base + b * META_BLK + n_meta] = np.arange(n_meta)
        valid_np[m_base + b * META_BLK:m_base + b * META_BLK + n_meta] = True
    valid = jnp.asarray(valid_np)

    half = ROT_DIM // 2
    freqs = ROPE_THETA ** (-jnp.arange(0, ROT_DIM, 2, dtype=F32) / ROT_DIM)
    ang = jnp.asarray(pos)[:, None] * freqs[None, :]
    cos, sin = jnp.cos(ang), jnp.sin(ang)
    ones = jnp.ones((t_pad, HEAD_DIM - ROT_DIM), F32)
    zeros_h = jnp.zeros((t_pad, half), F32)
    zeros_r = jnp.zeros((t_pad, HEAD_DIM - ROT_DIM), F32)
    reps = LANES // HEAD_DIM
    cos_t = jnp.tile(jnp.concatenate([cos, cos, ones], 1), (1, reps))
    sin_lo_t = jnp.tile(jnp.concatenate([zeros_h, sin, zeros_r], 1), (1, reps))
    sin_hi_t = jnp.tile(jnp.concatenate([-sin, zeros_h, zeros_r], 1), (1, reps))

    kj = np.arange(3 * QBLK)[:, None]
    qc = np.arange(QBLK)[None, :] // CHUNK
    key_chunk = np.where(kj < 2 * QBLK, kj // CHUNK - 2, 0)
    band_ok = (kj < 2 * QBLK) & (qc - key_chunk >= 0) & (qc - key_chunk <= WINDOW // CHUNK)
    meta_ok = (kj >= 2 * QBLK) & (kj < 2 * QBLK + n_meta)
    later = band_ok | meta_ok
    first = (band_ok & (kj >= QBLK)) | meta_ok
    only_meta = np.broadcast_to(meta_ok, later.shape)
    bias_p = jnp.asarray(np.where(np.stack([first, later, only_meta]), 0.0, NEG_INF).astype(np.float32))
    n_keys_s = n_meta + WINDOW + dec_s
    bias_s = jnp.asarray(np.where(np.broadcast_to(kj < n_keys_s, later.shape), 0.0, NEG_INF)
                         .astype(np.float32))[None]

    meta_rows = jnp.concatenate([meta_tokens.astype(F32), jnp.zeros((META_BLK - n_meta, d), F32)], 0)
    x = jnp.concatenate([x_prompt.reshape(t_real, d), x_sample.reshape(s_rows, d),
                         jnp.tile(meta_rows, (bsz, 1))], 0)
    xb = x.astype(BF16)

    n_assign = (t_real + s_rows + bsz * n_meta) * TOPK
    n_runs = (t_pad // CMB_TILE) * n_experts
    n_rows = -(-(n_assign + n_runs * (RUN_ALIGN - 1) + n_experts * (MOE_BLK - 1)) // MOE_BLK) * MOE_BLK

    outs = {k: [] for k in ("mk", "mv", "wk", "wv", "ks", "vs", "cp", "hp", "cs", "hs")}
    for l in range(depth):
        idx = l // 2
        if l % 2 == 0:
            w_in = attn_w_in[idx].astype(BF16)
            qkv = _qkv_proj(xb, w_in, cos_t, sin_lo_t, sin_hi_t, q_w + kv_w)
            k_col, v_col = q_w // kv_w, q_w // kv_w + 1
            sink_rows = jnp.repeat(attn_sink[idx].astype(F32).reshape(n_kv, 1, GROUP), QBLK, axis=2)

            q_s = qkv[s_base:m_base, :q_w].reshape(dec_b, dec_s, q_w)
            q_s = jnp.pad(q_s, ((0, 0), (0, QBLK - dec_s), (0, 0))).reshape(dec_b * QBLK, q_w)
            k_new = qkv[s_base:m_base, q_w:q_w + kv_w].reshape(dec_b, dec_s, kv_w)
            v_new = qkv[s_base:m_base, q_w + kv_w:].reshape(dec_b, dec_s, kv_w)
            pad_k = jnp.zeros((dec_b, 3 * QBLK - n_keys_s, kv_w), F32)
            k_s = jnp.concatenate([cache_meta_k[idx].reshape(dec_b, n_meta, kv_w).astype(F32),
                                   cache_win_k[idx].reshape(dec_b, WINDOW, kv_w).astype(F32), k_new, pad_k], 1)
            v_s = jnp.concatenate([cache_meta_v[idx].reshape(dec_b, n_meta, kv_w).astype(F32),
                                   cache_win_v[idx].reshape(dec_b, WINDOW, kv_w).astype(F32), v_new, pad_k], 1)
            kv_s = jnp.concatenate([k_s, v_s], axis=2).reshape(dec_b * 3 * QBLK, 2 * kv_w)
            o_s = _attention(q_s, kv_s, dec_b, lambda i: i,
                             [lambda i: 3 * i, lambda i: 3 * i + 1, lambda i: 3 * i + 2], bias_s, lambda i: 0,
                             sink_rows, jnp.zeros((dec_b * QBLK, q_w), BF16), lambda i: i, q_w, kv_w, 0, 1,
                             "attn_sample")
            o_s = o_s.reshape(dec_b, QBLK, q_w)[:, :dec_s].reshape(s_rows, q_w)

            steps = nqb + 1
            is_meta = lambda i: (i % steps) == nqb
            bidx = lambda i: i // steps
            pblk = lambda i: i % steps
            frame_blk = lambda i: bidx(i) * nqb + jnp.minimum(pblk(i), nqb - 1)
            meta_blk = lambda i: m_base // QBLK + bidx(i)
            q_map = lambda i: jnp.where(is_meta(i), meta_blk(i), frame_blk(i))
            prev_map = lambda i: bidx(i) * nqb + jnp.clip(pblk(i) - 1, 0, nqb - 1)
            bias_map = lambda i: jnp.where(is_meta(i), 2, jnp.minimum(pblk(i), 1))
            o_init = jnp.concatenate([jnp.zeros((t_real, q_w), BF16), o_s,
                                      jnp.zeros((t_pad - m_base, q_w), BF16)], 0)
            o = _attention(qkv, qkv, bsz * steps, q_map, [prev_map, frame_blk, meta_blk], bias_p, bias_map,
                           sink_rows, o_init, q_map, q_w, kv_w, k_col, v_col, "attn_prompt")

            kp = qkv[:t_real, q_w:q_w + kv_w].reshape(bsz, seq, n_kv, HEAD_DIM)
            vp = qkv[:t_real, q_w + kv_w:].reshape(bsz, seq, n_kv, HEAD_DIM)
            km = qkv[m_base:, q_w:q_w + kv_w].reshape(bsz, META_BLK, n_kv, HEAD_DIM)[:, :n_meta]
            vm = qkv[m_base:, q_w + kv_w:].reshape(bsz, META_BLK, n_kv, HEAD_DIM)[:, :n_meta]
            outs["mk"].append(km); outs["mv"].append(vm)
            outs["wk"].append(kp[:, -WINDOW:]); outs["wv"].append(vp[:, -WINDOW:])
            outs["ks"].append(k_new.reshape(dec_b, dec_s, n_kv, HEAD_DIM))
            outs["vs"].append(v_new.reshape(dec_b, dec_s, n_kv, HEAD_DIM))
            w_out = attn_w_out[idx].astype(BF16)
        else:
            u = _matmul(xb, lru_w_in[idx].astype(BF16), 1024, "lru_in_proj")
            sp = jax.nn.softplus(-lru_lambda[idx].astype(F32)).reshape(1, c_rnn)
            wts = (lru_conv_w[idx].astype(F32), lru_conv_b[idx].astype(F32).reshape(1, c_rnn),
                   lru_gate_a_w[idx].astype(BF16), lru_gate_a_b[idx].astype(F32).reshape(1, c_rnn),
                   lru_gate_x_w[idx].astype(BF16), lru_gate_x_b[idx].astype(F32).reshape(1, c_rnn), sp)
            zc = jnp.zeros((bsz, CONV_W - 1, c_rnn), F32)
            zh = jnp.zeros((bsz, 1, c_rnn), F32)
            y_m, c_m, h_m = _lru_seq(u, bsz, 1, META_BLK, m_base, META_BLK, n_meta,
                                     jnp.zeros((bsz * META_BLK, c_rnn), BF16), 0, zc, zh, wts)
            y_s, c_s, h_s = _lru_seq(u, dec_b, 1, dec_s, s_base, dec_s, dec_s,
                                     jnp.zeros((s_rows, c_rnn), BF16), 0,
                                     state_conv[idx].astype(F32), state_h[idx].astype(F32).reshape(dec_b, 1, c_rnn),
                                     wts)
            o_init = jnp.concatenate([jnp.zeros((t_real, c_rnn), BF16), y_s, y_m], 0)
            o, c_p, h_p = _lru_seq(u, bsz, seq // LRU_TILE, LRU_TILE, 0, seq, LRU_TILE, o_init, 0,
                                   c_m, h_m, wts)
            outs["cp"].append(c_p); outs["hp"].append(h_p.reshape(bsz, c_rnn))
            outs["cs"].append(c_s); outs["hs"].append(h_s.reshape(dec_b, c_rnn))
            w_out = lru_w_out[idx].astype(BF16)

        ne_pad = -(-n_experts // LANES) * LANES
        wr = jnp.pad(moe_router_w[l].astype(F32), ((0, 0), (0, ne_pad - n_experts)))
        wr_hi, wr_lo = _split_bf16(wr)
        x1, x1b, logits = _proj_ln(o, w_out, x, ln_g[l, 0].astype(F32).reshape(1, d),
                                   ln_b[l, 0].astype(F32).reshape(1, d), wr_hi, wr_lo, alpha)

        eidx, gate, member = _route(logits[:, :n_experts], moe_router_bias[l], valid, n_experts)
        (loc, loc_t, tile_chunks, chunk_buf, chunk_hbm, zero_start, zero_chunks, block_expert,
         n_used) = _dispatch_plan(eidx, member, valid, n_experts, n_rows)
        x_sorted = _dispatch(x1b, loc_t, tile_chunks, chunk_buf, chunk_hbm, zero_start, zero_chunks,
                             n_rows, n_experts)
        y_sorted = _moe_experts(x_sorted, moe_w1, moe_w3, moe_w2, l, block_expert, n_used)

        w13 = jnp.concatenate([moe_shared_w1[l], moe_shared_w3[l]], axis=1).astype(BF16)
        x, xb = _combine_ln(x1b, x1, loc, gate, y_sorted, tile_chunks, chunk_buf, chunk_hbm, w13,
                            moe_shared_w2[l].astype(BF16), ln_g[l, 1].astype(F32).reshape(1, d),
                            ln_b[l, 1].astype(F32).reshape(1, d), alpha, n_experts)

    y_prompt = x[:t_real].reshape(bsz, seq, d)
    y_sample = x[s_base:m_base].reshape(dec_b, dec_s, d)
    st = lambda k: jnp.stack(outs[k])
    return (y_prompt, y_sample, st("mk"), st("mv"), st("wk"), st("wv"), st("ks"), st("vs"),
            st("cp"), st("hp"), st("cs"), st("hs"))
```

```python
import functools
import math

import numpy as np
import jax
import jax.numpy as jnp
from jax import lax
from jax.experimental import pallas as pl
from jax.experimental.pallas import tpu as pltpu

F32 = jnp.float32
BF16 = jnp.bfloat16

PAST_LEN = 1024
CHUNK = 64
WINDOW = 128
HEAD_DIM = 64
GROUP = 8
ROT_DIM = HEAD_DIM // 4
ROPE_THETA = 500000.0
ATTN_SCALE = HEAD_DIM ** -0.5
N_LRU_BLOCKS = 8
CONV_W = 4
LRU_C = 8.0
N_GROUPS = 8
TOPK_GROUPS = 4
TOPK = 8
ROUTED_SCALE = 2.5
LN_EPS = 1e-5
NEG_INF = -1e30

LANES = 128
SUBLANES = 8
QBLK = 2 * CHUNK
META_BLK = 128
ROW_TILE = 512
PROJ_TILE = 256
LRU_TILE = 256
SCAN_LANES = 512
MOE_BLK = 512
CMB_TILE = 256
RUN_ALIGN = SUBLANES
VMEM_LIMIT = 56 * 1024 * 1024


def _cparams(*sem):
    return pltpu.CompilerParams(dimension_semantics=sem, vmem_limit_bytes=VMEM_LIMIT)


def _mm_kernel(x_ref, w_ref, o_ref):
    o_ref[...] = jnp.dot(x_ref[...], w_ref[...], preferred_element_type=F32)


def _matmul(x, w, tn, name):
    m, k = x.shape
    n = w.shape[1]
    tm = ROW_TILE
    return pl.pallas_call(
        _mm_kernel,
        grid=(n // tn, m // tm),
        in_specs=[pl.BlockSpec((tm, k), lambda j, i: (i, 0)),
                  pl.BlockSpec((k, tn), lambda j, i: (0, j))],
        out_specs=pl.BlockSpec((tm, tn), lambda j, i: (i, j)),
        out_shape=jax.ShapeDtypeStruct((m, n), F32),
        compiler_params=_cparams("parallel", "parallel"),
        name=name,
    )(x, w)


def _qkv_kernel(x_ref, w_ref, c_ref, s1_ref, s2_ref, o_ref, *, tn, rope_cols):
    j = pl.program_id(0)
    acc = jnp.dot(x_ref[...], w_ref[...], preferred_element_type=F32)
    c, s1, s2 = c_ref[...], s1_ref[...], s2_ref[...]
    lane = lax.broadcasted_iota(jnp.int32, (acc.shape[0], LANES), 1)
    for g in range(tn // LANES):
        a = acc[:, g * LANES:(g + 1) * LANES]
        lo = pltpu.roll(a, ROT_DIM // 2, 1)
        hi = pltpu.roll(a, LANES - ROT_DIM // 2, 1)
        roped = a * c + lo * s1 + hi * s2
        col = j * tn + g * LANES + lane
        o_ref[:, g * LANES:(g + 1) * LANES] = jnp.where(col < rope_cols, roped, a)


def _qkv_proj(xb, w, cos_t, sin_lo_t, sin_hi_t, rope_cols):
    m, k = xb.shape
    n = w.shape[1]
    tm = ROW_TILE
    tn = 2560 if n % 2560 == 0 else 256
    assert n % tn == 0 and m % tm == 0
    tab = pl.BlockSpec((tm, LANES), lambda j, i: (i, 0))
    return pl.pallas_call(
        functools.partial(_qkv_kernel, tn=tn, rope_cols=rope_cols),
        grid=(n // tn, m // tm),
        in_specs=[pl.BlockSpec((tm, k), lambda j, i: (i, 0)),
                  pl.BlockSpec((k, tn), lambda j, i: (0, j)),
                  tab, tab, tab],
        out_specs=pl.BlockSpec((tm, tn), lambda j, i: (i, j)),
        out_shape=jax.ShapeDtypeStruct((m, n), F32),
        compiler_params=_cparams("parallel", "parallel"),
        name="qkv_rope",
    )(xb, w, cos_t, sin_lo_t, sin_hi_t)


def _layer_norm_rows(z, g, b):
    mu = jnp.mean(z, -1, keepdims=True)
    d = z - mu
    var = jnp.mean(d * d, -1, keepdims=True)
    return d * lax.rsqrt(var + LN_EPS) * g + b


def _proj_ln_kernel(a_ref, w_ref, x_ref, g_ref, b_ref, wrh_ref, wrl_ref, o_ref, ob_ref, lg_ref, *, alpha):
    acc = jnp.dot(a_ref[...], w_ref[...], preferred_element_type=F32)
    y = _layer_norm_rows(alpha * x_ref[...] + acc, g_ref[...], b_ref[...])
    o_ref[...] = y
    yb = y.astype(BF16)
    ob_ref[...] = yb
    ylo = (y - yb.astype(F32)).astype(BF16)
    wrh = wrh_ref[...]
    lg_ref[...] = (jnp.dot(yb, wrh, preferred_element_type=F32)
                   + jnp.dot(ylo, wrh, preferred_element_type=F32)
                   + jnp.dot(yb, wrl_ref[...], preferred_element_type=F32))


def _proj_ln(a, w, x, g, b, wr_hi, wr_lo, alpha):
    m, k = a.shape
    d = w.shape[1]
    ne = wr_hi.shape[1]
    tm = PROJ_TILE
    row = lambda i: (i, 0)
    fix = lambda i: (0, 0)
    return pl.pallas_call(
        functools.partial(_proj_ln_kernel, alpha=alpha),
        grid=(m // tm,),
        in_specs=[pl.BlockSpec((tm, k), row), pl.BlockSpec((k, d), fix), pl.BlockSpec((tm, d), row),
                  pl.BlockSpec((1, d), fix), pl.BlockSpec((1, d), fix),
                  pl.BlockSpec((d, ne), fix), pl.BlockSpec((d, ne), fix)],
        out_specs=[pl.BlockSpec((tm, d), row), pl.BlockSpec((tm, d), row), pl.BlockSpec((tm, ne), row)],
        out_shape=[jax.ShapeDtypeStruct((m, d), F32), jax.ShapeDtypeStruct((m, d), BF16),
                   jax.ShapeDtypeStruct((m, ne), F32)],
        compiler_params=_cparams("parallel"),
        name="proj_ln_router",
    )(a, w, x, g, b, wr_hi, wr_lo)


def _sigmoid(x):
    return 0.5 * (jnp.tanh(0.5 * x) + 1.0)


def _silu(x):
    return x * _sigmoid(x)


COMBINE_ROWS = 256
WAIT_GROUPS = (32, 4, 1)
CHUNK_LIST = 512


def _wait_chunks(make_copy, n):
    for g in WAIT_GROUPS:
        cnt = n // g
        n = n - cnt * g
        lax.fori_loop(0, cnt, lambda j, c, g=g: (make_copy(g * RUN_ALIGN).wait(), c)[1], 0)


def _combine_ln_kernel(tot_ref, buf_rows_ref, hbm_rows_ref, xb_ref, x_ref, loc_ref, gate_ref, y_hbm, w13_ref, w2_ref,
                       g_ref, b_ref, o_ref, ob_ref, ybuf, sem, *, alpha, dh):
    i = pl.program_id(0)
    rows = ybuf.shape[0]

    def clear(j, carry):
        ybuf[pl.ds(pl.multiple_of(j * RUN_ALIGN, RUN_ALIGN), RUN_ALIGN), :] = jnp.zeros(
            (RUN_ALIGN, ybuf.shape[1]), ybuf.dtype)
        return carry

    lax.fori_loop(tot_ref[i], rows // RUN_ALIGN, clear, 0)

    def fetch(c, carry):
        src = pl.multiple_of(hbm_rows_ref[c], RUN_ALIGN)
        dst = pl.multiple_of(buf_rows_ref[c], RUN_ALIGN)
        pltpu.make_async_copy(y_hbm.at[pl.ds(src, RUN_ALIGN)], ybuf.at[pl.ds(dst, RUN_ALIGN)], sem).start()
        return carry

    lax.fori_loop(0, tot_ref[i], fetch, 0)

    u = jnp.dot(xb_ref[...], w13_ref[...], preferred_element_type=F32)
    hs = (_silu(u[:, :dh]) * u[:, dh:]).astype(BF16)
    f = jnp.dot(hs, w2_ref[...], preferred_element_type=F32)

    _wait_chunks(lambda n: pltpu.make_async_copy(y_hbm.at[pl.ds(0, n)], ybuf.at[pl.ds(0, n)], sem), tot_ref[i])

    loc = loc_ref[...]
    gate = gate_ref[...]
    tm = loc.shape[0]
    r_lo = jnp.zeros((tm, ybuf.shape[1]), F32)
    r_hi = jnp.zeros((tm, ybuf.shape[1]), F32)
    for c0 in range(0, rows, COMBINE_ROWS):
        col = c0 + lax.broadcasted_iota(jnp.int32, (tm, COMBINE_ROWS), 1)
        sel = jnp.zeros((tm, COMBINE_ROWS), F32)
        for k in range(TOPK):
            sel = jnp.where(col == loc[:, k:k + 1], gate[:, k:k + 1], sel)
        sel = sel.astype(BF16)
        y_lo, y_hi = _unpack_halves(ybuf[c0:c0 + COMBINE_ROWS, :])
        r_lo = r_lo + jnp.dot(sel, y_lo, preferred_element_type=F32)
        r_hi = r_hi + jnp.dot(sel, y_hi, preferred_element_type=F32)
    f = f + jnp.concatenate([r_lo, r_hi], axis=1)
    y = _layer_norm_rows(alpha * x_ref[...] + f, g_ref[...], b_ref[...])
    o_ref[...] = y
    ob_ref[...] = y.astype(BF16)


def _combine_ln(xb, x, loc, gate, y_sorted, tile_chunks, chunk_buf, chunk_hbm, w13, w2, g, b, alpha, n_experts):
    m, d = x.shape
    dh = w2.shape[0]
    tm = CMB_TILE
    buf_rows = -(-(tm * TOPK + n_experts * (RUN_ALIGN - 1)) // 256) * 256
    row = lambda i, *_: (i, 0)
    fix = lambda i, *_: (0, 0)
    lst = pl.BlockSpec((CHUNK_LIST,), lambda i, *_: (i,), memory_space=pltpu.SMEM)
    grid_spec = pltpu.PrefetchScalarGridSpec(
        num_scalar_prefetch=1,
        grid=(m // tm,),
        in_specs=[lst, lst, pl.BlockSpec((tm, d), row), pl.BlockSpec((tm, d), row), pl.BlockSpec((tm, TOPK), row),
                  pl.BlockSpec((tm, TOPK), row), pl.BlockSpec(memory_space=pl.ANY),
                  pl.BlockSpec((d, 2 * dh), fix), pl.BlockSpec((dh, d), fix),
                  pl.BlockSpec((1, d), fix), pl.BlockSpec((1, d), fix)],
        out_specs=[pl.BlockSpec((tm, d), row), pl.BlockSpec((tm, d), row)],
        scratch_shapes=[pltpu.VMEM((buf_rows, y_sorted.shape[1]), y_sorted.dtype), pltpu.SemaphoreType.DMA(())],
    )
    return pl.pallas_call(
        functools.partial(_combine_ln_kernel, alpha=alpha, dh=dh),
        grid_spec=grid_spec,
        out_shape=[jax.ShapeDtypeStruct((m, d), F32), jax.ShapeDtypeStruct((m, d), BF16)],
        compiler_params=_cparams("arbitrary"),
        name="combine_ln",
    )(tile_chunks, chunk_buf, chunk_hbm, xb, x, loc, gate, y_sorted, w13, w2, g, b)


def _attn_kernel(q_ref, k0_ref, k1_ref, k2_ref, v0_ref, v1_ref, v2_ref, bias_ref, sink_ref, init_ref, o_ref,
                 ot_ref, *, n_kv):
    del init_ref
    hd = HEAD_DIM
    qt = (q_ref[...] * ATTN_SCALE).T.astype(BF16)
    kb = jnp.concatenate([k0_ref[...], k1_ref[...], k2_ref[...]], 0).astype(BF16)
    vt = jnp.concatenate([v0_ref[...], v1_ref[...], v2_ref[...]], 0).T.astype(BF16)
    bias = bias_ref[0]
    bias = jnp.concatenate([bias] * GROUP, axis=1)
    for h in range(n_kv):
        kh = kb[:, h * hd:(h + 1) * hd]
        qth = jnp.concatenate(
            [qt[(h * GROUP + g) * hd:(h * GROUP + g + 1) * hd, :] for g in range(GROUP)], axis=1)
        s = jnp.dot(kh, qth, preferred_element_type=F32) + bias
        sink = sink_ref[h]
        m = jnp.maximum(jnp.max(s, axis=0, keepdims=True), sink)
        p = jnp.exp(s - m)
        den = jnp.sum(p, axis=0, keepdims=True) + jnp.exp(sink - m)
        ot = jnp.dot(vt[h * hd:(h + 1) * hd, :], p.astype(BF16), preferred_element_type=F32)
        ot = ot * (1.0 / den)
        for g in range(GROUP):
            r0 = (h * GROUP + g) * hd
            ot_ref[r0:r0 + hd, :] = ot[:, g * QBLK:(g + 1) * QBLK]
    o_ref[...] = ot_ref[...].T.astype(BF16)


def _attention(q_arr, k_arr, n_steps, q_map, kv_maps, bias, bias_map, sink_rows, init, out_map,
               q_w, kv_w, k_col, v_col, name):
    n_kv = kv_w // HEAD_DIM
    kspecs = [pl.BlockSpec((QBLK, kv_w), (lambda i, f=f: (f(i), k_col))) for f in kv_maps]
    vspecs = [pl.BlockSpec((QBLK, kv_w), (lambda i, f=f: (f(i), v_col))) for f in kv_maps]
    nk = 3 * QBLK
    return pl.pallas_call(
        functools.partial(_attn_kernel, n_kv=n_kv),
        grid=(n_steps,),
        in_specs=[pl.BlockSpec((QBLK, q_w), lambda i: (q_map(i), 0))] + kspecs + vspecs + [
            pl.BlockSpec((1, nk, QBLK), lambda i: (bias_map(i), 0, 0)),
            pl.BlockSpec((n_kv, 1, GROUP * QBLK), lambda i: (0, 0, 0)),
            pl.BlockSpec(memory_space=pl.ANY)],
        out_specs=pl.BlockSpec((QBLK, q_w), lambda i: (out_map(i), 0)),
        out_shape=jax.ShapeDtypeStruct(init.shape, BF16),
        scratch_shapes=[pltpu.VMEM((q_w, QBLK), F32)],
        input_output_aliases={9: 0},
        compiler_params=_cparams("parallel"),
        name=name,
    )(q_arr, k_arr, k_arr, k_arr, k_arr, k_arr, k_arr, bias, sink_rows, init)


def _gelu_tanh(x):
    c = math.sqrt(2.0 / math.pi)
    return x * (0.5 * (1.0 + jnp.tanh(c * (x + 0.044715 * (x * x * x)))))


def _lru_kernel(xb_ref, gb_ref, cprev_ref, hprev_ref, cw_ref, cb_ref, gaw_ref, gab_ref, gxw_ref, gxb_ref,
                sp_ref, init_ref, y_ref, cnew_ref, hnew_ref, xcat, a_scr, b_scr, hcar, *, tt, valid_last):
    del init_ref
    j = pl.program_id(1)
    c_rnn = xb_ref.shape[1]
    bw = c_rnn // N_LRU_BLOCKS
    tail = SUBLANES

    @pl.when(j == 0)
    def _():
        xcat[0:tail, :] = jnp.zeros((tail, c_rnn), F32)
        xcat[tail - (CONV_W - 1):tail, :] = cprev_ref[...]
        hcar[...] = jnp.broadcast_to(hprev_ref[...], (SUBLANES, c_rnn))

    xcat[tail:tail + tt, :] = xb_ref[...]
    cw = cw_ref[...]
    xc = cb_ref[...]
    for tap in range(CONV_W):
        off = tail - (CONV_W - 1) + tap
        xc = xc + xcat[off:off + tt, :] * cw[tap:tap + 1, :]
    cnew_ref[...] = xcat[tail + valid_last - (CONV_W - 1):tail + valid_last, :]
    xcat[0:tail, :] = xcat[tt:tt + tail, :]

    xcb = xc.astype(BF16)
    rs, gs = [], []
    for n in range(N_LRU_BLOCKS):
        xs = xcb[:, n * bw:(n + 1) * bw]
        rs.append(jnp.dot(xs, gaw_ref[n], preferred_element_type=F32))
        gs.append(jnp.dot(xs, gxw_ref[n], preferred_element_type=F32))
    r = _sigmoid(jnp.concatenate(rs, axis=1) + gab_ref[...])
    gi = _sigmoid(jnp.concatenate(gs, axis=1) + gxb_ref[...])
    log_a = (-LRU_C * r) * sp_ref[...]
    a = jnp.exp(log_a)
    a_scr[...] = a
    b_scr[...] = jnp.sqrt(-jnp.tanh(log_a) * (a * a + 1.0)) * (gi * xc)

    row = lax.broadcasted_iota(jnp.int32, (SUBLANES, SCAN_LANES), 0)
    for c in range(c_rnn // SCAN_LANES):
        cs = slice(c * SCAN_LANES, (c + 1) * SCAN_LANES)

        def body(i, carry, cs=cs):
            r0 = pl.multiple_of(i * SUBLANES, SUBLANES)
            av = a_scr[pl.ds(r0, SUBLANES), cs]
            bv = b_scr[pl.ds(r0, SUBLANES), cs]
            for k in (1, 2, 4):
                a_sh = jnp.where(row >= k, pltpu.roll(av, k, 0), 1.0)
                b_sh = jnp.where(row >= k, pltpu.roll(bv, k, 0), 0.0)
                bv = av * b_sh + bv
                av = av * a_sh
            hv = av * carry + bv
            b_scr[pl.ds(r0, SUBLANES), cs] = hv
            return jnp.broadcast_to(hv[SUBLANES - 1:SUBLANES, :], (SUBLANES, SCAN_LANES))

        hcar[:, cs] = lax.fori_loop(0, tt // SUBLANES, body, hcar[:, cs])

    h = b_scr[...]
    hnew_ref[...] = b_scr[valid_last - 1:valid_last, :]
    y_ref[...] = (h * _gelu_tanh(gb_ref[...])).astype(BF16)


def _lru_seq(u, n_seq, n_tiles, tt, row0, seq_stride, valid_last, init, out_row0, cprev, hprev, wts):
    c_rnn = u.shape[1] // 2
    cw, cb, gaw, gab, gxw, gxb, sp = wts
    b0, bs, ob0 = row0 // tt, seq_stride // tt, out_row0 // tt
    bw = c_rnn // N_LRU_BLOCKS
    fix2 = lambda s, j: (0, 0)
    fix3 = lambda s, j: (0, 0, 0)
    per_seq = lambda s, j: (s, 0, 0)
    return pl.pallas_call(
        functools.partial(_lru_kernel, tt=tt, valid_last=valid_last),
        grid=(n_seq, n_tiles),
        in_specs=[pl.BlockSpec((tt, c_rnn), lambda s, j: (b0 + s * bs + j, 0)),
                  pl.BlockSpec((tt, c_rnn), lambda s, j: (b0 + s * bs + j, 1)),
                  pl.BlockSpec((None, CONV_W - 1, c_rnn), per_seq),
                  pl.BlockSpec((None, 1, c_rnn), per_seq),
                  pl.BlockSpec((CONV_W, c_rnn), fix2), pl.BlockSpec((1, c_rnn), fix2),
                  pl.BlockSpec((N_LRU_BLOCKS, bw, bw), fix3), pl.BlockSpec((1, c_rnn), fix2),
                  pl.BlockSpec((N_LRU_BLOCKS, bw, bw), fix3), pl.BlockSpec((1, c_rnn), fix2),
                  pl.BlockSpec((1, c_rnn), fix2), pl.BlockSpec(memory_space=pl.ANY)],
        out_specs=[pl.BlockSpec((tt, c_rnn), lambda s, j: (ob0 + s * bs + j, 0)),
                   pl.BlockSpec((None, CONV_W - 1, c_rnn), per_seq),
                   pl.BlockSpec((None, 1, c_rnn), per_seq)],
        out_shape=[jax.ShapeDtypeStruct(init.shape, BF16),
                   jax.ShapeDtypeStruct((n_seq, CONV_W - 1, c_rnn), F32),
                   jax.ShapeDtypeStruct((n_seq, 1, c_rnn), F32)],
        scratch_shapes=[pltpu.VMEM((SUBLANES + tt, c_rnn), F32), pltpu.VMEM((tt, c_rnn), F32),
                        pltpu.VMEM((tt, c_rnn), F32), pltpu.VMEM((SUBLANES, c_rnn), F32)],
        input_output_aliases={11: 0},
        compiler_params=_cparams("parallel", "arbitrary"),
        name="rglru_seq",
    )(u, u, cprev, hprev, cw, cb, gaw, gab, gxw, gxb, sp, init)


def _pack_halves(y):
    h = y.shape[1] // 2
    lo = lax.bitcast_convert_type(y[:, :h].astype(BF16).astype(F32), jnp.uint32)
    hi = lax.bitcast_convert_type(y[:, h:].astype(BF16).astype(F32), jnp.uint32)
    return (lo >> 16) | (hi & jnp.uint32(0xFFFF0000))


def _unpack_halves(w):
    lo = lax.bitcast_convert_type(w << 16, F32).astype(BF16)
    hi = lax.bitcast_convert_type(w & jnp.uint32(0xFFFF0000), F32).astype(BF16)
    return lo, hi


DISPATCH_ROWS = 128


def _dispatch_kernel(tot_ref, zs_ref, zc_ref, buf_rows_ref, hbm_rows_ref, x_ref, loct_ref, o_hbm, obuf, zbuf, sem,
                     zsem, *, n_experts):
    i = pl.program_id(0)
    _, rows, hw = obuf.shape
    tm = x_ref.shape[0]
    slot = i % 2

    def drain(s, n):
        _wait_chunks(lambda m: pltpu.make_async_copy(obuf.at[s, pl.ds(0, m)], o_hbm.at[pl.ds(0, m)], sem.at[s]), n)

    @pl.when(i >= 2)
    def _():
        drain(slot, tot_ref[jnp.maximum(i - 2, 0)])

    loct = loct_ref[...]
    x = x_ref[...]
    x_lo, x_hi = x[:, :hw], x[:, hw:]
    for r0 in range(0, rows, DISPATCH_ROWS):
        rowi = r0 + lax.broadcasted_iota(jnp.int32, (DISPATCH_ROWS, tm), 0)
        p = jnp.zeros((DISPATCH_ROWS, tm), F32)
        for k in range(TOPK):
            p = jnp.where(rowi == loct[k:k + 1, :], 1.0, p)
        pb = p.astype(BF16)
        lo = jnp.dot(pb, x_lo, preferred_element_type=F32)
        hi = jnp.dot(pb, x_hi, preferred_element_type=F32)
        obuf[slot, r0:r0 + DISPATCH_ROWS, :] = ((lax.bitcast_convert_type(lo, jnp.uint32) >> 16)
                                                | lax.bitcast_convert_type(hi, jnp.uint32))

    def send(c, carry):
        src = pl.multiple_of(buf_rows_ref[c], RUN_ALIGN)
        dst = pl.multiple_of(hbm_rows_ref[c], RUN_ALIGN)
        pltpu.make_async_copy(obuf.at[slot, pl.ds(src, RUN_ALIGN)], o_hbm.at[pl.ds(dst, RUN_ALIGN)],
                              sem.at[slot]).start()
        return carry

    lax.fori_loop(0, tot_ref[i], send, 0)

    @pl.when(i == pl.num_programs(0) - 1)
    def _():
        drain(slot, tot_ref[i])

        @pl.when(i >= 1)
        def _():
            drain(1 - slot, tot_ref[jnp.maximum(i - 1, 0)])

        zbuf[...] = jnp.zeros(zbuf.shape, zbuf.dtype)

        def zero_copy(dst_row):
            return pltpu.make_async_copy(zbuf, o_hbm.at[pl.ds(pl.multiple_of(dst_row, RUN_ALIGN), RUN_ALIGN)], zsem)

        def zero_expert(e, c):
            return lax.fori_loop(0, zc_ref[e], lambda j, c2: (zero_copy(zs_ref[e] + j * RUN_ALIGN).start(), c2)[1], c)

        def zero_wait(e, c):
            return lax.fori_loop(0, zc_ref[e], lambda j, c2: (zero_copy(0).wait(), c2)[1], c)

        lax.fori_loop(0, n_experts, zero_expert, 0)
        lax.fori_loop(0, n_experts, zero_wait, 0)


def _dispatch(xb, loc_t, tile_chunks, chunk_buf, chunk_hbm, zero_start, zero_chunks, n_rows, n_experts):
    m, d = xb.shape
    tm = CMB_TILE
    buf_rows = -(-(tm * TOPK + n_experts * (RUN_ALIGN - 1)) // DISPATCH_ROWS) * DISPATCH_ROWS
    lst = pl.BlockSpec((CHUNK_LIST,), lambda i, *_: (i,), memory_space=pltpu.SMEM)
    grid_spec = pltpu.PrefetchScalarGridSpec(
        num_scalar_prefetch=3,
        grid=(m // tm,),
        in_specs=[lst, lst, pl.BlockSpec((tm, d), lambda i, *_: (i, 0)),
                  pl.BlockSpec((None, TOPK, tm), lambda i, *_: (i, 0, 0))],
        out_specs=pl.BlockSpec(memory_space=pl.ANY),
        scratch_shapes=[pltpu.VMEM((2, buf_rows, d // 2), jnp.uint32), pltpu.VMEM((RUN_ALIGN, d // 2), jnp.uint32),
                        pltpu.SemaphoreType.DMA((2,)), pltpu.SemaphoreType.DMA(())],
    )
    return pl.pallas_call(
        functools.partial(_dispatch_kernel, n_experts=n_experts),
        grid_spec=grid_spec,
        out_shape=jax.ShapeDtypeStruct((n_rows, d // 2), jnp.uint32),
        compiler_params=_cparams("arbitrary"),
        name="moe_dispatch",
    )(tile_chunks, zero_start, zero_chunks, chunk_buf, chunk_hbm, xb, loc_t)


def _moe_kernel(be_ref, nu_ref, x_ref, w1_ref, w3_ref, w2_ref, o_ref, w1b, w3b, w2b):
    i = pl.program_id(0)

    @pl.when(i < nu_ref[0])
    def _():
        prev = be_ref[jnp.maximum(i - 1, 0)]

        @pl.when(jnp.logical_or(i == 0, be_ref[i] != prev))
        def _():
            w1b[...] = w1_ref[...].astype(BF16)
            w3b[...] = w3_ref[...].astype(BF16)
            w2b[...] = w2_ref[...].astype(BF16)

        xlo, xhi = _unpack_halves(x_ref[...])
        hk = xlo.shape[1]
        a = (jnp.dot(xlo, w1b[0:hk, :], preferred_element_type=F32)
             + jnp.dot(xhi, w1b[hk:2 * hk, :], preferred_element_type=F32))
        b = (jnp.dot(xlo, w3b[0:hk, :], preferred_element_type=F32)
             + jnp.dot(xhi, w3b[hk:2 * hk, :], preferred_element_type=F32))
        h = (_silu(a) * b).astype(BF16)
        o_ref[...] = _pack_halves(jnp.dot(h, w2b[...], preferred_element_type=F32))


def _moe_experts(x_sorted, w1, w3, w2, layer, block_expert, n_used):
    n_rows, hw = x_sorted.shape
    d, de = w1.shape[2], w1.shape[3]
    nb = n_rows // MOE_BLK
    blk = lambda i, be, nu: (jnp.minimum(i, nu[0] - 1), 0)
    wmap = lambda i, be, nu: (layer, be[jnp.minimum(i, nu[0] - 1)], 0, 0)
    grid_spec = pltpu.PrefetchScalarGridSpec(
        num_scalar_prefetch=2,
        grid=(nb,),
        in_specs=[pl.BlockSpec((MOE_BLK, hw), blk),
                  pl.BlockSpec((None, None, d, de), wmap), pl.BlockSpec((None, None, d, de), wmap),
                  pl.BlockSpec((None, None, de, d), wmap)],
        out_specs=pl.BlockSpec((MOE_BLK, d // 2), blk),
        scratch_shapes=[pltpu.VMEM((d, de), BF16), pltpu.VMEM((d, de), BF16), pltpu.VMEM((de, d), BF16)],
    )
    return pl.pallas_call(
        _moe_kernel,
        grid_spec=grid_spec,
        out_shape=jax.ShapeDtypeStruct((n_rows, d // 2), jnp.uint32),
        compiler_params=_cparams("arbitrary"),
        name="moe_experts",
    )(block_expert, n_used, x_sorted, w1, w3, w2)


def _route(logits, router_bias, valid, n_experts):
    t = logits.shape[0]
    per = n_experts // N_GROUPS
    scores = jax.nn.sigmoid(logits)
    biased = scores + router_bias.astype(F32)
    grp = biased.reshape(t, N_GROUPS, per)
    m1 = jnp.max(grp, -1, keepdims=True)
    is_max = grp == m1
    first = is_max & (jnp.cumsum(is_max.astype(jnp.int32), -1) == 1)
    m2 = jnp.max(jnp.where(first, -jnp.inf, grp), -1)
    grp_score = m1[..., 0] + m2

    def rank(v):
        n = v.shape[-1]
        idx = jnp.arange(n)
        vi, vj = v[:, :, None], v[:, None, :]
        beaten = (vj > vi) | ((vj == vi) & (idx[None, None, :] < idx[None, :, None]))
        return jnp.sum(beaten.astype(jnp.int32), -1)

    gsel = rank(grp_score) < TOPK_GROUPS
    emask = jnp.repeat(gsel, per, axis=-1)
    e_rank = rank(jnp.where(emask, biased, NEG_INF))
    slot = e_rank[:, None, :] == jnp.arange(TOPK)[None, :, None]
    eidx = jnp.sum(jnp.where(slot, jnp.arange(n_experts)[None, None, :], 0), -1)
    g = jnp.sum(jnp.where(slot, scores[:, None, :], 0.0), -1)
    g = g / (jnp.sum(g, -1, keepdims=True) + 1e-20) * ROUTED_SCALE
    g = jnp.where(valid[:, None], g, 0.0)
    member = ((e_rank < TOPK) & valid[:, None]).astype(jnp.int32)
    return eidx.astype(jnp.int32), g, member


def _dispatch_plan(eidx, member, valid, n_experts, n_rows):
    t = eidx.shape[0]
    nt = t // CMB_TILE
    i32 = jnp.int32
    m3 = member.reshape(nt, CMB_TILE, n_experts)
    cnt = jnp.sum(m3, axis=1)
    rlen = (cnt + RUN_ALIGN - 1) // RUN_ALIGN * RUN_ALIGN
    tri = (jnp.arange(CMB_TILE)[:, None] > jnp.arange(CMB_TILE)[None, :]).astype(BF16)
    rank = jnp.einsum("ts,nse->nte", tri, m3.astype(BF16), preferred_element_type=F32)
    rank = rank.astype(i32).reshape(t, n_experts)
    seg = jnp.sum(rlen, axis=0)
    padded = (seg + MOE_BLK - 1) // MOE_BLK * MOE_BLK
    pad_end = jnp.cumsum(padded)
    pad_start = pad_end - padded
    run_off = jnp.cumsum(rlen, axis=0) - rlen
    run_start = pad_start[None, :] + run_off
    lbase = jnp.cumsum(rlen, axis=1) - rlen
    pos = (jnp.repeat(lbase, CMB_TILE, axis=0) + rank)
    hit = eidx[:, :, None] == jnp.arange(n_experts, dtype=i32)[None, None, :]
    loc = jnp.sum(jnp.where(hit, pos[:, None, :], 0), axis=-1)
    loc = jnp.where(valid[:, None], loc, -1).astype(i32)
    loc_t = loc.reshape(nt, CMB_TILE, TOPK).transpose(0, 2, 1)
    nb = n_rows // MOE_BLK
    blk_row = jnp.arange(nb, dtype=i32) * MOE_BLK
    block_expert = jnp.minimum(jnp.sum((pad_end[None, :] <= blk_row[:, None]).astype(i32), axis=1),
                               n_experts - 1).astype(i32)
    n_used = jnp.maximum(pad_end[-1] // MOE_BLK, 1).astype(i32).reshape(1)

    n_chunks = (rlen // RUN_ALIGN).astype(i32)
    cum = jnp.cumsum(n_chunks, axis=1)
    c = jnp.arange(CHUNK_LIST, dtype=i32)
    e_c = jnp.minimum(jnp.sum((cum[:, None, :] <= c[None, :, None]).astype(i32), axis=-1), n_experts - 1)
    own = e_c[:, :, None] == jnp.arange(n_experts, dtype=i32)[None, None, :]
    pick = lambda tab: jnp.sum(jnp.where(own, tab[:, None, :], 0), axis=-1)
    off = (c[None, :] - pick(cum - n_chunks)) * RUN_ALIGN
    chunk_buf = (pick(lbase) + off).reshape(-1).astype(i32)
    chunk_hbm = (pick(run_start) + off).reshape(-1).astype(i32)
    return (loc, loc_t, cum[:, -1].astype(i32), chunk_buf, chunk_hbm, (pad_start + seg).astype(i32),
            ((padded - seg) // RUN_ALIGN).astype(i32), block_expert, n_used)


def _split_bf16(w):
    hi = w.astype(BF16)
    lo = (w - hi.astype(F32)).astype(BF16)
    return hi, lo


def kernel(x_prompt, x_sample, cache_meta_k, cache_meta_v, cache_win_k, cache_win_v, state_conv, state_h, meta_tokens, ln_g, ln_b, attn_w_in, attn_w_out, attn_sink, lru_w_in, lru_conv_w, lru_conv_b, lru_gate_a_w, lru_gate_a_b, lru_gate_x_w, lru_gate_x_b, lru_lambda, lru_w_out, moe_router_w, moe_router_bias, moe_w1, moe_w3, moe_w2, moe_shared_w1, moe_shared_w3, moe_shared_w2):
    bsz, seq, d = x_prompt.shape
    dec_b, dec_s, _ = x_sample.shape
    n_meta = meta_tokens.shape[0]
    depth = ln_g.shape[0]
    n_experts = moe_router_w.shape[2]
    past_len = PAST_LEN
    n_heads = d // HEAD_DIM
    n_kv = n_heads // GROUP
    q_w, kv_w = n_heads * HEAD_DIM, n_kv * HEAD_DIM
    c_rnn = lru_w_in.shape[2] // 2
    alpha = (2 * depth) ** 0.25

    t_real = bsz * seq
    s_base, s_rows = t_real, dec_b * dec_s
    m_base = s_base + s_rows
    t_pad = m_base + bsz * META_BLK
    assert seq % LRU_TILE == 0 and seq % QBLK == 0 and s_rows % QBLK == 0 and t_pad % ROW_TILE == 0
    assert n_meta <= META_BLK and dec_s <= QBLK and q_w == d and kv_w % LANES == 0
    assert (CMB_TILE * TOPK + n_experts * (RUN_ALIGN - 1)) // RUN_ALIGN <= CHUNK_LIST and t_pad % CMB_TILE == 0
    assert cache_win_k.shape[2] == WINDOW and WINDOW == QBLK and n_meta + WINDOW + dec_s <= 3 * QBLK
    nqb = seq // QBLK

    pos = np.zeros((t_pad,), np.float32)
    pos[:t_real] = np.tile(np.arange(seq) + n_meta, bsz)
    pos[s_base:m_base] = np.tile(past_len + n_meta + np.arange(dec_s), dec_b)
    valid_np = np.zeros((t_pad,), bool)
    valid_np[:m_base] = True
    for b in range(bsz):
        pos[m_base + b * META_BLK:m_base + b * META_BLK + n_meta] = np.arange(n_meta)
        valid_np[m_base + b * META_BLK:m_base + b * META_BLK + n_meta] = True
    valid = jnp.asarray(valid_np)

    half = ROT_DIM // 2
    freqs = ROPE_THETA ** (-jnp.arange(0, ROT_DIM, 2, dtype=F32) / ROT_DIM)
    ang = jnp.asarray(pos)[:, None] * freqs[None, :]
    cos, sin = jnp.cos(ang), jnp.sin(ang)
    ones = jnp.ones((t_pad, HEAD_DIM - ROT_DIM), F32)
    zeros_h = jnp.zeros((t_pad, half), F32)
    zeros_r = jnp.zeros((t_pad, HEAD_DIM - ROT_DIM), F32)
    reps = LANES // HEAD_DIM
    cos_t = jnp.tile(jnp.concatenate([cos, cos, ones], 1), (1, reps))
    sin_lo_t = jnp.tile(jnp.concatenate([zeros_h, sin, zeros_r], 1), (1, reps))
    sin_hi_t = jnp.tile(jnp.concatenate([-sin, zeros_h, zeros_r], 1), (1, reps))

    kj = np.arange(3 * QBLK)[:, None]
    qc = np.arange(QBLK)[None, :] // CHUNK
    key_chunk = np.where(kj < 2 * QBLK, kj // CHUNK - 2, 0)
    band_ok = (kj < 2 * QBLK) & (qc - key_chunk >= 0) & (qc - key_chunk <= WINDOW // CHUNK)
    meta_ok = (kj >= 2 * QBLK) & (kj < 2 * QBLK + n_meta)
    later = band_ok | meta_ok
    first = (band_ok & (kj >= QBLK)) | meta_ok
    only_meta = np.broadcast_to(meta_ok, later.shape)
    bias_p = jnp.asarray(np.where(np.stack([first, later, only_meta]), 0.0, NEG_INF).astype(np.float32))
    n_keys_s = n_meta + WINDOW + dec_s
    bias_s = jnp.asarray(np.where(np.broadcast_to(kj < n_keys_s, later.shape), 0.0, NEG_INF)
                         .astype(np.float32))[None]

    meta_rows = jnp.concatenate([meta_tokens.astype(F32), jnp.zeros((META_BLK - n_meta, d), F32)], 0)
    x = jnp.concatenate([x_prompt.reshape(t_real, d), x_sample.reshape(s_rows, d),
                         jnp.tile(meta_rows, (bsz, 1))], 0)
    xb = x.astype(BF16)

    n_assign = (t_real + s_rows + bsz * n_meta) * TOPK
    n_runs = (t_pad // CMB_TILE) * n_experts
    n_rows = -(-(n_assign + n_runs * (RUN_ALIGN - 1) + n_experts * (MOE_BLK - 1)) // MOE_BLK) * MOE_BLK

    outs = {k: [] for k in ("mk", "mv", "wk", "wv", "ks", "vs", "cp", "hp", "cs", "hs")}
    for l in range(depth):
        idx = l // 2
        if l % 2 == 0:
            w_in = attn_w_in[idx].astype(BF16)
            qkv = _qkv_proj(xb, w_in, cos_t, sin_lo_t, sin_hi_t, q_w + kv_w)
            k_col, v_col = q_w // kv_w, q_w // kv_w + 1
            sink_rows = jnp.repeat(attn_sink[idx].astype(F32).reshape(n_kv, 1, GROUP), QBLK, axis=2)

            q_s = qkv[s_base:m_base, :q_w].reshape(dec_b, dec_s, q_w)
            q_s = jnp.pad(q_s, ((0, 0), (0, QBLK - dec_s), (0, 0))).reshape(dec_b * QBLK, q_w)
            k_new = qkv[s_base:m_base, q_w:q_w + kv_w].reshape(dec_b, dec_s, kv_w)
            v_new = qkv[s_base:m_base, q_w + kv_w:].reshape(dec_b, dec_s, kv_w)
            pad_k = jnp.zeros((dec_b, 3 * QBLK - n_keys_s, kv_w), F32)
            k_s = jnp.concatenate([cache_meta_k[idx].reshape(dec_b, n_meta, kv_w).astype(F32),
                                   cache_win_k[idx].reshape(dec_b, WINDOW, kv_w).astype(F32), k_new, pad_k], 1)
            v_s = jnp.concatenate([cache_meta_v[idx].reshape(dec_b, n_meta, kv_w).astype(F32),
                                   cache_win_v[idx].reshape(dec_b, WINDOW, kv_w).astype(F32), v_new, pad_k], 1)
            kv_s = jnp.concatenate([k_s, v_s], axis=2).reshape(dec_b * 3 * QBLK, 2 * kv_w)
            o_s = _attention(q_s, kv_s, dec_b, lambda i: i,
                             [lambda i: 3 * i, lambda i: 3 * i + 1, lambda i: 3 * i + 2], bias_s, lambda i: 0,
                             sink_rows, jnp.zeros((dec_b * QBLK, q_w), BF16), lambda i: i, q_w, kv_w, 0, 1,
                             "attn_sample")
            o_s = o_s.reshape(dec_b, QBLK, q_w)[:, :dec_s].reshape(s_rows, q_w)

            steps = nqb + 1
            is_meta = lambda i: (i % steps) == nqb
            bidx = lambda i: i // steps
            pblk = lambda i: i % steps
            frame_blk = lambda i: bidx(i) * nqb + jnp.minimum(pblk(i), nqb - 1)
            meta_blk = lambda i: m_base // QBLK + bidx(i)
            q_map = lambda i: jnp.where(is_meta(i), meta_blk(i), frame_blk(i))
            prev_map = lambda i: bidx(i) * nqb + jnp.clip(pblk(i) - 1, 0, nqb - 1)
            bias_map = lambda i: jnp.where(is_meta(i), 2, jnp.minimum(pblk(i), 1))
            o_init = jnp.concatenate([jnp.zeros((t_real, q_w), BF16), o_s,
                                      jnp.zeros((t_pad - m_base, q_w), BF16)], 0)
            o = _attention(qkv, qkv, bsz * steps, q_map, [prev_map, frame_blk, meta_blk], bias_p, bias_map,
                           sink_rows, o_init, q_map, q_w, kv_w, k_col, v_col, "attn_prompt")

            kp = qkv[:t_real, q_w:q_w + kv_w].reshape(bsz, seq, n_kv, HEAD_DIM)
            vp = qkv[:t_real, q_w + kv_w:].reshape(bsz, seq, n_kv, HEAD_DIM)
            km = qkv[m_base:, q_w:q_w + kv_w].reshape(bsz, META_BLK, n_kv, HEAD_DIM)[:, :n_meta]
            vm = qkv[m_base:, q_w + kv_w:].reshape(bsz, META_BLK, n_kv, HEAD_DIM)[:, :n_meta]
            outs["mk"].append(km); outs["mv"].append(vm)
            outs["wk"].append(kp[:, -WINDOW:]); outs["wv"].append(vp[:, -WINDOW:])
            outs["ks"].append(k_new.reshape(dec_b, dec_s, n_kv, HEAD_DIM))
            outs["vs"].append(v_new.reshape(dec_b, dec_s, n_kv, HEAD_DIM))
            w_out = attn_w_out[idx].astype(BF16)
        else:
            u = _matmul(xb, lru_w_in[idx].astype(BF16), 2048, "lru_in_proj")
            sp = jax.nn.softplus(-lru_lambda[idx].astype(F32)).reshape(1, c_rnn)
            wts = (lru_conv_w[idx].astype(F32), lru_conv_b[idx].astype(F32).reshape(1, c_rnn),
                   lru_gate_a_w[idx].astype(BF16), lru_gate_a_b[idx].astype(F32).reshape(1, c_rnn),
                   lru_gate_x_w[idx].astype(BF16), lru_gate_x_b[idx].astype(F32).reshape(1, c_rnn), sp)
            zc = jnp.zeros((bsz, CONV_W - 1, c_rnn), F32)
            zh = jnp.zeros((bsz, 1, c_rnn), F32)
            y_m, c_m, h_m = _lru_seq(u, bsz, 1, META_BLK, m_base, META_BLK, n_meta,
                                     jnp.zeros((bsz * META_BLK, c_rnn), BF16), 0, zc, zh, wts)
            y_s, c_s, h_s = _lru_seq(u, dec_b, 1, dec_s, s_base, dec_s, dec_s,
                                     jnp.zeros((s_rows, c_rnn), BF16), 0,
                                     state_conv[idx].astype(F32), state_h[idx].astype(F32).reshape(dec_b, 1, c_rnn),
                                     wts)
            o_init = jnp.concatenate([jnp.zeros((t_real, c_rnn), BF16), y_s, y_m], 0)
            o, c_p, h_p = _lru_seq(u, bsz, seq // LRU_TILE, LRU_TILE, 0, seq, LRU_TILE, o_init, 0,
                                   c_m, h_m, wts)
            outs["cp"].append(c_p); outs["hp"].append(h_p.reshape(bsz, c_rnn))
            outs["cs"].append(c_s); outs["hs"].append(h_s.reshape(dec_b, c_rnn))
            w_out = lru_w_out[idx].astype(BF16)

        ne_pad = -(-n_experts // LANES) * LANES
        wr = jnp.pad(moe_router_w[l].astype(F32), ((0, 0), (0, ne_pad - n_experts)))
        wr_hi, wr_lo = _split_bf16(wr)
        x1, x1b, logits = _proj_ln(o, w_out, x, ln_g[l, 0].astype(F32).reshape(1, d),
                                   ln_b[l, 0].astype(F32).reshape(1, d), wr_hi, wr_lo, alpha)

        eidx, gate, member = _route(logits[:, :n_experts], moe_router_bias[l], valid, n_experts)
        (loc, loc_t, tile_chunks, chunk_buf, chunk_hbm, zero_start, zero_chunks, block_expert,
         n_used) = _dispatch_plan(eidx, member, valid, n_experts, n_rows)
        x_sorted = _dispatch(x1b, loc_t, tile_chunks, chunk_buf, chunk_hbm, zero_start, zero_chunks,
                             n_rows, n_experts)
        y_sorted = _moe_experts(x_sorted, moe_w1, moe_w3, moe_w2, l, block_expert, n_used)

        w13 = jnp.concatenate([moe_shared_w1[l], moe_shared_w3[l]], axis=1).astype(BF16)
        x, xb = _combine_ln(x1b, x1, loc, gate, y_sorted, tile_chunks, chunk_buf, chunk_hbm, w13,
                            moe_shared_w2[l].astype(BF16), ln_g[l, 1].astype(F32).reshape(1, d),
                            ln_b[l, 1].astype(F32).reshape(1, d), alpha, n_experts)

    y_prompt = x[:t_real].reshape(bsz, seq, d)
    y_sample = x[s_base:m_base].reshape(dec_b, dec_s, d)
    st = lambda k: jnp.stack(outs[k])
    return (y_prompt, y_sample, st("mk"), st("mv"), st("wk"), st("wv"), st("ks"), st("vs"),
            st("cp"), st("hp"), st("cs"), st("hs"))
```

```python
import functools
import math

import numpy as np
import jax
import jax.numpy as jnp
from jax import lax
from jax.experimental import pallas as pl
from jax.experimental.pallas import tpu as pltpu

F32 = jnp.float32
BF16 = jnp.bfloat16

PAST_LEN = 1024
CHUNK = 64
WINDOW = 128
HEAD_DIM = 64
GROUP = 8
ROT_DIM = HEAD_DIM // 4
ROPE_THETA = 500000.0
ATTN_SCALE = HEAD_DIM ** -0.5
N_LRU_BLOCKS = 8
CONV_W = 4
LRU_C = 8.0
N_GROUPS = 8
TOPK_GROUPS = 4
TOPK = 8
ROUTED_SCALE = 2.5
LN_EPS = 1e-5
NEG_INF = -1e30

LANES = 128
SUBLANES = 8
QBLK = 2 * CHUNK
META_BLK = 128
ROW_TILE = 512
PROJ_TILE = 256
LRU_TILE = 256
SCAN_LANES = 512
MOE_BLK = 512
CMB_TILE = 256
RUN_ALIGN = SUBLANES
VMEM_LIMIT = 56 * 1024 * 1024


def _cparams(*sem):
    return pltpu.CompilerParams(dimension_semantics=sem, vmem_limit_bytes=VMEM_LIMIT)


def _mm_kernel(x_ref, w_ref, o_ref):
    o_ref[...] = jnp.dot(x_ref[...], w_ref[...], preferred_element_type=F32)


def _matmul(x, w, tn, name):
    m, k = x.shape
    n = w.shape[1]
    tm = ROW_TILE
    return pl.pallas_call(
        _mm_kernel,
        grid=(n // tn, m // tm),
        in_specs=[pl.BlockSpec((tm, k), lambda j, i: (i, 0)),
                  pl.BlockSpec((k, tn), lambda j, i: (0, j))],
        out_specs=pl.BlockSpec((tm, tn), lambda j, i: (i, j)),
        out_shape=jax.ShapeDtypeStruct((m, n), F32),
        compiler_params=_cparams("parallel", "parallel"),
        name=name,
    )(x, w)


def _qkv_kernel(x_ref, w_ref, c_ref, s1_ref, s2_ref, o_ref, *, tn, rope_cols):
    j = pl.program_id(0)
    acc = jnp.dot(x_ref[...], w_ref[...], preferred_element_type=F32)
    c, s1, s2 = c_ref[...], s1_ref[...], s2_ref[...]
    lane = lax.broadcasted_iota(jnp.int32, (acc.shape[0], LANES), 1)
    for g in range(tn // LANES):
        a = acc[:, g * LANES:(g + 1) * LANES]
        lo = pltpu.roll(a, ROT_DIM // 2, 1)
        hi = pltpu.roll(a, LANES - ROT_DIM // 2, 1)
        roped = a * c + lo * s1 + hi * s2
        col = j * tn + g * LANES + lane
        o_ref[:, g * LANES:(g + 1) * LANES] = jnp.where(col < rope_cols, roped, a)


def _qkv_proj(xb, w, cos_t, sin_lo_t, sin_hi_t, rope_cols):
    m, k = xb.shape
    n = w.shape[1]
    tm = ROW_TILE
    tn = 2560 if n % 2560 == 0 else 256
    assert n % tn == 0 and m % tm == 0
    tab = pl.BlockSpec((tm, LANES), lambda j, i: (i, 0))
    return pl.pallas_call(
        functools.partial(_qkv_kernel, tn=tn, rope_cols=rope_cols),
        grid=(n // tn, m // tm),
        in_specs=[pl.BlockSpec((tm, k), lambda j, i: (i, 0)),
                  pl.BlockSpec((k, tn), lambda j, i: (0, j)),
                  tab, tab, tab],
        out_specs=pl.BlockSpec((tm, tn), lambda j, i: (i, j)),
        out_shape=jax.ShapeDtypeStruct((m, n), F32),
        compiler_params=_cparams("parallel", "parallel"),
        name="qkv_rope",
    )(xb, w, cos_t, sin_lo_t, sin_hi_t)


def _layer_norm_rows(z, g, b):
    mu = jnp.mean(z, -1, keepdims=True)
    d = z - mu
    var = jnp.mean(d * d, -1, keepdims=True)
    return d * lax.rsqrt(var + LN_EPS) * g + b


def _proj_ln_kernel(a_ref, w_ref, x_ref, g_ref, b_ref, wrh_ref, wrl_ref, o_ref, ob_ref, lg_ref, *, alpha):
    acc = jnp.dot(a_ref[...], w_ref[...], preferred_element_type=F32)
    y = _layer_norm_rows(alpha * x_ref[...] + acc, g_ref[...], b_ref[...])
    o_ref[...] = y
    yb = y.astype(BF16)
    ob_ref[...] = yb
    ylo = (y - yb.astype(F32)).astype(BF16)
    wrh = wrh_ref[...]
    lg_ref[...] = (jnp.dot(yb, wrh, preferred_element_type=F32)
                   + jnp.dot(ylo, wrh, preferred_element_type=F32)
                   + jnp.dot(yb, wrl_ref[...], preferred_element_type=F32))


def _proj_ln(a, w, x, g, b, wr_hi, wr_lo, alpha):
    m, k = a.shape
    d = w.shape[1]
    ne = wr_hi.shape[1]
    tm = PROJ_TILE
    row = lambda i: (i, 0)
    fix = lambda i: (0, 0)
    return pl.pallas_call(
        functools.partial(_proj_ln_kernel, alpha=alpha),
        grid=(m // tm,),
        in_specs=[pl.BlockSpec((tm, k), row), pl.BlockSpec((k, d), fix), pl.BlockSpec((tm, d), row),
                  pl.BlockSpec((1, d), fix), pl.BlockSpec((1, d), fix),
                  pl.BlockSpec((d, ne), fix), pl.BlockSpec((d, ne), fix)],
        out_specs=[pl.BlockSpec((tm, d), row), pl.BlockSpec((tm, d), row), pl.BlockSpec((tm, ne), row)],
        out_shape=[jax.ShapeDtypeStruct((m, d), F32), jax.ShapeDtypeStruct((m, d), BF16),
                   jax.ShapeDtypeStruct((m, ne), F32)],
        compiler_params=_cparams("parallel"),
        name="proj_ln_router",
    )(a, w, x, g, b, wr_hi, wr_lo)


def _sigmoid(x):
    return 0.5 * (jnp.tanh(0.5 * x) + 1.0)


def _silu(x):
    return x * _sigmoid(x)


COMBINE_ROWS = 256
WAIT_GROUPS = (32, 4, 1)
CHUNK_LIST = 512


def _wait_chunks(make_copy, n):
    for g in WAIT_GROUPS:
        cnt = n // g
        n = n - cnt * g
        lax.fori_loop(0, cnt, lambda j, c, g=g: (make_copy(g * RUN_ALIGN).wait(), c)[1], 0)


def _combine_ln_kernel(tot_ref, buf_rows_ref, hbm_rows_ref, nbuf_rows_ref, nhbm_rows_ref, xb_ref, x_ref, loc_ref,
                       gate_ref, y_hbm, w13_ref, w2_ref, g_ref, b_ref, o_ref, ob_ref, ybuf, sem, *, alpha, dh):
    i = pl.program_id(0)
    n_tiles = pl.num_programs(0)
    _, rows, hw = ybuf.shape
    slot = i % 2

    def fetch_tile(s, n, brow_ref, hrow_ref):
        def clear(j, carry):
            ybuf[s, pl.ds(pl.multiple_of(j * RUN_ALIGN, RUN_ALIGN), RUN_ALIGN), :] = jnp.zeros(
                (RUN_ALIGN, hw), ybuf.dtype)
            return carry

        lax.fori_loop(n, rows // RUN_ALIGN, clear, 0)

        def fetch(c, carry):
            src = pl.multiple_of(hrow_ref[c], RUN_ALIGN)
            dst = pl.multiple_of(brow_ref[c], RUN_ALIGN)
            pltpu.make_async_copy(y_hbm.at[pl.ds(src, RUN_ALIGN)], ybuf.at[s, pl.ds(dst, RUN_ALIGN)],
                                  sem.at[s]).start()
            return carry

        lax.fori_loop(0, n, fetch, 0)

    @pl.when(i == 0)
    def _():
        fetch_tile(slot, tot_ref[i], buf_rows_ref, hbm_rows_ref)

    @pl.when(i + 1 < n_tiles)
    def _():
        fetch_tile(1 - slot, tot_ref[jnp.minimum(i + 1, n_tiles - 1)], nbuf_rows_ref, nhbm_rows_ref)

    u = jnp.dot(xb_ref[...], w13_ref[...], preferred_element_type=F32)
    hs = (_silu(u[:, :dh]) * u[:, dh:]).astype(BF16)
    f = jnp.dot(hs, w2_ref[...], preferred_element_type=F32)

    _wait_chunks(lambda n: pltpu.make_async_copy(y_hbm.at[pl.ds(0, n)], ybuf.at[slot, pl.ds(0, n)], sem.at[slot]),
                 tot_ref[i])

    loc = loc_ref[...]
    gate = gate_ref[...]
    tm = loc.shape[0]
    r_lo = jnp.zeros((tm, hw), F32)
    r_hi = jnp.zeros((tm, hw), F32)
    for c0 in range(0, rows, COMBINE_ROWS):
        col = c0 + lax.broadcasted_iota(jnp.int32, (tm, COMBINE_ROWS), 1)
        sel = jnp.zeros((tm, COMBINE_ROWS), F32)
        for k in range(TOPK):
            sel = jnp.where(col == loc[:, k:k + 1], gate[:, k:k + 1], sel)
        sel = sel.astype(BF16)
        y_lo, y_hi = _unpack_halves(ybuf[slot, c0:c0 + COMBINE_ROWS, :])
        r_lo = r_lo + jnp.dot(sel, y_lo, preferred_element_type=F32)
        r_hi = r_hi + jnp.dot(sel, y_hi, preferred_element_type=F32)
    f = f + jnp.concatenate([r_lo, r_hi], axis=1)
    y = _layer_norm_rows(alpha * x_ref[...] + f, g_ref[...], b_ref[...])
    o_ref[...] = y
    ob_ref[...] = y.astype(BF16)


def _combine_ln(xb, x, loc, gate, y_sorted, tile_chunks, chunk_buf, chunk_hbm, w13, w2, g, b, alpha, n_experts):
    m, d = x.shape
    dh = w2.shape[0]
    tm = CMB_TILE
    buf_rows = -(-(tm * TOPK + n_experts * (RUN_ALIGN - 1)) // 256) * 256
    row = lambda i, *_: (i, 0)
    fix = lambda i, *_: (0, 0)
    n_tiles = m // tm
    lst = pl.BlockSpec((CHUNK_LIST,), lambda i, *_: (i,), memory_space=pltpu.SMEM)
    nxt = pl.BlockSpec((CHUNK_LIST,), lambda i, *_: (jnp.minimum(i + 1, n_tiles - 1),), memory_space=pltpu.SMEM)
    grid_spec = pltpu.PrefetchScalarGridSpec(
        num_scalar_prefetch=1,
        grid=(n_tiles,),
        in_specs=[lst, lst, nxt, nxt, pl.BlockSpec((tm, d), row), pl.BlockSpec((tm, d), row),
                  pl.BlockSpec((tm, TOPK), row),
                  pl.BlockSpec((tm, TOPK), row), pl.BlockSpec(memory_space=pl.ANY),
                  pl.BlockSpec((d, 2 * dh), fix), pl.BlockSpec((dh, d), fix),
                  pl.BlockSpec((1, d), fix), pl.BlockSpec((1, d), fix)],
        out_specs=[pl.BlockSpec((tm, d), row), pl.BlockSpec((tm, d), row)],
        scratch_shapes=[pltpu.VMEM((2, buf_rows, y_sorted.shape[1]), y_sorted.dtype),
                        pltpu.SemaphoreType.DMA((2,))],
    )
    return pl.pallas_call(
        functools.partial(_combine_ln_kernel, alpha=alpha, dh=dh),
        grid_spec=grid_spec,
        out_shape=[jax.ShapeDtypeStruct((m, d), F32), jax.ShapeDtypeStruct((m, d), BF16)],
        compiler_params=_cparams("arbitrary"),
        name="combine_ln",
    )(tile_chunks, chunk_buf, chunk_hbm, chunk_buf, chunk_hbm, xb, x, loc, gate, y_sorted, w13, w2, g, b)


def _attn_kernel(q_ref, k0_ref, k1_ref, k2_ref, v0_ref, v1_ref, v2_ref, bias_ref, sink_ref, init_ref, o_ref,
                 ot_ref, *, n_kv):
    del init_ref
    hd = HEAD_DIM
    qt = (q_ref[...] * ATTN_SCALE).T.astype(BF16)
    kb = jnp.concatenate([k0_ref[...], k1_ref[...], k2_ref[...]], 0).astype(BF16)
    vt = jnp.concatenate([v0_ref[...], v1_ref[...], v2_ref[...]], 0).T.astype(BF16)
    bias = bias_ref[0]
    bias = jnp.concatenate([bias] * GROUP, axis=1)
    for h in range(n_kv):
        kh = kb[:, h * hd:(h + 1) * hd]
        qth = jnp.concatenate(
            [qt[(h * GROUP + g) * hd:(h * GROUP + g + 1) * hd, :] for g in range(GROUP)], axis=1)
        s = jnp.dot(kh, qth, preferred_element_type=F32) + bias
        sink = sink_ref[h]
        m = jnp.maximum(jnp.max(s, axis=0, keepdims=True), sink)
        p = jnp.exp(s - m)
        den = jnp.sum(p, axis=0, keepdims=True) + jnp.exp(sink - m)
        ot = jnp.dot(vt[h * hd:(h + 1) * hd, :], p.astype(BF16), preferred_element_type=F32)
        ot = ot * (1.0 / den)
        for g in range(GROUP):
            r0 = (h * GROUP + g) * hd
            ot_ref[r0:r0 + hd, :] = ot[:, g * QBLK:(g + 1) * QBLK]
    o_ref[...] = ot_ref[...].T.astype(BF16)


def _attention(q_arr, k_arr, n_steps, q_map, kv_maps, bias, bias_map, sink_rows, init, out_map,
               q_w, kv_w, k_col, v_col, name):
    n_kv = kv_w // HEAD_DIM
    kspecs = [pl.BlockSpec((QBLK, kv_w), (lambda i, f=f: (f(i), k_col))) for f in kv_maps]
    vspecs = [pl.BlockSpec((QBLK, kv_w), (lambda i, f=f: (f(i), v_col))) for f in kv_maps]
    nk = 3 * QBLK
    return pl.pallas_call(
        functools.partial(_attn_kernel, n_kv=n_kv),
        grid=(n_steps,),
        in_specs=[pl.BlockSpec((QBLK, q_w), lambda i: (q_map(i), 0))] + kspecs + vspecs + [
            pl.BlockSpec((1, nk, QBLK), lambda i: (bias_map(i), 0, 0)),
            pl.BlockSpec((n_kv, 1, GROUP * QBLK), lambda i: (0, 0, 0)),
            pl.BlockSpec(memory_space=pl.ANY)],
        out_specs=pl.BlockSpec((QBLK, q_w), lambda i: (out_map(i), 0)),
        out_shape=jax.ShapeDtypeStruct(init.shape, BF16),
        scratch_shapes=[pltpu.VMEM((q_w, QBLK), F32)],
        input_output_aliases={9: 0},
        compiler_params=_cparams("parallel"),
        name=name,
    )(q_arr, k_arr, k_arr, k_arr, k_arr, k_arr, k_arr, bias, sink_rows, init)


def _gelu_tanh(x):
    c = math.sqrt(2.0 / math.pi)
    return x * (0.5 * (1.0 + jnp.tanh(c * (x + 0.044715 * (x * x * x)))))


def _lru_kernel(xb_ref, gb_ref, cprev_ref, hprev_ref, cw_ref, cb_ref, gaw_ref, gab_ref, gxw_ref, gxb_ref,
                sp_ref, init_ref, y_ref, cnew_ref, hnew_ref, xcat, a_scr, b_scr, hcar, *, tt, valid_last):
    del init_ref
    j = pl.program_id(1)
    c_rnn = xb_ref.shape[1]
    bw = c_rnn // N_LRU_BLOCKS
    tail = SUBLANES

    @pl.when(j == 0)
    def _():
        xcat[0:tail, :] = jnp.zeros((tail, c_rnn), F32)
        xcat[tail - (CONV_W - 1):tail, :] = cprev_ref[...]
        hcar[...] = jnp.broadcast_to(hprev_ref[...], (SUBLANES, c_rnn))

    xcat[tail:tail + tt, :] = xb_ref[...]
    cw = cw_ref[...]
    xc = cb_ref[...]
    for tap in range(CONV_W):
        off = tail - (CONV_W - 1) + tap
        xc = xc + xcat[off:off + tt, :] * cw[tap:tap + 1, :]
    cnew_ref[...] = xcat[tail + valid_last - (CONV_W - 1):tail + valid_last, :]
    xcat[0:tail, :] = xcat[tt:tt + tail, :]

    xcb = xc.astype(BF16)
    rs, gs = [], []
    for n in range(N_LRU_BLOCKS):
        xs = xcb[:, n * bw:(n + 1) * bw]
        rs.append(jnp.dot(xs, gaw_ref[n], preferred_element_type=F32))
        gs.append(jnp.dot(xs, gxw_ref[n], preferred_element_type=F32))
    r = _sigmoid(jnp.concatenate(rs, axis=1) + gab_ref[...])
    gi = _sigmoid(jnp.concatenate(gs, axis=1) + gxb_ref[...])
    log_a = (-LRU_C * r) * sp_ref[...]
    a = jnp.exp(log_a)
    a_scr[...] = a
    b_scr[...] = jnp.sqrt(-jnp.tanh(log_a) * (a * a + 1.0)) * (gi * xc)

    row = lax.broadcasted_iota(jnp.int32, (SUBLANES, SCAN_LANES), 0)
    for c in range(c_rnn // SCAN_LANES):
        cs = slice(c * SCAN_LANES, (c + 1) * SCAN_LANES)

        def body(i, carry, cs=cs):
            r0 = pl.multiple_of(i * SUBLANES, SUBLANES)
            av = a_scr[pl.ds(r0, SUBLANES), cs]
            bv = b_scr[pl.ds(r0, SUBLANES), cs]
            for k in (1, 2, 4):
                a_sh = jnp.where(row >= k, pltpu.roll(av, k, 0), 1.0)
                b_sh = jnp.where(row >= k, pltpu.roll(bv, k, 0), 0.0)
                bv = av * b_sh + bv
                av = av * a_sh
            hv = av * carry + bv
            b_scr[pl.ds(r0, SUBLANES), cs] = hv
            return jnp.broadcast_to(hv[SUBLANES - 1:SUBLANES, :], (SUBLANES, SCAN_LANES))

        hcar[:, cs] = lax.fori_loop(0, tt // SUBLANES, body, hcar[:, cs])

    h = b_scr[...]
    hnew_ref[...] = b_scr[valid_last - 1:valid_last, :]
    y_ref[...] = (h * _gelu_tanh(gb_ref[...])).astype(BF16)


def _lru_seq(u, n_seq, n_tiles, tt, row0, seq_stride, valid_last, init, out_row0, cprev, hprev, wts):
    c_rnn = u.shape[1] // 2
    cw, cb, gaw, gab, gxw, gxb, sp = wts
    b0, bs, ob0 = row0 // tt, seq_stride // tt, out_row0 // tt
    bw = c_rnn // N_LRU_BLOCKS
    fix2 = lambda s, j: (0, 0)
    fix3 = lambda s, j: (0, 0, 0)
    per_seq = lambda s, j: (s, 0, 0)
    return pl.pallas_call(
        functools.partial(_lru_kernel, tt=tt, valid_last=valid_last),
        grid=(n_seq, n_tiles),
        in_specs=[pl.BlockSpec((tt, c_rnn), lambda s, j: (b0 + s * bs + j, 0)),
                  pl.BlockSpec((tt, c_rnn), lambda s, j: (b0 + s * bs + j, 1)),
                  pl.BlockSpec((None, CONV_W - 1, c_rnn), per_seq),
                  pl.BlockSpec((None, 1, c_rnn), per_seq),
                  pl.BlockSpec((CONV_W, c_rnn), fix2), pl.BlockSpec((1, c_rnn), fix2),
                  pl.BlockSpec((N_LRU_BLOCKS, bw, bw), fix3), pl.BlockSpec((1, c_rnn), fix2),
                  pl.BlockSpec((N_LRU_BLOCKS, bw, bw), fix3), pl.BlockSpec((1, c_rnn), fix2),
                  pl.BlockSpec((1, c_rnn), fix2), pl.BlockSpec(memory_space=pl.ANY)],
        out_specs=[pl.BlockSpec((tt, c_rnn), lambda s, j: (ob0 + s * bs + j, 0)),
                   pl.BlockSpec((None, CONV_W - 1, c_rnn), per_seq),
                   pl.BlockSpec((None, 1, c_rnn), per_seq)],
        out_shape=[jax.ShapeDtypeStruct(init.shape, BF16),
                   jax.ShapeDtypeStruct((n_seq, CONV_W - 1, c_rnn), F32),
                   jax.ShapeDtypeStruct((n_seq, 1, c_rnn), F32)],
        scratch_shapes=[pltpu.VMEM((SUBLANES + tt, c_rnn), F32), pltpu.VMEM((tt, c_rnn), F32),
                        pltpu.VMEM((tt, c_rnn), F32), pltpu.VMEM((SUBLANES, c_rnn), F32)],
        input_output_aliases={11: 0},
        compiler_params=_cparams("parallel", "arbitrary"),
        name="rglru_seq",
    )(u, u, cprev, hprev, cw, cb, gaw, gab, gxw, gxb, sp, init)


def _pack_halves(y):
    h = y.shape[1] // 2
    lo = lax.bitcast_convert_type(y[:, :h].astype(BF16).astype(F32), jnp.uint32)
    hi = lax.bitcast_convert_type(y[:, h:].astype(BF16).astype(F32), jnp.uint32)
    return (lo >> 16) | (hi & jnp.uint32(0xFFFF0000))


def _unpack_halves(w):
    lo = lax.bitcast_convert_type(w << 16, F32).astype(BF16)
    hi = lax.bitcast_convert_type(w & jnp.uint32(0xFFFF0000), F32).astype(BF16)
    return lo, hi


MOE_SPLIT = 1
DISPATCH_ROWS = 128


def _dispatch_kernel(tot_ref, zs_ref, zc_ref, buf_rows_ref, hbm_rows_ref, x_ref, loct_ref, o_hbm, obuf, zbuf, sem,
                     zsem, *, n_experts):
    i = pl.program_id(0)
    _, rows, hw = obuf.shape
    tm = x_ref.shape[0]
    slot = i % 2

    def drain(s, n):
        _wait_chunks(lambda m: pltpu.make_async_copy(obuf.at[s, pl.ds(0, m)], o_hbm.at[pl.ds(0, m)], sem.at[s]), n)

    @pl.when(i >= 2)
    def _():
        drain(slot, tot_ref[jnp.maximum(i - 2, 0)])

    loct = loct_ref[...]
    x = x_ref[...]
    x_lo, x_hi = x[:, :hw], x[:, hw:]
    for r0 in range(0, rows, DISPATCH_ROWS):
        rowi = r0 + lax.broadcasted_iota(jnp.int32, (DISPATCH_ROWS, tm), 0)
        p = jnp.zeros((DISPATCH_ROWS, tm), F32)
        for k in range(TOPK):
            p = jnp.where(rowi == loct[k:k + 1, :], 1.0, p)
        pb = p.astype(BF16)
        lo = jnp.dot(pb, x_lo, preferred_element_type=F32)
        hi = jnp.dot(pb, x_hi, preferred_element_type=F32)
        obuf[slot, r0:r0 + DISPATCH_ROWS, :] = ((lax.bitcast_convert_type(lo, jnp.uint32) >> 16)
                                                | lax.bitcast_convert_type(hi, jnp.uint32))

    def send(c, carry):
        src = pl.multiple_of(buf_rows_ref[c], RUN_ALIGN)
        dst = pl.multiple_of(hbm_rows_ref[c], RUN_ALIGN)
        pltpu.make_async_copy(obuf.at[slot, pl.ds(src, RUN_ALIGN)], o_hbm.at[pl.ds(dst, RUN_ALIGN)],
                              sem.at[slot]).start()
        return carry

    lax.fori_loop(0, tot_ref[i], send, 0)

    @pl.when(i == pl.num_programs(0) - 1)
    def _():
        drain(slot, tot_ref[i])

        @pl.when(i >= 1)
        def _():
            drain(1 - slot, tot_ref[jnp.maximum(i - 1, 0)])

        zbuf[...] = jnp.zeros(zbuf.shape, zbuf.dtype)

        def zero_copy(dst_row):
            return pltpu.make_async_copy(zbuf, o_hbm.at[pl.ds(pl.multiple_of(dst_row, RUN_ALIGN), RUN_ALIGN)], zsem)

        def zero_expert(e, c):
            return lax.fori_loop(0, zc_ref[e], lambda j, c2: (zero_copy(zs_ref[e] + j * RUN_ALIGN).start(), c2)[1], c)

        def zero_wait(e, c):
            return lax.fori_loop(0, zc_ref[e], lambda j, c2: (zero_copy(0).wait(), c2)[1], c)

        lax.fori_loop(0, n_experts, zero_expert, 0)
        lax.fori_loop(0, n_experts, zero_wait, 0)


def _dispatch(xb, loc_t, tile_chunks, chunk_buf, chunk_hbm, zero_start, zero_chunks, n_rows, n_experts):
    m, d = xb.shape
    tm = CMB_TILE
    buf_rows = -(-(tm * TOPK + n_experts * (RUN_ALIGN - 1)) // DISPATCH_ROWS) * DISPATCH_ROWS
    lst = pl.BlockSpec((CHUNK_LIST,), lambda i, *_: (i,), memory_space=pltpu.SMEM)
    grid_spec = pltpu.PrefetchScalarGridSpec(
        num_scalar_prefetch=3,
        grid=(m // tm,),
        in_specs=[lst, lst, pl.BlockSpec((tm, d), lambda i, *_: (i, 0)),
                  pl.BlockSpec((None, TOPK, tm), lambda i, *_: (i, 0, 0))],
        out_specs=pl.BlockSpec(memory_space=pl.ANY),
        scratch_shapes=[pltpu.VMEM((2, buf_rows, d // 2), jnp.uint32), pltpu.VMEM((RUN_ALIGN, d // 2), jnp.uint32),
                        pltpu.SemaphoreType.DMA((2,)), pltpu.SemaphoreType.DMA(())],
    )
    return pl.pallas_call(
        functools.partial(_dispatch_kernel, n_experts=n_experts),
        grid_spec=grid_spec,
        out_shape=jax.ShapeDtypeStruct((n_rows, d // 2), jnp.uint32),
        compiler_params=_cparams("arbitrary"),
        name="moe_dispatch",
    )(tile_chunks, zero_start, zero_chunks, chunk_buf, chunk_hbm, xb, loc_t)


def _moe_kernel(be_ref, nu_ref, x_ref, w1_ref, w3_ref, w2_ref, o_ref, w1b, w3b, w2b):
    i = pl.program_id(0)

    @pl.when(i < nu_ref[0])
    def _():
        prev = be_ref[jnp.maximum(i - 1, 0)]

        @pl.when(jnp.logical_or(i == 0, be_ref[i] != prev))
        def _():
            w1b[...] = w1_ref[...].astype(BF16)
            w3b[...] = w3_ref[...].astype(BF16)
            w2b[...] = w2_ref[...].astype(BF16)

        hk = x_ref.shape[1]
        sub = x_ref.shape[0] // MOE_SPLIT
        for s in range(MOE_SPLIT):
            xlo, xhi = _unpack_halves(x_ref[s * sub:(s + 1) * sub, :])
            a = (jnp.dot(xlo, w1b[0:hk, :], preferred_element_type=F32)
                 + jnp.dot(xhi, w1b[hk:2 * hk, :], preferred_element_type=F32))
            b = (jnp.dot(xlo, w3b[0:hk, :], preferred_element_type=F32)
                 + jnp.dot(xhi, w3b[hk:2 * hk, :], preferred_element_type=F32))
            h = (_silu(a) * b).astype(BF16)
            o_ref[s * sub:(s + 1) * sub, :] = _pack_halves(jnp.dot(h, w2b[...], preferred_element_type=F32))


def _moe_experts(x_sorted, w1, w3, w2, layer, block_expert, n_used):
    n_rows, hw = x_sorted.shape
    d, de = w1.shape[2], w1.shape[3]
    nb = n_rows // MOE_BLK
    blk = lambda i, be, nu: (jnp.minimum(i, nu[0] - 1), 0)
    wmap = lambda i, be, nu: (layer, be[jnp.minimum(i, nu[0] - 1)], 0, 0)
    grid_spec = pltpu.PrefetchScalarGridSpec(
        num_scalar_prefetch=2,
        grid=(nb,),
        in_specs=[pl.BlockSpec((MOE_BLK, hw), blk),
                  pl.BlockSpec((None, None, d, de), wmap), pl.BlockSpec((None, None, d, de), wmap),
                  pl.BlockSpec((None, None, de, d), wmap)],
        out_specs=pl.BlockSpec((MOE_BLK, d // 2), blk),
        scratch_shapes=[pltpu.VMEM((d, de), BF16), pltpu.VMEM((d, de), BF16), pltpu.VMEM((de, d), BF16)],
    )
    return pl.pallas_call(
        _moe_kernel,
        grid_spec=grid_spec,
        out_shape=jax.ShapeDtypeStruct((n_rows, d // 2), jnp.uint32),
        compiler_params=_cparams("arbitrary"),
        name="moe_experts",
    )(block_expert, n_used, x_sorted, w1, w3, w2)


def _route(logits, router_bias, valid, n_experts):
    t = logits.shape[0]
    per = n_experts // N_GROUPS
    scores = jax.nn.sigmoid(logits)
    biased = scores + router_bias.astype(F32)
    grp = biased.reshape(t, N_GROUPS, per)
    m1 = jnp.max(grp, -1, keepdims=True)
    is_max = grp == m1
    first = is_max & (jnp.cumsum(is_max.astype(jnp.int32), -1) == 1)
    m2 = jnp.max(jnp.where(first, -jnp.inf, grp), -1)
    grp_score = m1[..., 0] + m2

    def rank(v):
        n = v.shape[-1]
        idx = jnp.arange(n)
        vi, vj = v[:, :, None], v[:, None, :]
        beaten = (vj > vi) | ((vj == vi) & (idx[None, None, :] < idx[None, :, None]))
        return jnp.sum(beaten.astype(jnp.int32), -1)

    gsel = rank(grp_score) < TOPK_GROUPS
    emask = jnp.repeat(gsel, per, axis=-1)
    e_rank = rank(jnp.where(emask, biased, NEG_INF))
    slot = e_rank[:, None, :] == jnp.arange(TOPK)[None, :, None]
    eidx = jnp.sum(jnp.where(slot, jnp.arange(n_experts)[None, None, :], 0), -1)
    g = jnp.sum(jnp.where(slot, scores[:, None, :], 0.0), -1)
    g = g / (jnp.sum(g, -1, keepdims=True) + 1e-20) * ROUTED_SCALE
    g = jnp.where(valid[:, None], g, 0.0)
    member = ((e_rank < TOPK) & valid[:, None]).astype(jnp.int32)
    return eidx.astype(jnp.int32), g, member


def _dispatch_plan(eidx, member, valid, n_experts, n_rows):
    t = eidx.shape[0]
    nt = t // CMB_TILE
    i32 = jnp.int32
    m3 = member.reshape(nt, CMB_TILE, n_experts)
    cnt = jnp.sum(m3, axis=1)
    rlen = (cnt + RUN_ALIGN - 1) // RUN_ALIGN * RUN_ALIGN
    tri = (jnp.arange(CMB_TILE)[:, None] > jnp.arange(CMB_TILE)[None, :]).astype(BF16)
    rank = jnp.einsum("ts,nse->nte", tri, m3.astype(BF16), preferred_element_type=F32)
    rank = rank.astype(i32).reshape(t, n_experts)
    seg = jnp.sum(rlen, axis=0)
    padded = (seg + MOE_BLK - 1) // MOE_BLK * MOE_BLK
    pad_end = jnp.cumsum(padded)
    pad_start = pad_end - padded
    run_off = jnp.cumsum(rlen, axis=0) - rlen
    run_start = pad_start[None, :] + run_off
    lbase = jnp.cumsum(rlen, axis=1) - rlen
    pos = (jnp.repeat(lbase, CMB_TILE, axis=0) + rank)
    hit = eidx[:, :, None] == jnp.arange(n_experts, dtype=i32)[None, None, :]
    loc = jnp.sum(jnp.where(hit, pos[:, None, :], 0), axis=-1)
    loc = jnp.where(valid[:, None], loc, -1).astype(i32)
    loc_t = loc.reshape(nt, CMB_TILE, TOPK).transpose(0, 2, 1)
    nb = n_rows // MOE_BLK
    blk_row = jnp.arange(nb, dtype=i32) * MOE_BLK
    block_expert = jnp.minimum(jnp.sum((pad_end[None, :] <= blk_row[:, None]).astype(i32), axis=1),
                               n_experts - 1).astype(i32)
    n_used = jnp.maximum(pad_end[-1] // MOE_BLK, 1).astype(i32).reshape(1)

    n_chunks = (rlen // RUN_ALIGN).astype(i32)
    cum = jnp.cumsum(n_chunks, axis=1)
    c = jnp.arange(CHUNK_LIST, dtype=i32)
    e_c = jnp.minimum(jnp.sum((cum[:, None, :] <= c[None, :, None]).astype(i32), axis=-1), n_experts - 1)
    own = e_c[:, :, None] == jnp.arange(n_experts, dtype=i32)[None, None, :]
    pick = lambda tab: jnp.sum(jnp.where(own, tab[:, None, :], 0), axis=-1)
    off = (c[None, :] - pick(cum - n_chunks)) * RUN_ALIGN
    chunk_buf = (pick(lbase) + off).reshape(-1).astype(i32)
    chunk_hbm = (pick(run_start) + off).reshape(-1).astype(i32)
    return (loc, loc_t, cum[:, -1].astype(i32), chunk_buf, chunk_hbm, (pad_start + seg).astype(i32),
            ((padded - seg) // RUN_ALIGN).astype(i32), block_expert, n_used)


def _split_bf16(w):
    hi = w.astype(BF16)
    lo = (w - hi.astype(F32)).astype(BF16)
    return hi, lo


def kernel(x_prompt, x_sample, cache_meta_k, cache_meta_v, cache_win_k, cache_win_v, state_conv, state_h, meta_tokens, ln_g, ln_b, attn_w_in, attn_w_out, attn_sink, lru_w_in, lru_conv_w, lru_conv_b, lru_gate_a_w, lru_gate_a_b, lru_gate_x_w, lru_gate_x_b, lru_lambda, lru_w_out, moe_router_w, moe_router_bias, moe_w1, moe_w3, moe_w2, moe_shared_w1, moe_shared_w3, moe_shared_w2):
    bsz, seq, d = x_prompt.shape
    dec_b, dec_s, _ = x_sample.shape
    n_meta = meta_tokens.shape[0]
    depth = ln_g.shape[0]
    n_experts = moe_router_w.shape[2]
    past_len = PAST_LEN
    n_heads = d // HEAD_DIM
    n_kv = n_heads // GROUP
    q_w, kv_w = n_heads * HEAD_DIM, n_kv * HEAD_DIM
    c_rnn = lru_w_in.shape[2] // 2
    alpha = (2 * depth) ** 0.25

    t_real = bsz * seq
    s_base, s_rows = t_real, dec_b * dec_s
    m_base = s_base + s_rows
    t_pad = m_base + bsz * META_BLK
    assert seq % LRU_TILE == 0 and seq % QBLK == 0 and s_rows % QBLK == 0 and t_pad % ROW_TILE == 0
    assert n_meta <= META_BLK and dec_s <= QBLK and q_w == d and kv_w % LANES == 0
    assert (CMB_TILE * TOPK + n_experts * (RUN_ALIGN - 1)) // RUN_ALIGN <= CHUNK_LIST and t_pad % CMB_TILE == 0
    assert cache_win_k.shape[2] == WINDOW and WINDOW == QBLK and n_meta + WINDOW + dec_s <= 3 * QBLK
    nqb = seq // QBLK

    pos = np.zeros((t_pad,), np.float32)
    pos[:t_real] = np.tile(np.arange(seq) + n_meta, bsz)
    pos[s_base:m_base] = np.tile(past_len + n_meta + np.arange(dec_s), dec_b)
    valid_np = np.zeros((t_pad,), bool)
    valid_np[:m_base] = True
    for b in range(bsz):
        pos[m_base + b * META_BLK:m_base + b * META_BLK + n_meta] = np.arange(n_meta)
        valid_np[m_base + b * META_BLK:m_base + b * META_BLK + n_meta] = True
    valid = jnp.asarray(valid_np)

    half = ROT_DIM // 2
    freqs = ROPE_THETA ** (-jnp.arange(0, ROT_DIM, 2, dtype=F32) / ROT_DIM)
    ang = jnp.asarray(pos)[:, None] * freqs[None, :]
    cos, sin = jnp.cos(ang), jnp.sin(ang)
    ones = jnp.ones((t_pad, HEAD_DIM - ROT_DIM), F32)
    zeros_h = jnp.zeros((t_pad, half), F32)
    zeros_r = jnp.zeros((t_pad, HEAD_DIM - ROT_DIM), F32)
    reps = LANES // HEAD_DIM
    cos_t = jnp.tile(jnp.concatenate([cos, cos, ones], 1), (1, reps))
    sin_lo_t = jnp.tile(jnp.concatenate([zeros_h, sin, zeros_r], 1), (1, reps))
    sin_hi_t = jnp.tile(jnp.concatenate([-sin, zeros_h, zeros_r], 1), (1, reps))

    kj = np.arange(3 * QBLK)[:, None]
    qc = np.arange(QBLK)[None, :] // CHUNK
    key_chunk = np.where(kj < 2 * QBLK, kj // CHUNK - 2, 0)
    band_ok = (kj < 2 * QBLK) & (qc - key_chunk >= 0) & (qc - key_chunk <= WINDOW // CHUNK)
    meta_ok = (kj >= 2 * QBLK) & (kj < 2 * QBLK + n_meta)
    later = band_ok | meta_ok
    first = (band_ok & (kj >= QBLK)) | meta_ok
    only_meta = np.broadcast_to(meta_ok, later.shape)
    bias_p = jnp.asarray(np.where(np.stack([first, later, only_meta]), 0.0, NEG_INF).astype(np.float32))
    n_keys_s = n_meta + WINDOW + dec_s
    bias_s = jnp.asarray(np.where(np.broadcast_to(kj < n_keys_s, later.shape), 0.0, NEG_INF)
                         .astype(np.float32))[None]

    meta_rows = jnp.concatenate([meta_tokens.astype(F32), jnp.zeros((META_BLK - n_meta, d), F32)], 0)
    x = jnp.concatenate([x_prompt.reshape(t_real, d), x_sample.reshape(s_rows, d),
                         jnp.tile(meta_rows, (bsz, 1))], 0)
    xb = x.astype(BF16)

    n_assign = (t_real + s_rows + bsz * n_meta) * TOPK
    n_runs = (t_pad // CMB_TILE) * n_experts
    n_rows = -(-(n_assign + n_runs * (RUN_ALIGN - 1) + n_experts * (MOE_BLK - 1)) // MOE_BLK) * MOE_BLK

    outs = {k: [] for k in ("mk", "mv", "wk", "wv", "ks", "vs", "cp", "hp", "cs", "hs")}
    for l in range(depth):
        idx = l // 2
        if l % 2 == 0:
            w_in = attn_w_in[idx].astype(BF16)
            qkv = _qkv_proj(xb, w_in, cos_t, sin_lo_t, sin_hi_t, q_w + kv_w)
            k_col, v_col = q_w // kv_w, q_w // kv_w + 1
            sink_rows = jnp.repeat(attn_sink[idx].astype(F32).reshape(n_kv, 1, GROUP), QBLK, axis=2)

            q_s = qkv[s_base:m_base, :q_w].reshape(dec_b, dec_s, q_w)
            q_s = jnp.pad(q_s, ((0, 0), (0, QBLK - dec_s), (0, 0))).reshape(dec_b * QBLK, q_w)
            k_new = qkv[s_base:m_base, q_w:q_w + kv_w].reshape(dec_b, dec_s, kv_w)
            v_new = qkv[s_base:m_base, q_w + kv_w:].reshape(dec_b, dec_s, kv_w)
            pad_k = jnp.zeros((dec_b, 3 * QBLK - n_keys_s, kv_w), F32)
            k_s = jnp.concatenate([cache_meta_k[idx].reshape(dec_b, n_meta, kv_w).astype(F32),
                                   cache_win_k[idx].reshape(dec_b, WINDOW, kv_w).astype(F32), k_new, pad_k], 1)
            v_s = jnp.concatenate([cache_meta_v[idx].reshape(dec_b, n_meta, kv_w).astype(F32),
                                   cache_win_v[idx].reshape(dec_b, WINDOW, kv_w).astype(F32), v_new, pad_k], 1)
            kv_s = jnp.concatenate([k_s, v_s], axis=2).reshape(dec_b * 3 * QBLK, 2 * kv_w)
            o_s = _attention(q_s, kv_s, dec_b, lambda i: i,
                             [lambda i: 3 * i, lambda i: 3 * i + 1, lambda i: 3 * i + 2], bias_s, lambda i: 0,
                             sink_rows, jnp.zeros((dec_b * QBLK, q_w), BF16), lambda i: i, q_w, kv_w, 0, 1,
                             "attn_sample")
            o_s = o_s.reshape(dec_b, QBLK, q_w)[:, :dec_s].reshape(s_rows, q_w)

            steps = nqb + 1
            is_meta = lambda i: (i % steps) == nqb
            bidx = lambda i: i // steps
            pblk = lambda i: i % steps
            frame_blk = lambda i: bidx(i) * nqb + jnp.minimum(pblk(i), nqb - 1)
            meta_blk = lambda i: m_base // QBLK + bidx(i)
            q_map = lambda i: jnp.where(is_meta(i), meta_blk(i), frame_blk(i))
            prev_map = lambda i: bidx(i) * nqb + jnp.clip(pblk(i) - 1, 0, nqb - 1)
            bias_map = lambda i: jnp.where(is_meta(i), 2, jnp.minimum(pblk(i), 1))
            o_init = jnp.concatenate([jnp.zeros((t_real, q_w), BF16), o_s,
                                      jnp.zeros((t_pad - m_base, q_w), BF16)], 0)
            o = _attention(qkv, qkv, bsz * steps, q_map, [prev_map, frame_blk, meta_blk], bias_p, bias_map,
                           sink_rows, o_init, q_map, q_w, kv_w, k_col, v_col, "attn_prompt")

            kp = qkv[:t_real, q_w:q_w + kv_w].reshape(bsz, seq, n_kv, HEAD_DIM)
            vp = qkv[:t_real, q_w + kv_w:].reshape(bsz, seq, n_kv, HEAD_DIM)
            km = qkv[m_base:, q_w:q_w + kv_w].reshape(bsz, META_BLK, n_kv, HEAD_DIM)[:, :n_meta]
            vm = qkv[m_base:, q_w + kv_w:].reshape(bsz, META_BLK, n_kv, HEAD_DIM)[:, :n_meta]
            outs["mk"].append(km); outs["mv"].append(vm)
            outs["wk"].append(kp[:, -WINDOW:]); outs["wv"].append(vp[:, -WINDOW:])
            outs["ks"].append(k_new.reshape(dec_b, dec_s, n_kv, HEAD_DIM))
            outs["vs"].append(v_new.reshape(dec_b, dec_s, n_kv, HEAD_DIM))
            w_out = attn_w_out[idx].astype(BF16)
        else:
            u = _matmul(xb, lru_w_in[idx].astype(BF16), 2048, "lru_in_proj")
            sp = jax.nn.softplus(-lru_lambda[idx].astype(F32)).reshape(1, c_rnn)
            wts = (lru_conv_w[idx].astype(F32), lru_conv_b[idx].astype(F32).reshape(1, c_rnn),
                   lru_gate_a_w[idx].astype(BF16), lru_gate_a_b[idx].astype(F32).reshape(1, c_rnn),
                   lru_gate_x_w[idx].astype(BF16), lru_gate_x_b[idx].astype(F32).reshape(1, c_rnn), sp)
            zc = jnp.zeros((bsz, CONV_W - 1, c_rnn), F32)
            zh = jnp.zeros((bsz, 1, c_rnn), F32)
            y_m, c_m, h_m = _lru_seq(u, bsz, 1, META_BLK, m_base, META_BLK, n_meta,
                                     jnp.zeros((bsz * META_BLK, c_rnn), BF16), 0, zc, zh, wts)
            y_s, c_s, h_s = _lru_seq(u, dec_b, 1, dec_s, s_base, dec_s, dec_s,
                                     jnp.zeros((s_rows, c_rnn), BF16), 0,
                                     state_conv[idx].astype(F32), state_h[idx].astype(F32).reshape(dec_b, 1, c_rnn),
                                     wts)
            o_init = jnp.concatenate([jnp.zeros((t_real, c_rnn), BF16), y_s, y_m], 0)
            o, c_p, h_p = _lru_seq(u, bsz, seq // LRU_TILE, LRU_TILE, 0, seq, LRU_TILE, o_init, 0,
                                   c_m, h_m, wts)
            outs["cp"].append(c_p); outs["hp"].append(h_p.reshape(bsz, c_rnn))
            outs["cs"].append(c_s); outs["hs"].append(h_s.reshape(dec_b, c_rnn))
            w_out = lru_w_out[idx].astype(BF16)

        ne_pad = -(-n_experts // LANES) * LANES
        wr = jnp.pad(moe_router_w[l].astype(F32), ((0, 0), (0, ne_pad - n_experts)))
        wr_hi, wr_lo = _split_bf16(wr)
        x1, x1b, logits = _proj_ln(o, w_out, x, ln_g[l, 0].astype(F32).reshape(1, d),
                                   ln_b[l, 0].astype(F32).reshape(1, d), wr_hi, wr_lo, alpha)

        eidx, gate, member = _route(logits[:, :n_experts], moe_router_bias[l], valid, n_experts)
        (loc, loc_t, tile_chunks, chunk_buf, chunk_hbm, zero_start, zero_chunks, block_expert,
         n_used) = _dispatch_plan(eidx, member, valid, n_experts, n_rows)
        x_sorted = _dispatch(x1b, loc_t, tile_chunks, chunk_buf, chunk_hbm, zero_start, zero_chunks,
                             n_rows, n_experts)
        y_sorted = _moe_experts(x_sorted, moe_w1, moe_w3, moe_w2, l, block_expert, n_used)

        w13 = jnp.concatenate([moe_shared_w1[l], moe_shared_w3[l]], axis=1).astype(BF16)
        x, xb = _combine_ln(x1b, x1, loc, gate, y_sorted, tile_chunks, chunk_buf, chunk_hbm, w13,
                            moe_shared_w2[l].astype(BF16), ln_g[l, 1].astype(F32).reshape(1, d),
                            ln_b[l, 1].astype(F32).reshape(1, d), alpha, n_experts)

    y_prompt = x[:t_real].reshape(bsz, seq, d)
    y_sample = x[s_base:m_base].reshape(dec_b, dec_s, d)
    st = lambda k: jnp.stack(outs[k])
    return (y_prompt, y_sample, st("mk"), st("mv"), st("wk"), st("wv"), st("ks"), st("vs"),
            st("cp"), st("hp"), st("cs"), st("hs"))
```

```python
import functools
import math

import numpy as np
import jax
import jax.numpy as jnp
from jax import lax
from jax.experimental import pallas as pl
from jax.experimental.pallas import tpu as pltpu

F32 = jnp.float32
BF16 = jnp.bfloat16

PAST_LEN = 1024
CHUNK = 64
WINDOW = 128
HEAD_DIM = 64
GROUP = 8
ROT_DIM = HEAD_DIM // 4
ROPE_THETA = 500000.0
ATTN_SCALE = HEAD_DIM ** -0.5
N_LRU_BLOCKS = 8
CONV_W = 4
LRU_C = 8.0
N_GROUPS = 8
TOPK_GROUPS = 4
TOPK = 8
ROUTED_SCALE = 2.5
LN_EPS = 1e-5
NEG_INF = -1e30

LANES = 128
SUBLANES = 8
QBLK = 2 * CHUNK
META_BLK = 128
ROW_TILE = 512
PROJ_TILE = 256
LRU_TILE = 512
SCAN_LANES = 512
MOE_BLK = 512
CMB_TILE = 256
RUN_ALIGN = SUBLANES
VMEM_LIMIT = 56 * 1024 * 1024


def _cparams(*sem):
    return pltpu.CompilerParams(dimension_semantics=sem, vmem_limit_bytes=VMEM_LIMIT)


def _mm_kernel(x_ref, w_ref, o_ref):
    o_ref[...] = jnp.dot(x_ref[...], w_ref[...], preferred_element_type=F32)


def _matmul(x, w, tn, name):
    m, k = x.shape
    n = w.shape[1]
    tm = ROW_TILE
    return pl.pallas_call(
        _mm_kernel,
        grid=(n // tn, m // tm),
        in_specs=[pl.BlockSpec((tm, k), lambda j, i: (i, 0)),
                  pl.BlockSpec((k, tn), lambda j, i: (0, j))],
        out_specs=pl.BlockSpec((tm, tn), lambda j, i: (i, j)),
        out_shape=jax.ShapeDtypeStruct((m, n), F32),
        compiler_params=_cparams("parallel", "parallel"),
        name=name,
    )(x, w)


def _qkv_kernel(x_ref, w_ref, c_ref, s1_ref, s2_ref, o_ref, *, tn, rope_cols):
    j = pl.program_id(0)
    acc = jnp.dot(x_ref[...], w_ref[...], preferred_element_type=F32)
    c, s1, s2 = c_ref[...], s1_ref[...], s2_ref[...]
    lane = lax.broadcasted_iota(jnp.int32, (acc.shape[0], LANES), 1)
    for g in range(tn // LANES):
        a = acc[:, g * LANES:(g + 1) * LANES]
        lo = pltpu.roll(a, ROT_DIM // 2, 1)
        hi = pltpu.roll(a, LANES - ROT_DIM // 2, 1)
        roped = a * c + lo * s1 + hi * s2
        col = j * tn + g * LANES + lane
        o_ref[:, g * LANES:(g + 1) * LANES] = jnp.where(col < rope_cols, roped, a)


def _qkv_proj(xb, w, cos_t, sin_lo_t, sin_hi_t, rope_cols):
    m, k = xb.shape
    n = w.shape[1]
    tm = ROW_TILE
    tn = 2560 if n % 2560 == 0 else 256
    assert n % tn == 0 and m % tm == 0
    tab = pl.BlockSpec((tm, LANES), lambda j, i: (i, 0))
    return pl.pallas_call(
        functools.partial(_qkv_kernel, tn=tn, rope_cols=rope_cols),
        grid=(n // tn, m // tm),
        in_specs=[pl.BlockSpec((tm, k), lambda j, i: (i, 0)),
                  pl.BlockSpec((k, tn), lambda j, i: (0, j)),
                  tab, tab, tab],
        out_specs=pl.BlockSpec((tm, tn), lambda j, i: (i, j)),
        out_shape=jax.ShapeDtypeStruct((m, n), F32),
        compiler_params=_cparams("parallel", "parallel"),
        name="qkv_rope",
    )(xb, w, cos_t, sin_lo_t, sin_hi_t)


def _layer_norm_rows(z, g, b):
    mu = jnp.mean(z, -1, keepdims=True)
    d = z - mu
    var = jnp.mean(d * d, -1, keepdims=True)
    return d * lax.rsqrt(var + LN_EPS) * g + b


def _proj_ln_kernel(a_ref, w_ref, x_ref, g_ref, b_ref, wrh_ref, wrl_ref, o_ref, ob_ref, lg_ref, *, alpha):
    acc = jnp.dot(a_ref[...], w_ref[...], preferred_element_type=F32)
    y = _layer_norm_rows(alpha * x_ref[...] + acc, g_ref[...], b_ref[...])
    o_ref[...] = y
    yb = y.astype(BF16)
    ob_ref[...] = yb
    ylo = (y - yb.astype(F32)).astype(BF16)
    wrh = wrh_ref[...]
    lg_ref[...] = (jnp.dot(yb, wrh, preferred_element_type=F32)
                   + jnp.dot(ylo, wrh, preferred_element_type=F32)
                   + jnp.dot(yb, wrl_ref[...], preferred_element_type=F32))


def _proj_ln(a, w, x, g, b, wr_hi, wr_lo, alpha):
    m, k = a.shape
    d = w.shape[1]
    ne = wr_hi.shape[1]
    tm = PROJ_TILE
    row = lambda i: (i, 0)
    fix = lambda i: (0, 0)
    return pl.pallas_call(
        functools.partial(_proj_ln_kernel, alpha=alpha),
        grid=(m // tm,),
        in_specs=[pl.BlockSpec((tm, k), row), pl.BlockSpec((k, d), fix), pl.BlockSpec((tm, d), row),
                  pl.BlockSpec((1, d), fix), pl.BlockSpec((1, d), fix),
                  pl.BlockSpec((d, ne), fix), pl.BlockSpec((d, ne), fix)],
        out_specs=[pl.BlockSpec((tm, d), row), pl.BlockSpec((tm, d), row), pl.BlockSpec((tm, ne), row)],
        out_shape=[jax.ShapeDtypeStruct((m, d), F32), jax.ShapeDtypeStruct((m, d), BF16),
                   jax.ShapeDtypeStruct((m, ne), F32)],
        compiler_params=_cparams("parallel"),
        name="proj_ln_router",
    )(a, w, x, g, b, wr_hi, wr_lo)


def _sigmoid(x):
    return 0.5 * (jnp.tanh(0.5 * x) + 1.0)


def _silu(x):
    return x * _sigmoid(x)


COMBINE_ROWS = 256
WAIT_GROUPS = (32, 4, 1)
CHUNK_LIST = 512


def _wait_chunks(make_copy, n):
    for g in WAIT_GROUPS:
        cnt = n // g
        n = n - cnt * g
        lax.fori_loop(0, cnt, lambda j, c, g=g: (make_copy(g * RUN_ALIGN).wait(), c)[1], 0)


def _combine_ln_kernel(tot_ref, buf_rows_ref, hbm_rows_ref, nbuf_rows_ref, nhbm_rows_ref, xb_ref, x_ref, loc_ref,
                       gate_ref, y_hbm, w13_ref, w2_ref, g_ref, b_ref, o_ref, ob_ref, ybuf, sem, *, alpha, dh):
    i = pl.program_id(0)
    n_tiles = pl.num_programs(0)
    _, rows, hw = ybuf.shape
    slot = i % 2

    def fetch_tile(s, n, brow_ref, hrow_ref):
        def clear(j, carry):
            ybuf[s, pl.ds(pl.multiple_of(j * RUN_ALIGN, RUN_ALIGN), RUN_ALIGN), :] = jnp.zeros(
                (RUN_ALIGN, hw), ybuf.dtype)
            return carry

        lax.fori_loop(n, rows // RUN_ALIGN, clear, 0)

        def fetch(c, carry):
            src = pl.multiple_of(hrow_ref[c], RUN_ALIGN)
            dst = pl.multiple_of(brow_ref[c], RUN_ALIGN)
            pltpu.make_async_copy(y_hbm.at[pl.ds(src, RUN_ALIGN)], ybuf.at[s, pl.ds(dst, RUN_ALIGN)],
                                  sem.at[s]).start()
            return carry

        lax.fori_loop(0, n, fetch, 0)

    @pl.when(i == 0)
    def _():
        fetch_tile(slot, tot_ref[i], buf_rows_ref, hbm_rows_ref)

    @pl.when(i + 1 < n_tiles)
    def _():
        fetch_tile(1 - slot, tot_ref[jnp.minimum(i + 1, n_tiles - 1)], nbuf_rows_ref, nhbm_rows_ref)

    u = jnp.dot(xb_ref[...], w13_ref[...], preferred_element_type=F32)
    hs = (_silu(u[:, :dh]) * u[:, dh:]).astype(BF16)
    f = jnp.dot(hs, w2_ref[...], preferred_element_type=F32)

    _wait_chunks(lambda n: pltpu.make_async_copy(y_hbm.at[pl.ds(0, n)], ybuf.at[slot, pl.ds(0, n)], sem.at[slot]),
                 tot_ref[i])

    loc = loc_ref[...]
    gate = gate_ref[...]
    tm = loc.shape[0]
    r_lo = jnp.zeros((tm, hw), F32)
    r_hi = jnp.zeros((tm, hw), F32)
    for c0 in range(0, rows, COMBINE_ROWS):
        col = c0 + lax.broadcasted_iota(jnp.int32, (tm, COMBINE_ROWS), 1)
        sel = jnp.zeros((tm, COMBINE_ROWS), F32)
        for k in range(TOPK):
            sel = jnp.where(col == loc[:, k:k + 1], gate[:, k:k + 1], sel)
        sel = sel.astype(BF16)
        y_lo, y_hi = _unpack_halves(ybuf[slot, c0:c0 + COMBINE_ROWS, :])
        r_lo = r_lo + jnp.dot(sel, y_lo, preferred_element_type=F32)
        r_hi = r_hi + jnp.dot(sel, y_hi, preferred_element_type=F32)
    f = f + jnp.concatenate([r_lo, r_hi], axis=1)
    y = _layer_norm_rows(alpha * x_ref[...] + f, g_ref[...], b_ref[...])
    o_ref[...] = y
    ob_ref[...] = y.astype(BF16)


def _combine_ln(xb, x, loc, gate, y_sorted, tile_chunks, chunk_buf, chunk_hbm, w13, w2, g, b, alpha, n_experts):
    m, d = x.shape
    dh = w2.shape[0]
    tm = CMB_TILE
    buf_rows = -(-(tm * TOPK + n_experts * (RUN_ALIGN - 1)) // 256) * 256
    row = lambda i, *_: (i, 0)
    fix = lambda i, *_: (0, 0)
    n_tiles = m // tm
    lst = pl.BlockSpec((CHUNK_LIST,), lambda i, *_: (i,), memory_space=pltpu.SMEM)
    nxt = pl.BlockSpec((CHUNK_LIST,), lambda i, *_: (jnp.minimum(i + 1, n_tiles - 1),), memory_space=pltpu.SMEM)
    grid_spec = pltpu.PrefetchScalarGridSpec(
        num_scalar_prefetch=1,
        grid=(n_tiles,),
        in_specs=[lst, lst, nxt, nxt, pl.BlockSpec((tm, d), row), pl.BlockSpec((tm, d), row),
                  pl.BlockSpec((tm, TOPK), row),
                  pl.BlockSpec((tm, TOPK), row), pl.BlockSpec(memory_space=pl.ANY),
                  pl.BlockSpec((d, 2 * dh), fix), pl.BlockSpec((dh, d), fix),
                  pl.BlockSpec((1, d), fix), pl.BlockSpec((1, d), fix)],
        out_specs=[pl.BlockSpec((tm, d), row), pl.BlockSpec((tm, d), row)],
        scratch_shapes=[pltpu.VMEM((2, buf_rows, y_sorted.shape[1]), y_sorted.dtype),
                        pltpu.SemaphoreType.DMA((2,))],
    )
    return pl.pallas_call(
        functools.partial(_combine_ln_kernel, alpha=alpha, dh=dh),
        grid_spec=grid_spec,
        out_shape=[jax.ShapeDtypeStruct((m, d), F32), jax.ShapeDtypeStruct((m, d), BF16)],
        compiler_params=_cparams("arbitrary"),
        name="combine_ln",
    )(tile_chunks, chunk_buf, chunk_hbm, chunk_buf, chunk_hbm, xb, x, loc, gate, y_sorted, w13, w2, g, b)


def _attn_kernel(q_ref, k0_ref, k1_ref, k2_ref, v0_ref, v1_ref, v2_ref, bias_ref, sink_ref, init_ref, o_ref,
                 ot_ref, *, n_kv):
    del init_ref
    hd = HEAD_DIM
    qt = (q_ref[...] * ATTN_SCALE).T.astype(BF16)
    kb = jnp.concatenate([k0_ref[...], k1_ref[...], k2_ref[...]], 0).astype(BF16)
    vt = jnp.concatenate([v0_ref[...], v1_ref[...], v2_ref[...]], 0).T.astype(BF16)
    bias = bias_ref[0]
    bias = jnp.concatenate([bias] * GROUP, axis=1)
    for h in range(n_kv):
        kh = kb[:, h * hd:(h + 1) * hd]
        qth = jnp.concatenate(
            [qt[(h * GROUP + g) * hd:(h * GROUP + g + 1) * hd, :] for g in range(GROUP)], axis=1)
        s = jnp.dot(kh, qth, preferred_element_type=F32) + bias
        sink = sink_ref[h]
        m = jnp.maximum(jnp.max(s, axis=0, keepdims=True), sink)
        p = jnp.exp(s - m)
        den = jnp.sum(p, axis=0, keepdims=True) + jnp.exp(sink - m)
        ot = jnp.dot(vt[h * hd:(h + 1) * hd, :], p.astype(BF16), preferred_element_type=F32)
        ot = ot * (1.0 / den)
        for g in range(GROUP):
            r0 = (h * GROUP + g) * hd
            ot_ref[r0:r0 + hd, :] = ot[:, g * QBLK:(g + 1) * QBLK]
    o_ref[...] = ot_ref[...].T.astype(BF16)


def _attention(q_arr, k_arr, n_steps, q_map, kv_maps, bias, bias_map, sink_rows, init, out_map,
               q_w, kv_w, k_col, v_col, name):
    n_kv = kv_w // HEAD_DIM
    kspecs = [pl.BlockSpec((QBLK, kv_w), (lambda i, f=f: (f(i), k_col))) for f in kv_maps]
    vspecs = [pl.BlockSpec((QBLK, kv_w), (lambda i, f=f: (f(i), v_col))) for f in kv_maps]
    nk = 3 * QBLK
    return pl.pallas_call(
        functools.partial(_attn_kernel, n_kv=n_kv),
        grid=(n_steps,),
        in_specs=[pl.BlockSpec((QBLK, q_w), lambda i: (q_map(i), 0))] + kspecs + vspecs + [
            pl.BlockSpec((1, nk, QBLK), lambda i: (bias_map(i), 0, 0)),
            pl.BlockSpec((n_kv, 1, GROUP * QBLK), lambda i: (0, 0, 0)),
            pl.BlockSpec(memory_space=pl.ANY)],
        out_specs=pl.BlockSpec((QBLK, q_w), lambda i: (out_map(i), 0)),
        out_shape=jax.ShapeDtypeStruct(init.shape, BF16),
        scratch_shapes=[pltpu.VMEM((q_w, QBLK), F32)],
        input_output_aliases={9: 0},
        compiler_params=_cparams("parallel"),
        name=name,
    )(q_arr, k_arr, k_arr, k_arr, k_arr, k_arr, k_arr, bias, sink_rows, init)


def _gelu_tanh(x):
    c = math.sqrt(2.0 / math.pi)
    return x * (0.5 * (1.0 + jnp.tanh(c * (x + 0.044715 * (x * x * x)))))


def _lru_kernel(xb_ref, gb_ref, cprev_ref, hprev_ref, cw_ref, cb_ref, gaw_ref, gab_ref, gxw_ref, gxb_ref,
                sp_ref, init_ref, y_ref, cnew_ref, hnew_ref, xcat, a_scr, b_scr, hcar, *, tt, valid_last):
    del init_ref
    j = pl.program_id(1)
    c_rnn = xb_ref.shape[1]
    bw = c_rnn // N_LRU_BLOCKS
    tail = SUBLANES

    @pl.when(j == 0)
    def _():
        xcat[0:tail, :] = jnp.zeros((tail, c_rnn), F32)
        xcat[tail - (CONV_W - 1):tail, :] = cprev_ref[...]
        hcar[...] = jnp.broadcast_to(hprev_ref[...], (SUBLANES, c_rnn))

    xcat[tail:tail + tt, :] = xb_ref[...]
    cw = cw_ref[...]
    xc = cb_ref[...]
    for tap in range(CONV_W):
        off = tail - (CONV_W - 1) + tap
        xc = xc + xcat[off:off + tt, :] * cw[tap:tap + 1, :]
    cnew_ref[...] = xcat[tail + valid_last - (CONV_W - 1):tail + valid_last, :]
    xcat[0:tail, :] = xcat[tt:tt + tail, :]

    xcb = xc.astype(BF16)
    rs, gs = [], []
    for n in range(N_LRU_BLOCKS):
        xs = xcb[:, n * bw:(n + 1) * bw]
        rs.append(jnp.dot(xs, gaw_ref[n], preferred_element_type=F32))
        gs.append(jnp.dot(xs, gxw_ref[n], preferred_element_type=F32))
    r = _sigmoid(jnp.concatenate(rs, axis=1) + gab_ref[...])
    gi = _sigmoid(jnp.concatenate(gs, axis=1) + gxb_ref[...])
    log_a = (-LRU_C * r) * sp_ref[...]
    a = jnp.exp(log_a)
    a_scr[...] = a
    b_scr[...] = jnp.sqrt(-jnp.tanh(log_a) * (a * a + 1.0)) * (gi * xc)

    row = lax.broadcasted_iota(jnp.int32, (SUBLANES, SCAN_LANES), 0)
    for c in range(c_rnn // SCAN_LANES):
        cs = slice(c * SCAN_LANES, (c + 1) * SCAN_LANES)

        def body(i, carry, cs=cs):
            r0 = pl.multiple_of(i * SUBLANES, SUBLANES)
            av = a_scr[pl.ds(r0, SUBLANES), cs]
            bv = b_scr[pl.ds(r0, SUBLANES), cs]
            for k in (1, 2, 4):
                a_sh = jnp.where(row >= k, pltpu.roll(av, k, 0), 1.0)
                b_sh = jnp.where(row >= k, pltpu.roll(bv, k, 0), 0.0)
                bv = av * b_sh + bv
                av = av * a_sh
            hv = av * carry + bv
            b_scr[pl.ds(r0, SUBLANES), cs] = hv
            return jnp.broadcast_to(hv[SUBLANES - 1:SUBLANES, :], (SUBLANES, SCAN_LANES))

        hcar[:, cs] = lax.fori_loop(0, tt // SUBLANES, body, hcar[:, cs])

    h = b_scr[...]
    hnew_ref[...] = b_scr[valid_last - 1:valid_last, :]
    y_ref[...] = (h * _gelu_tanh(gb_ref[...])).astype(BF16)


def _lru_seq(u, n_seq, n_tiles, tt, row0, seq_stride, valid_last, init, out_row0, cprev, hprev, wts):
    c_rnn = u.shape[1] // 2
    cw, cb, gaw, gab, gxw, gxb, sp = wts
    b0, bs, ob0 = row0 // tt, seq_stride // tt, out_row0 // tt
    bw = c_rnn // N_LRU_BLOCKS
    fix2 = lambda s, j: (0, 0)
    fix3 = lambda s, j: (0, 0, 0)
    per_seq = lambda s, j: (s, 0, 0)
    return pl.pallas_call(
        functools.partial(_lru_kernel, tt=tt, valid_last=valid_last),
        grid=(n_seq, n_tiles),
        in_specs=[pl.BlockSpec((tt, c_rnn), lambda s, j: (b0 + s * bs + j, 0)),
                  pl.BlockSpec((tt, c_rnn), lambda s, j: (b0 + s * bs + j, 1)),
                  pl.BlockSpec((None, CONV_W - 1, c_rnn), per_seq),
                  pl.BlockSpec((None, 1, c_rnn), per_seq),
                  pl.BlockSpec((CONV_W, c_rnn), fix2), pl.BlockSpec((1, c_rnn), fix2),
                  pl.BlockSpec((N_LRU_BLOCKS, bw, bw), fix3), pl.BlockSpec((1, c_rnn), fix2),
                  pl.BlockSpec((N_LRU_BLOCKS, bw, bw), fix3), pl.BlockSpec((1, c_rnn), fix2),
                  pl.BlockSpec((1, c_rnn), fix2), pl.BlockSpec(memory_space=pl.ANY)],
        out_specs=[pl.BlockSpec((tt, c_rnn), lambda s, j: (ob0 + s * bs + j, 0)),
                   pl.BlockSpec((None, CONV_W - 1, c_rnn), per_seq),
                   pl.BlockSpec((None, 1, c_rnn), per_seq)],
        out_shape=[jax.ShapeDtypeStruct(init.shape, BF16),
                   jax.ShapeDtypeStruct((n_seq, CONV_W - 1, c_rnn), F32),
                   jax.ShapeDtypeStruct((n_seq, 1, c_rnn), F32)],
        scratch_shapes=[pltpu.VMEM((SUBLANES + tt, c_rnn), F32), pltpu.VMEM((tt, c_rnn), F32),
                        pltpu.VMEM((tt, c_rnn), F32), pltpu.VMEM((SUBLANES, c_rnn), F32)],
        input_output_aliases={11: 0},
        compiler_params=_cparams("parallel", "arbitrary"),
        name="rglru_seq",
    )(u, u, cprev, hprev, cw, cb, gaw, gab, gxw, gxb, sp, init)


def _pack_halves(y):
    h = y.shape[1] // 2
    lo = lax.bitcast_convert_type(y[:, :h].astype(BF16).astype(F32), jnp.uint32)
    hi = lax.bitcast_convert_type(y[:, h:].astype(BF16).astype(F32), jnp.uint32)
    return (lo >> 16) | (hi & jnp.uint32(0xFFFF0000))


def _unpack_halves(w):
    lo = lax.bitcast_convert_type(w << 16, F32).astype(BF16)
    hi = lax.bitcast_convert_type(w & jnp.uint32(0xFFFF0000), F32).astype(BF16)
    return lo, hi


MOE_SPLIT = 1
DISPATCH_ROWS = 128


def _dispatch_kernel(tot_ref, zs_ref, zc_ref, buf_rows_ref, hbm_rows_ref, x_ref, loct_ref, o_hbm, obuf, zbuf, sem,
                     zsem, *, n_experts):
    i = pl.program_id(0)
    _, rows, hw = obuf.shape
    tm = x_ref.shape[0]
    slot = i % 2

    def drain(s, n):
        _wait_chunks(lambda m: pltpu.make_async_copy(obuf.at[s, pl.ds(0, m)], o_hbm.at[pl.ds(0, m)], sem.at[s]), n)

    @pl.when(i >= 2)
    def _():
        drain(slot, tot_ref[jnp.maximum(i - 2, 0)])

    loct = loct_ref[...]
    x = x_ref[...]
    x_lo, x_hi = x[:, :hw], x[:, hw:]
    for r0 in range(0, rows, DISPATCH_ROWS):
        rowi = r0 + lax.broadcasted_iota(jnp.int32, (DISPATCH_ROWS, tm), 0)
        p = jnp.zeros((DISPATCH_ROWS, tm), F32)
        for k in range(TOPK):
            p = jnp.where(rowi == loct[k:k + 1, :], 1.0, p)
        pb = p.astype(BF16)
        lo = jnp.dot(pb, x_lo, preferred_element_type=F32)
        hi = jnp.dot(pb, x_hi, preferred_element_type=F32)
        obuf[slot, r0:r0 + DISPATCH_ROWS, :] = ((lax.bitcast_convert_type(lo, jnp.uint32) >> 16)
                                                | lax.bitcast_convert_type(hi, jnp.uint32))

    def send(c, carry):
        src = pl.multiple_of(buf_rows_ref[c], RUN_ALIGN)
        dst = pl.multiple_of(hbm_rows_ref[c], RUN_ALIGN)
        pltpu.make_async_copy(obuf.at[slot, pl.ds(src, RUN_ALIGN)], o_hbm.at[pl.ds(dst, RUN_ALIGN)],
                              sem.at[slot]).start()
        return carry

    lax.fori_loop(0, tot_ref[i], send, 0)

    @pl.when(i == pl.num_programs(0) - 1)
    def _():
        drain(slot, tot_ref[i])

        @pl.when(i >= 1)
        def _():
            drain(1 - slot, tot_ref[jnp.maximum(i - 1, 0)])

        zbuf[...] = jnp.zeros(zbuf.shape, zbuf.dtype)

        def zero_copy(dst_row):
            return pltpu.make_async_copy(zbuf, o_hbm.at[pl.ds(pl.multiple_of(dst_row, RUN_ALIGN), RUN_ALIGN)], zsem)

        def zero_expert(e, c):
            return lax.fori_loop(0, zc_ref[e], lambda j, c2: (zero_copy(zs_ref[e] + j * RUN_ALIGN).start(), c2)[1], c)

        def zero_wait(e, c):
            return lax.fori_loop(0, zc_ref[e], lambda j, c2: (zero_copy(0).wait(), c2)[1], c)

        lax.fori_loop(0, n_experts, zero_expert, 0)
        lax.fori_loop(0, n_experts, zero_wait, 0)


def _dispatch(xb, loc_t, tile_chunks, chunk_buf, chunk_hbm, zero_start, zero_chunks, n_rows, n_experts):
    m, d = xb.shape
    tm = CMB_TILE
    buf_rows = -(-(tm * TOPK + n_experts * (RUN_ALIGN - 1)) // DISPATCH_ROWS) * DISPATCH_ROWS
    lst = pl.BlockSpec((CHUNK_LIST,), lambda i, *_: (i,), memory_space=pltpu.SMEM)
    grid_spec = pltpu.PrefetchScalarGridSpec(
        num_scalar_prefetch=3,
        grid=(m // tm,),
        in_specs=[lst, lst, pl.BlockSpec((tm, d), lambda i, *_: (i, 0)),
                  pl.BlockSpec((None, TOPK, tm), lambda i, *_: (i, 0, 0))],
        out_specs=pl.BlockSpec(memory_space=pl.ANY),
        scratch_shapes=[pltpu.VMEM((2, buf_rows, d // 2), jnp.uint32), pltpu.VMEM((RUN_ALIGN, d // 2), jnp.uint32),
                        pltpu.SemaphoreType.DMA((2,)), pltpu.SemaphoreType.DMA(())],
    )
    return pl.pallas_call(
        functools.partial(_dispatch_kernel, n_experts=n_experts),
        grid_spec=grid_spec,
        out_shape=jax.ShapeDtypeStruct((n_rows, d // 2), jnp.uint32),
        compiler_params=_cparams("arbitrary"),
        name="moe_dispatch",
    )(tile_chunks, zero_start, zero_chunks, chunk_buf, chunk_hbm, xb, loc_t)


def _moe_kernel(be_ref, nu_ref, x_ref, w1_ref, w3_ref, w2_ref, o_ref, w1b, w3b, w2b):
    i = pl.program_id(0)

    @pl.when(i < nu_ref[0])
    def _():
        prev = be_ref[jnp.maximum(i - 1, 0)]

        @pl.when(jnp.logical_or(i == 0, be_ref[i] != prev))
        def _():
            w1b[...] = w1_ref[...].astype(BF16)
            w3b[...] = w3_ref[...].astype(BF16)
            w2b[...] = w2_ref[...].astype(BF16)

        hk = x_ref.shape[1]
        sub = x_ref.shape[0] // MOE_SPLIT
        for s in range(MOE_SPLIT):
            xlo, xhi = _unpack_halves(x_ref[s * sub:(s + 1) * sub, :])
            a = (jnp.dot(xlo, w1b[0:hk, :], preferred_element_type=F32)
                 + jnp.dot(xhi, w1b[hk:2 * hk, :], preferred_element_type=F32))
            b = (jnp.dot(xlo, w3b[0:hk, :], preferred_element_type=F32)
                 + jnp.dot(xhi, w3b[hk:2 * hk, :], preferred_element_type=F32))
            h = (_silu(a) * b).astype(BF16)
            o_ref[s * sub:(s + 1) * sub, :] = _pack_halves(jnp.dot(h, w2b[...], preferred_element_type=F32))


def _moe_experts(x_sorted, w1, w3, w2, layer, block_expert, n_used):
    n_rows, hw = x_sorted.shape
    d, de = w1.shape[2], w1.shape[3]
    nb = n_rows // MOE_BLK
    blk = lambda i, be, nu: (jnp.minimum(i, nu[0] - 1), 0)
    wmap = lambda i, be, nu: (layer, be[jnp.minimum(i, nu[0] - 1)], 0, 0)
    grid_spec = pltpu.PrefetchScalarGridSpec(
        num_scalar_prefetch=2,
        grid=(nb,),
        in_specs=[pl.BlockSpec((MOE_BLK, hw), blk),
                  pl.BlockSpec((None, None, d, de), wmap), pl.BlockSpec((None, None, d, de), wmap),
                  pl.BlockSpec((None, None, de, d), wmap)],
        out_specs=pl.BlockSpec((MOE_BLK, d // 2), blk),
        scratch_shapes=[pltpu.VMEM((d, de), BF16), pltpu.VMEM((d, de), BF16), pltpu.VMEM((de, d), BF16)],
    )
    return pl.pallas_call(
        _moe_kernel,
        grid_spec=grid_spec,
        out_shape=jax.ShapeDtypeStruct((n_rows, d // 2), jnp.uint32),
        compiler_params=_cparams("arbitrary"),
        name="moe_experts",
    )(block_expert, n_used, x_sorted, w1, w3, w2)


def _route(logits, router_bias, valid, n_experts):
    t = logits.shape[0]
    per = n_experts // N_GROUPS
    scores = jax.nn.sigmoid(logits)
    biased = scores + router_bias.astype(F32)
    grp = biased.reshape(t, N_GROUPS, per)
    m1 = jnp.max(grp, -1, keepdims=True)
    is_max = grp == m1
    first = is_max & (jnp.cumsum(is_max.astype(jnp.int32), -1) == 1)
    m2 = jnp.max(jnp.where(first, -jnp.inf, grp), -1)
    grp_score = m1[..., 0] + m2

    def rank(v):
        n = v.shape[-1]
        idx = jnp.arange(n)
        vi, vj = v[:, :, None], v[:, None, :]
        beaten = (vj > vi) | ((vj == vi) & (idx[None, None, :] < idx[None, :, None]))
        return jnp.sum(beaten.astype(jnp.int32), -1)

    gsel = rank(grp_score) < TOPK_GROUPS
    emask = jnp.repeat(gsel, per, axis=-1)
    e_rank = rank(jnp.where(emask, biased, NEG_INF))
    slot = e_rank[:, None, :] == jnp.arange(TOPK)[None, :, None]
    eidx = jnp.sum(jnp.where(slot, jnp.arange(n_experts)[None, None, :], 0), -1)
    g = jnp.sum(jnp.where(slot, scores[:, None, :], 0.0), -1)
    g = g / (jnp.sum(g, -1, keepdims=True) + 1e-20) * ROUTED_SCALE
    g = jnp.where(valid[:, None], g, 0.0)
    member = ((e_rank < TOPK) & valid[:, None]).astype(jnp.int32)
    return eidx.astype(jnp.int32), g, member


def _dispatch_plan(eidx, member, valid, n_experts, n_rows):
    t = eidx.shape[0]
    nt = t // CMB_TILE
    i32 = jnp.int32
    m3 = member.reshape(nt, CMB_TILE, n_experts)
    cnt = jnp.sum(m3, axis=1)
    rlen = (cnt + RUN_ALIGN - 1) // RUN_ALIGN * RUN_ALIGN
    tri = (jnp.arange(CMB_TILE)[:, None] > jnp.arange(CMB_TILE)[None, :]).astype(BF16)
    rank = jnp.einsum("ts,nse->nte", tri, m3.astype(BF16), preferred_element_type=F32)
    rank = rank.astype(i32).reshape(t, n_experts)
    seg = jnp.sum(rlen, axis=0)
    padded = (seg + MOE_BLK - 1) // MOE_BLK * MOE_BLK
    pad_end = jnp.cumsum(padded)
    pad_start = pad_end - padded
    run_off = jnp.cumsum(rlen, axis=0) - rlen
    run_start = pad_start[None, :] + run_off
    lbase = jnp.cumsum(rlen, axis=1) - rlen
    pos = (jnp.repeat(lbase, CMB_TILE, axis=0) + rank)
    hit = eidx[:, :, None] == jnp.arange(n_experts, dtype=i32)[None, None, :]
    loc = jnp.sum(jnp.where(hit, pos[:, None, :], 0), axis=-1)
    loc = jnp.where(valid[:, None], loc, -1).astype(i32)
    loc_t = loc.reshape(nt, CMB_TILE, TOPK).transpose(0, 2, 1)
    nb = n_rows // MOE_BLK
    blk_row = jnp.arange(nb, dtype=i32) * MOE_BLK
    block_expert = jnp.minimum(jnp.sum((pad_end[None, :] <= blk_row[:, None]).astype(i32), axis=1),
                               n_experts - 1).astype(i32)
    n_used = jnp.maximum(pad_end[-1] // MOE_BLK, 1).astype(i32).reshape(1)

    n_chunks = (rlen // RUN_ALIGN).astype(i32)
    cum = jnp.cumsum(n_chunks, axis=1)
    c = jnp.arange(CHUNK_LIST, dtype=i32)
    e_c = jnp.minimum(jnp.sum((cum[:, None, :] <= c[None, :, None]).astype(i32), axis=-1), n_experts - 1)
    own = e_c[:, :, None] == jnp.arange(n_experts, dtype=i32)[None, None, :]
    pick = lambda tab: jnp.sum(jnp.where(own, tab[:, None, :], 0), axis=-1)
    off = (c[None, :] - pick(cum - n_chunks)) * RUN_ALIGN
    chunk_buf = (pick(lbase) + off).reshape(-1).astype(i32)
    chunk_hbm = (pick(run_start) + off).reshape(-1).astype(i32)
    return (loc, loc_t, cum[:, -1].astype(i32), chunk_buf, chunk_hbm, (pad_start + seg).astype(i32),
            ((padded - seg) // RUN_ALIGN).astype(i32), block_expert, n_used)


def _split_bf16(w):
    hi = w.astype(BF16)
    lo = (w - hi.astype(F32)).astype(BF16)
    return hi, lo


def kernel(x_prompt, x_sample, cache_meta_k, cache_meta_v, cache_win_k, cache_win_v, state_conv, state_h, meta_tokens, ln_g, ln_b, attn_w_in, attn_w_out, attn_sink, lru_w_in, lru_conv_w, lru_conv_b, lru_gate_a_w, lru_gate_a_b, lru_gate_x_w, lru_gate_x_b, lru_lambda, lru_w_out, moe_router_w, moe_router_bias, moe_w1, moe_w3, moe_w2, moe_shared_w1, moe_shared_w3, moe_shared_w2):
    bsz, seq, d = x_prompt.shape
    dec_b, dec_s, _ = x_sample.shape
    n_meta = meta_tokens.shape[0]
    depth = ln_g.shape[0]
    n_experts = moe_router_w.shape[2]
    past_len = PAST_LEN
    n_heads = d // HEAD_DIM
    n_kv = n_heads // GROUP
    q_w, kv_w = n_heads * HEAD_DIM, n_kv * HEAD_DIM
    c_rnn = lru_w_in.shape[2] // 2
    alpha = (2 * depth) ** 0.25

    t_real = bsz * seq
    s_base, s_rows = t_real, dec_b * dec_s
    m_base = s_base + s_rows
    t_pad = m_base + bsz * META_BLK
    assert seq % LRU_TILE == 0 and seq % QBLK == 0 and s_rows % QBLK == 0 and t_pad % ROW_TILE == 0
    assert n_meta <= META_BLK and dec_s <= QBLK and q_w == d and kv_w % LANES == 0
    assert (CMB_TILE * TOPK + n_experts * (RUN_ALIGN - 1)) // RUN_ALIGN <= CHUNK_LIST and t_pad % CMB_TILE == 0
    assert cache_win_k.shape[2] == WINDOW and WINDOW == QBLK and n_meta + WINDOW + dec_s <= 3 * QBLK
    nqb = seq // QBLK

    pos = np.zeros((t_pad,), np.float32)
    pos[:t_real] = np.tile(np.arange(seq) + n_meta, bsz)
    pos[s_base:m_base] = np.tile(past_len + n_meta + np.arange(dec_s), dec_b)
    valid_np = np.zeros((t_pad,), bool)
    valid_np[:m_base] = True
    for b in range(bsz):
        pos[m_base + b * META_BLK:m_base + b * META_BLK + n_meta] = np.arange(n_meta)
        valid_np[m_base + b * META_BLK:m_base + b * META_BLK + n_meta] = True
    valid = jnp.asarray(valid_np)

    half = ROT_DIM // 2
    freqs = ROPE_THETA ** (-jnp.arange(0, ROT_DIM, 2, dtype=F32) / ROT_DIM)
    ang = jnp.asarray(pos)[:, None] * freqs[None, :]
    cos, sin = jnp.cos(ang), jnp.sin(ang)
    ones = jnp.ones((t_pad, HEAD_DIM - ROT_DIM), F32)
    zeros_h = jnp.zeros((t_pad, half), F32)
    zeros_r = jnp.zeros((t_pad, HEAD_DIM - ROT_DIM), F32)
    reps = LANES // HEAD_DIM
    cos_t = jnp.tile(jnp.concatenate([cos, cos, ones], 1), (1, reps))
    sin_lo_t = jnp.tile(jnp.concatenate([zeros_h, sin, zeros_r], 1), (1, reps))
    sin_hi_t = jnp.tile(jnp.concatenate([-sin, zeros_h, zeros_r], 1), (1, reps))

    kj = np.arange(3 * QBLK)[:, None]
    qc = np.arange(QBLK)[None, :] // CHUNK
    key_chunk = np.where(kj < 2 * QBLK, kj // CHUNK - 2, 0)
    band_ok = (kj < 2 * QBLK) & (qc - key_chunk >= 0) & (qc - key_chunk <= WINDOW // CHUNK)
    meta_ok = (kj >= 2 * QBLK) & (kj < 2 * QBLK + n_meta)
    later = band_ok | meta_ok
    first = (band_ok & (kj >= QBLK)) | meta_ok
    only_meta = np.broadcast_to(meta_ok, later.shape)
    bias_p = jnp.asarray(np.where(np.stack([first, later, only_meta]), 0.0, NEG_INF).astype(np.float32))
    n_keys_s = n_meta + WINDOW + dec_s
    bias_s = jnp.asarray(np.where(np.broadcast_to(kj < n_keys_s, later.shape), 0.0, NEG_INF)
                         .astype(np.float32))[None]

    meta_rows = jnp.concatenate([meta_tokens.astype(F32), jnp.zeros((META_BLK - n_meta, d), F32)], 0)
    x = jnp.concatenate([x_prompt.reshape(t_real, d), x_sample.reshape(s_rows, d),
                         jnp.tile(meta_rows, (bsz, 1))], 0)
    xb = x.astype(BF16)

    n_assign = (t_real + s_rows + bsz * n_meta) * TOPK
    n_runs = (t_pad // CMB_TILE) * n_experts
    n_rows = -(-(n_assign + n_runs * (RUN_ALIGN - 1) + n_experts * (MOE_BLK - 1)) // MOE_BLK) * MOE_BLK

    outs = {k: [] for k in ("mk", "mv", "wk", "wv", "ks", "vs", "cp", "hp", "cs", "hs")}
    for l in range(depth):
        idx = l // 2
        if l % 2 == 0:
            w_in = attn_w_in[idx].astype(BF16)
            qkv = _qkv_proj(xb, w_in, cos_t, sin_lo_t, sin_hi_t, q_w + kv_w)
            k_col, v_col = q_w // kv_w, q_w // kv_w + 1
            sink_rows = jnp.repeat(attn_sink[idx].astype(F32).reshape(n_kv, 1, GROUP), QBLK, axis=2)

            q_s = qkv[s_base:m_base, :q_w].reshape(dec_b, dec_s, q_w)
            q_s = jnp.pad(q_s, ((0, 0), (0, QBLK - dec_s), (0, 0))).reshape(dec_b * QBLK, q_w)
            k_new = qkv[s_base:m_base, q_w:q_w + kv_w].reshape(dec_b, dec_s, kv_w)
            v_new = qkv[s_base:m_base, q_w + kv_w:].reshape(dec_b, dec_s, kv_w)
            pad_k = jnp.zeros((dec_b, 3 * QBLK - n_keys_s, kv_w), F32)
            k_s = jnp.concatenate([cache_meta_k[idx].reshape(dec_b, n_meta, kv_w).astype(F32),
                                   cache_win_k[idx].reshape(dec_b, WINDOW, kv_w).astype(F32), k_new, pad_k], 1)
            v_s = jnp.concatenate([cache_meta_v[idx].reshape(dec_b, n_meta, kv_w).astype(F32),
                                   cache_win_v[idx].reshape(dec_b, WINDOW, kv_w).astype(F32), v_new, pad_k], 1)
            kv_s = jnp.concatenate([k_s, v_s], axis=2).reshape(dec_b * 3 * QBLK, 2 * kv_w)
            o_s = _attention(q_s, kv_s, dec_b, lambda i: i,
                             [lambda i: 3 * i, lambda i: 3 * i + 1, lambda i: 3 * i + 2], bias_s, lambda i: 0,
                             sink_rows, jnp.zeros((dec_b * QBLK, q_w), BF16), lambda i: i, q_w, kv_w, 0, 1,
                             "attn_sample")
            o_s = o_s.reshape(dec_b, QBLK, q_w)[:, :dec_s].reshape(s_rows, q_w)

            steps = nqb + 1
            is_meta = lambda i: (i % steps) == nqb
            bidx = lambda i: i // steps
            pblk = lambda i: i % steps
            frame_blk = lambda i: bidx(i) * nqb + jnp.minimum(pblk(i), nqb - 1)
            meta_blk = lambda i: m_base // QBLK + bidx(i)
            q_map = lambda i: jnp.where(is_meta(i), meta_blk(i), frame_blk(i))
            prev_map = lambda i: bidx(i) * nqb + jnp.clip(pblk(i) - 1, 0, nqb - 1)
            bias_map = lambda i: jnp.where(is_meta(i), 2, jnp.minimum(pblk(i), 1))
            o_init = jnp.concatenate([jnp.zeros((t_real, q_w), BF16), o_s,
                                      jnp.zeros((t_pad - m_base, q_w), BF16)], 0)
            o = _attention(qkv, qkv, bsz * steps, q_map, [prev_map, frame_blk, meta_blk], bias_p, bias_map,
                           sink_rows, o_init, q_map, q_w, kv_w, k_col, v_col, "attn_prompt")

            kp = qkv[:t_real, q_w:q_w + kv_w].reshape(bsz, seq, n_kv, HEAD_DIM)
            vp = qkv[:t_real, q_w + kv_w:].reshape(bsz, seq, n_kv, HEAD_DIM)
            km = qkv[m_base:, q_w:q_w + kv_w].reshape(bsz, META_BLK, n_kv, HEAD_DIM)[:, :n_meta]
            vm = qkv[m_base:, q_w + kv_w:].reshape(bsz, META_BLK, n_kv, HEAD_DIM)[:, :n_meta]
            outs["mk"].append(km); outs["mv"].append(vm)
            outs["wk"].append(kp[:, -WINDOW:]); outs["wv"].append(vp[:, -WINDOW:])
            outs["ks"].append(k_new.reshape(dec_b, dec_s, n_kv, HEAD_DIM))
            outs["vs"].append(v_new.reshape(dec_b, dec_s, n_kv, HEAD_DIM))
            w_out = attn_w_out[idx].astype(BF16)
        else:
            u = _matmul(xb, lru_w_in[idx].astype(BF16), 2048, "lru_in_proj")
            sp = jax.nn.softplus(-lru_lambda[idx].astype(F32)).reshape(1, c_rnn)
            wts = (lru_conv_w[idx].astype(F32), lru_conv_b[idx].astype(F32).reshape(1, c_rnn),
                   lru_gate_a_w[idx].astype(BF16), lru_gate_a_b[idx].astype(F32).reshape(1, c_rnn),
                   lru_gate_x_w[idx].astype(BF16), lru_gate_x_b[idx].astype(F32).reshape(1, c_rnn), sp)
            zc = jnp.zeros((bsz, CONV_W - 1, c_rnn), F32)
            zh = jnp.zeros((bsz, 1, c_rnn), F32)
            y_m, c_m, h_m = _lru_seq(u, bsz, 1, META_BLK, m_base, META_BLK, n_meta,
                                     jnp.zeros((bsz * META_BLK, c_rnn), BF16), 0, zc, zh, wts)
            y_s, c_s, h_s = _lru_seq(u, dec_b, 1, dec_s, s_base, dec_s, dec_s,
                                     jnp.zeros((s_rows, c_rnn), BF16), 0,
                                     state_conv[idx].astype(F32), state_h[idx].astype(F32).reshape(dec_b, 1, c_rnn),
                                     wts)
            o_init = jnp.concatenate([jnp.zeros((t_real, c_rnn), BF16), y_s, y_m], 0)
            o, c_p, h_p = _lru_seq(u, bsz, seq // LRU_TILE, LRU_TILE, 0, seq, LRU_TILE, o_init, 0,
                                   c_m, h_m, wts)
            outs["cp"].append(c_p); outs["hp"].append(h_p.reshape(bsz, c_rnn))
            outs["cs"].append(c_s); outs["hs"].append(h_s.reshape(dec_b, c_rnn))
            w_out = lru_w_out[idx].astype(BF16)

        ne_pad = -(-n_experts // LANES) * LANES
        wr = jnp.pad(moe_router_w[l].astype(F32), ((0, 0), (0, ne_pad - n_experts)))
        wr_hi, wr_lo = _split_bf16(wr)
        x1, x1b, logits = _proj_ln(o, w_out, x, ln_g[l, 0].astype(F32).reshape(1, d),
                                   ln_b[l, 0].astype(F32).reshape(1, d), wr_hi, wr_lo, alpha)

        eidx, gate, member = _route(logits[:, :n_experts], moe_router_bias[l], valid, n_experts)
        (loc, loc_t, tile_chunks, chunk_buf, chunk_hbm, zero_start, zero_chunks, block_expert,
         n_used) = _dispatch_plan(eidx, member, valid, n_experts, n_rows)
        x_sorted = _dispatch(x1b, loc_t, tile_chunks, chunk_buf, chunk_hbm, zero_start, zero_chunks,
                             n_rows, n_experts)
        y_sorted = _moe_experts(x_sorted, moe_w1, moe_w3, moe_w2, l, block_expert, n_used)

        w13 = jnp.concatenate([moe_shared_w1[l], moe_shared_w3[l]], axis=1).astype(BF16)
        x, xb = _combine_ln(x1b, x1, loc, gate, y_sorted, tile_chunks, chunk_buf, chunk_hbm, w13,
                            moe_shared_w2[l].astype(BF16), ln_g[l, 1].astype(F32).reshape(1, d),
                            ln_b[l, 1].astype(F32).reshape(1, d), alpha, n_experts)

    y_prompt = x[:t_real].reshape(bsz, seq, d)
    y_sample = x[s_base:m_base].reshape(dec_b, dec_s, d)
    st = lambda k: jnp.stack(outs[k])
    return (y_prompt, y_sample, st("mk"), st("mv"), st("wk"), st("wv"), st("ks"), st("vs"),
            st("cp"), st("hp"), st("cs"), st("hs"))
```

```python
import functools
import math

import numpy as np
import jax
import jax.numpy as jnp
from jax import lax
from jax.experimental import pallas as pl
from jax.experimental.pallas import tpu as pltpu

F32 = jnp.float32
BF16 = jnp.bfloat16

PAST_LEN = 1024
CHUNK = 64
WINDOW = 128
HEAD_DIM = 64
GROUP = 8
ROT_DIM = HEAD_DIM // 4
ROPE_THETA = 500000.0
ATTN_SCALE = HEAD_DIM ** -0.5
N_LRU_BLOCKS = 8
CONV_W = 4
LRU_C = 8.0
N_GROUPS = 8
TOPK_GROUPS = 4
TOPK = 8
ROUTED_SCALE = 2.5
LN_EPS = 1e-5
NEG_INF = -1e30

LANES = 128
SUBLANES = 8
QBLK = 2 * CHUNK
META_BLK = 128
ROW_TILE = 512
PROJ_TILE = 256
LRU_TILE = 256
SCAN_LANES = 512
MOE_BLK = 512
CMB_TILE = 256
RUN_ALIGN = SUBLANES
VMEM_LIMIT = 56 * 1024 * 1024


def _cparams(*sem):
    return pltpu.CompilerParams(dimension_semantics=sem, vmem_limit_bytes=VMEM_LIMIT)


def _mm_kernel(x_ref, w_ref, o_ref):
    o_ref[...] = jnp.dot(x_ref[...], w_ref[...], preferred_element_type=F32)


def _matmul(x, w, tn, name):
    m, k = x.shape
    n = w.shape[1]
    tm = ROW_TILE
    return pl.pallas_call(
        _mm_kernel,
        grid=(n // tn, m // tm),
        in_specs=[pl.BlockSpec((tm, k), lambda j, i: (i, 0)),
                  pl.BlockSpec((k, tn), lambda j, i: (0, j))],
        out_specs=pl.BlockSpec((tm, tn), lambda j, i: (i, j)),
        out_shape=jax.ShapeDtypeStruct((m, n), F32),
        compiler_params=_cparams("parallel", "parallel"),
        name=name,
    )(x, w)


def _qkv_kernel(x_ref, w_ref, c_ref, s1_ref, s2_ref, o_ref, *, tn, rope_cols):
    j = pl.program_id(0)
    acc = jnp.dot(x_ref[...], w_ref[...], preferred_element_type=F32)
    c, s1, s2 = c_ref[...], s1_ref[...], s2_ref[...]
    lane = lax.broadcasted_iota(jnp.int32, (acc.shape[0], LANES), 1)
    for g in range(tn // LANES):
        a = acc[:, g * LANES:(g + 1) * LANES]
        lo = pltpu.roll(a, ROT_DIM // 2, 1)
        hi = pltpu.roll(a, LANES - ROT_DIM // 2, 1)
        roped = a * c + lo * s1 + hi * s2
        col = j * tn + g * LANES + lane
        o_ref[:, g * LANES:(g + 1) * LANES] = jnp.where(col < rope_cols, roped, a)


def _qkv_proj(xb, w, cos_t, sin_lo_t, sin_hi_t, rope_cols):
    m, k = xb.shape
    n = w.shape[1]
    tm = ROW_TILE
    tn = 2560 if n % 2560 == 0 else 256
    assert n % tn == 0 and m % tm == 0
    tab = pl.BlockSpec((tm, LANES), lambda j, i: (i, 0))
    return pl.pallas_call(
        functools.partial(_qkv_kernel, tn=tn, rope_cols=rope_cols),
        grid=(n // tn, m // tm),
        in_specs=[pl.BlockSpec((tm, k), lambda j, i: (i, 0)),
                  pl.BlockSpec((k, tn), lambda j, i: (0, j)),
                  tab, tab, tab],
        out_specs=pl.BlockSpec((tm, tn), lambda j, i: (i, j)),
        out_shape=jax.ShapeDtypeStruct((m, n), F32),
        compiler_params=_cparams("parallel", "parallel"),
        name="qkv_rope",
    )(xb, w, cos_t, sin_lo_t, sin_hi_t)


def _layer_norm_rows(z, g, b):
    mu = jnp.mean(z, -1, keepdims=True)
    d = z - mu
    var = jnp.mean(d * d, -1, keepdims=True)
    return d * lax.rsqrt(var + LN_EPS) * g + b


def _proj_ln_kernel(a_ref, w_ref, x_ref, g_ref, b_ref, wrh_ref, wrl_ref, o_ref, ob_ref, lg_ref, *, alpha):
    acc = jnp.dot(a_ref[...], w_ref[...], preferred_element_type=F32)
    y = _layer_norm_rows(alpha * x_ref[...] + acc, g_ref[...], b_ref[...])
    o_ref[...] = y
    yb = y.astype(BF16)
    ob_ref[...] = yb
    ylo = (y - yb.astype(F32)).astype(BF16)
    wrh = wrh_ref[...]
    lg_ref[...] = (jnp.dot(yb, wrh, preferred_element_type=F32)
                   + jnp.dot(ylo, wrh, preferred_element_type=F32)
                   + jnp.dot(yb, wrl_ref[...], preferred_element_type=F32))


def _proj_ln(a, w, x, g, b, wr_hi, wr_lo, alpha):
    m, k = a.shape
    d = w.shape[1]
    ne = wr_hi.shape[1]
    tm = PROJ_TILE
    row = lambda i: (i, 0)
    fix = lambda i: (0, 0)
    return pl.pallas_call(
        functools.partial(_proj_ln_kernel, alpha=alpha),
        grid=(m // tm,),
        in_specs=[pl.BlockSpec((tm, k), row), pl.BlockSpec((k, d), fix), pl.BlockSpec((tm, d), row),
                  pl.BlockSpec((1, d), fix), pl.BlockSpec((1, d), fix),
                  pl.BlockSpec((d, ne), fix), pl.BlockSpec((d, ne), fix)],
        out_specs=[pl.BlockSpec((tm, d), row), pl.BlockSpec((tm, d), row), pl.BlockSpec((tm, ne), row)],
        out_shape=[jax.ShapeDtypeStruct((m, d), F32), jax.ShapeDtypeStruct((m, d), BF16),
                   jax.ShapeDtypeStruct((m, ne), F32)],
        compiler_params=_cparams("parallel"),
        name="proj_ln_router",
    )(a, w, x, g, b, wr_hi, wr_lo)


def _sigmoid(x):
    return 0.5 * (jnp.tanh(0.5 * x) + 1.0)


def _silu(x):
    return x * _sigmoid(x)


COMBINE_ROWS = 256
WAIT_GROUPS = (32, 4, 1)
CHUNK_LIST = 512


def _wait_chunks(make_copy, n):
    for g in WAIT_GROUPS:
        cnt = n // g
        n = n - cnt * g
        lax.fori_loop(0, cnt, lambda j, c, g=g: (make_copy(g * RUN_ALIGN).wait(), c)[1], 0)


def _combine_ln_kernel(tot_ref, buf_rows_ref, hbm_rows_ref, nbuf_rows_ref, nhbm_rows_ref, xb_ref, x_ref, loc_ref,
                       gate_ref, y_hbm, w13_ref, w2_ref, g_ref, b_ref, o_ref, ob_ref, ybuf, sem, *, alpha, dh):
    i = pl.program_id(0)
    n_tiles = pl.num_programs(0)
    _, rows, hw = ybuf.shape
    slot = i % 2

    def fetch_tile(s, n, brow_ref, hrow_ref):
        def clear(j, carry):
            ybuf[s, pl.ds(pl.multiple_of(j * RUN_ALIGN, RUN_ALIGN), RUN_ALIGN), :] = jnp.zeros(
                (RUN_ALIGN, hw), ybuf.dtype)
            return carry

        lax.fori_loop(n, rows // RUN_ALIGN, clear, 0)

        def fetch(c, priority):
            src = pl.multiple_of(hrow_ref[c], RUN_ALIGN)
            dst = pl.multiple_of(brow_ref[c], RUN_ALIGN)
            pltpu.make_async_copy(y_hbm.at[pl.ds(src, RUN_ALIGN)], ybuf.at[s, pl.ds(dst, RUN_ALIGN)],
                                  sem.at[s]).start(priority=priority)

        def fetch_pair(p, carry):
            fetch(2 * p, 0)
            fetch(2 * p + 1, 1)
            return carry

        lax.fori_loop(0, n // 2, fetch_pair, 0)

        @pl.when(n % 2 == 1)
        def _():
            fetch(n - 1, 0)

    @pl.when(i == 0)
    def _():
        fetch_tile(slot, tot_ref[i], buf_rows_ref, hbm_rows_ref)

    @pl.when(i + 1 < n_tiles)
    def _():
        fetch_tile(1 - slot, tot_ref[jnp.minimum(i + 1, n_tiles - 1)], nbuf_rows_ref, nhbm_rows_ref)

    u = jnp.dot(xb_ref[...], w13_ref[...], preferred_element_type=F32)
    hs = (_silu(u[:, :dh]) * u[:, dh:]).astype(BF16)
    f = jnp.dot(hs, w2_ref[...], preferred_element_type=F32)

    _wait_chunks(lambda n: pltpu.make_async_copy(y_hbm.at[pl.ds(0, n)], ybuf.at[slot, pl.ds(0, n)], sem.at[slot]),
                 tot_ref[i])

    loc = loc_ref[...]
    gate = gate_ref[...]
    tm = loc.shape[0]
    r_lo = jnp.zeros((tm, hw), F32)
    r_hi = jnp.zeros((tm, hw), F32)
    for c0 in range(0, rows, COMBINE_ROWS):
        col = c0 + lax.broadcasted_iota(jnp.int32, (tm, COMBINE_ROWS), 1)
        sel = jnp.zeros((tm, COMBINE_ROWS), F32)
        for k in range(TOPK):
            sel = jnp.where(col == loc[:, k:k + 1], gate[:, k:k + 1], sel)
        sel = sel.astype(BF16)
        y_lo, y_hi = _unpack_halves(ybuf[slot, c0:c0 + COMBINE_ROWS, :])
        r_lo = r_lo + jnp.dot(sel, y_lo, preferred_element_type=F32)
        r_hi = r_hi + jnp.dot(sel, y_hi, preferred_element_type=F32)
    f = f + jnp.concatenate([r_lo, r_hi], axis=1)
    y = _layer_norm_rows(alpha * x_ref[...] + f, g_ref[...], b_ref[...])
    o_ref[...] = y
    ob_ref[...] = y.astype(BF16)


def _combine_ln(xb, x, loc, gate, y_sorted, tile_chunks, chunk_buf, chunk_hbm, w13, w2, g, b, alpha, n_experts):
    m, d = x.shape
    dh = w2.shape[0]
    tm = CMB_TILE
    buf_rows = -(-(tm * TOPK + n_experts * (RUN_ALIGN - 1)) // 256) * 256
    row = lambda i, *_: (i, 0)
    fix = lambda i, *_: (0, 0)
    n_tiles = m // tm
    lst = pl.BlockSpec((CHUNK_LIST,), lambda i, *_: (i,), memory_space=pltpu.SMEM)
    nxt = pl.BlockSpec((CHUNK_LIST,), lambda i, *_: (jnp.minimum(i + 1, n_tiles - 1),), memory_space=pltpu.SMEM)
    grid_spec = pltpu.PrefetchScalarGridSpec(
        num_scalar_prefetch=1,
        grid=(n_tiles,),
        in_specs=[lst, lst, nxt, nxt, pl.BlockSpec((tm, d), row), pl.BlockSpec((tm, d), row),
                  pl.BlockSpec((tm, TOPK), row),
                  pl.BlockSpec((tm, TOPK), row), pl.BlockSpec(memory_space=pl.ANY),
                  pl.BlockSpec((d, 2 * dh), fix), pl.BlockSpec((dh, d), fix),
                  pl.BlockSpec((1, d), fix), pl.BlockSpec((1, d), fix)],
        out_specs=[pl.BlockSpec((tm, d), row), pl.BlockSpec((tm, d), row)],
        scratch_shapes=[pltpu.VMEM((2, buf_rows, y_sorted.shape[1]), y_sorted.dtype),
                        pltpu.SemaphoreType.DMA((2,))],
    )
    return pl.pallas_call(
        functools.partial(_combine_ln_kernel, alpha=alpha, dh=dh),
        grid_spec=grid_spec,
        out_shape=[jax.ShapeDtypeStruct((m, d), F32), jax.ShapeDtypeStruct((m, d), BF16)],
        compiler_params=_cparams("arbitrary"),
        name="combine_ln",
    )(tile_chunks, chunk_buf, chunk_hbm, chunk_buf, chunk_hbm, xb, x, loc, gate, y_sorted, w13, w2, g, b)


def _attn_kernel(q_ref, k0_ref, k1_ref, k2_ref, v0_ref, v1_ref, v2_ref, bias_ref, sink_ref, init_ref, o_ref,
                 ot_ref, *, n_kv):
    del init_ref
    hd = HEAD_DIM
    qt = (q_ref[...] * ATTN_SCALE).T.astype(BF16)
    kb = jnp.concatenate([k0_ref[...], k1_ref[...], k2_ref[...]], 0).astype(BF16)
    vt = jnp.concatenate([v0_ref[...], v1_ref[...], v2_ref[...]], 0).T.astype(BF16)
    bias = bias_ref[0]
    bias = jnp.concatenate([bias] * GROUP, axis=1)
    for h in range(n_kv):
        kh = kb[:, h * hd:(h + 1) * hd]
        qth = jnp.concatenate(
            [qt[(h * GROUP + g) * hd:(h * GROUP + g + 1) * hd, :] for g in range(GROUP)], axis=1)
        s = jnp.dot(kh, qth, preferred_element_type=F32) + bias
        sink = sink_ref[h]
        m = jnp.maximum(jnp.max(s, axis=0, keepdims=True), sink)
        p = jnp.exp(s - m)
        den = jnp.sum(p, axis=0, keepdims=True) + jnp.exp(sink - m)
        ot = jnp.dot(vt[h * hd:(h + 1) * hd, :], p.astype(BF16), preferred_element_type=F32)
        ot = ot * (1.0 / den)
        for g in range(GROUP):
            r0 = (h * GROUP + g) * hd
            ot_ref[r0:r0 + hd, :] = ot[:, g * QBLK:(g + 1) * QBLK]
    o_ref[...] = ot_ref[...].T.astype(BF16)


def _attention(q_arr, k_arr, n_steps, q_map, kv_maps, bias, bias_map, sink_rows, init, out_map,
               q_w, kv_w, k_col, v_col, name):
    n_kv = kv_w // HEAD_DIM
    kspecs = [pl.BlockSpec((QBLK, kv_w), (lambda i, f=f: (f(i), k_col))) for f in kv_maps]
    vspecs = [pl.BlockSpec((QBLK, kv_w), (lambda i, f=f: (f(i), v_col))) for f in kv_maps]
    nk = 3 * QBLK
    return pl.pallas_call(
        functools.partial(_attn_kernel, n_kv=n_kv),
        grid=(n_steps,),
        in_specs=[pl.BlockSpec((QBLK, q_w), lambda i: (q_map(i), 0))] + kspecs + vspecs + [
            pl.BlockSpec((1, nk, QBLK), lambda i: (bias_map(i), 0, 0)),
            pl.BlockSpec((n_kv, 1, GROUP * QBLK), lambda i: (0, 0, 0)),
            pl.BlockSpec(memory_space=pl.ANY)],
        out_specs=pl.BlockSpec((QBLK, q_w), lambda i: (out_map(i), 0)),
        out_shape=jax.ShapeDtypeStruct(init.shape, BF16),
        scratch_shapes=[pltpu.VMEM((q_w, QBLK), F32)],
        input_output_aliases={9: 0},
        compiler_params=_cparams("parallel"),
        name=name,
    )(q_arr, k_arr, k_arr, k_arr, k_arr, k_arr, k_arr, bias, sink_rows, init)


def _gelu_tanh(x):
    c = math.sqrt(2.0 / math.pi)
    return x * (0.5 * (1.0 + jnp.tanh(c * (x + 0.044715 * (x * x * x)))))


def _lru_kernel(xb_ref, gb_ref, cprev_ref, hprev_ref, cw_ref, cb_ref, gaw_ref, gab_ref, gxw_ref, gxb_ref,
                sp_ref, init_ref, y_ref, cnew_ref, hnew_ref, xcat, a_scr, b_scr, hcar, *, tt, valid_last):
    del init_ref
    j = pl.program_id(1)
    c_rnn = xb_ref.shape[1]
    bw = c_rnn // N_LRU_BLOCKS
    tail = SUBLANES

    @pl.when(j == 0)
    def _():
        xcat[0:tail, :] = jnp.zeros((tail, c_rnn), F32)
        xcat[tail - (CONV_W - 1):tail, :] = cprev_ref[...]
        hcar[...] = jnp.broadcast_to(hprev_ref[...], (SUBLANES, c_rnn))

    xcat[tail:tail + tt, :] = xb_ref[...]
    cw = cw_ref[...]
    xc = cb_ref[...]
    for tap in range(CONV_W):
        off = tail - (CONV_W - 1) + tap
        xc = xc + xcat[off:off + tt, :] * cw[tap:tap + 1, :]
    cnew_ref[...] = xcat[tail + valid_last - (CONV_W - 1):tail + valid_last, :]
    xcat[0:tail, :] = xcat[tt:tt + tail, :]

    xcb = xc.astype(BF16)
    rs, gs = [], []
    for n in range(N_LRU_BLOCKS):
        xs = xcb[:, n * bw:(n + 1) * bw]
        rs.append(jnp.dot(xs, gaw_ref[n], preferred_element_type=F32))
        gs.append(jnp.dot(xs, gxw_ref[n], preferred_element_type=F32))
    r = _sigmoid(jnp.concatenate(rs, axis=1) + gab_ref[...])
    gi = _sigmoid(jnp.concatenate(gs, axis=1) + gxb_ref[...])
    log_a = (-LRU_C * r) * sp_ref[...]
    a = jnp.exp(log_a)
    a_scr[...] = a
    b_scr[...] = jnp.sqrt(-jnp.tanh(log_a) * (a * a + 1.0)) * (gi * xc)

    row = lax.broadcasted_iota(jnp.int32, (SUBLANES, SCAN_LANES), 0)
    for c in range(c_rnn // SCAN_LANES):
        cs = slice(c * SCAN_LANES, (c + 1) * SCAN_LANES)

        def body(i, carry, cs=cs):
            r0 = pl.multiple_of(i * SUBLANES, SUBLANES)
            av = a_scr[pl.ds(r0, SUBLANES), cs]
            bv = b_scr[pl.ds(r0, SUBLANES), cs]
            for k in (1, 2, 4):
                a_sh = jnp.where(row >= k, pltpu.roll(av, k, 0), 1.0)
                b_sh = jnp.where(row >= k, pltpu.roll(bv, k, 0), 0.0)
                bv = av * b_sh + bv
                av = av * a_sh
            hv = av * carry + bv
            b_scr[pl.ds(r0, SUBLANES), cs] = hv
            return jnp.broadcast_to(hv[SUBLANES - 1:SUBLANES, :], (SUBLANES, SCAN_LANES))

        hcar[:, cs] = lax.fori_loop(0, tt // SUBLANES, body, hcar[:, cs])

    h = b_scr[...]
    hnew_ref[...] = b_scr[valid_last - 1:valid_last, :]
    y_ref[...] = (h * _gelu_tanh(gb_ref[...])).astype(BF16)


def _lru_seq(u, n_seq, n_tiles, tt, row0, seq_stride, valid_last, init, out_row0, cprev, hprev, wts):
    c_rnn = u.shape[1] // 2
    cw, cb, gaw, gab, gxw, gxb, sp = wts
    b0, bs, ob0 = row0 // tt, seq_stride // tt, out_row0 // tt
    bw = c_rnn // N_LRU_BLOCKS
    fix2 = lambda s, j: (0, 0)
    fix3 = lambda s, j: (0, 0, 0)
    per_seq = lambda s, j: (s, 0, 0)
    return pl.pallas_call(
        functools.partial(_lru_kernel, tt=tt, valid_last=valid_last),
        grid=(n_seq, n_tiles),
        in_specs=[pl.BlockSpec((tt, c_rnn), lambda s, j: (b0 + s * bs + j, 0)),
                  pl.BlockSpec((tt, c_rnn), lambda s, j: (b0 + s * bs + j, 1)),
                  pl.BlockSpec((None, CONV_W - 1, c_rnn), per_seq),
                  pl.BlockSpec((None, 1, c_rnn), per_seq),
                  pl.BlockSpec((CONV_W, c_rnn), fix2), pl.BlockSpec((1, c_rnn), fix2),
                  pl.BlockSpec((N_LRU_BLOCKS, bw, bw), fix3), pl.BlockSpec((1, c_rnn), fix2),
                  pl.BlockSpec((N_LRU_BLOCKS, bw, bw), fix3), pl.BlockSpec((1, c_rnn), fix2),
                  pl.BlockSpec((1, c_rnn), fix2), pl.BlockSpec(memory_space=pl.ANY)],
        out_specs=[pl.BlockSpec((tt, c_rnn), lambda s, j: (ob0 + s * bs + j, 0)),
                   pl.BlockSpec((None, CONV_W - 1, c_rnn), per_seq),
                   pl.BlockSpec((None, 1, c_rnn), per_seq)],
        out_shape=[jax.ShapeDtypeStruct(init.shape, BF16),
                   jax.ShapeDtypeStruct((n_seq, CONV_W - 1, c_rnn), F32),
                   jax.ShapeDtypeStruct((n_seq, 1, c_rnn), F32)],
        scratch_shapes=[pltpu.VMEM((SUBLANES + tt, c_rnn), F32), pltpu.VMEM((tt, c_rnn), F32),
                        pltpu.VMEM((tt, c_rnn), F32), pltpu.VMEM((SUBLANES, c_rnn), F32)],
        input_output_aliases={11: 0},
        compiler_params=_cparams("parallel", "arbitrary"),
        name="rglru_seq",
    )(u, u, cprev, hprev, cw, cb, gaw, gab, gxw, gxb, sp, init)


def _pack_halves(y):
    h = y.shape[1] // 2
    lo = lax.bitcast_convert_type(y[:, :h].astype(BF16).astype(F32), jnp.uint32)
    hi = lax.bitcast_convert_type(y[:, h:].astype(BF16).astype(F32), jnp.uint32)
    return (lo >> 16) | (hi & jnp.uint32(0xFFFF0000))


def _unpack_halves(w):
    lo = lax.bitcast_convert_type(w << 16, F32).astype(BF16)
    hi = lax.bitcast_convert_type(w & jnp.uint32(0xFFFF0000), F32).astype(BF16)
    return lo, hi


MOE_SPLIT = 1
DISPATCH_ROWS = 128


def _dispatch_kernel(tot_ref, zs_ref, zc_ref, buf_rows_ref, hbm_rows_ref, x_ref, loct_ref, o_hbm, obuf, zbuf, sem,
                     zsem, *, n_experts):
    i = pl.program_id(0)
    _, rows, hw = obuf.shape
    tm = x_ref.shape[0]
    slot = i % 2

    def drain(s, n):
        _wait_chunks(lambda m: pltpu.make_async_copy(obuf.at[s, pl.ds(0, m)], o_hbm.at[pl.ds(0, m)], sem.at[s]), n)

    @pl.when(i >= 2)
    def _():
        drain(slot, tot_ref[jnp.maximum(i - 2, 0)])

    loct = loct_ref[...]
    x = x_ref[...]
    x_lo, x_hi = x[:, :hw], x[:, hw:]
    for r0 in range(0, rows, DISPATCH_ROWS):
        rowi = r0 + lax.broadcasted_iota(jnp.int32, (DISPATCH_ROWS, tm), 0)
        p = jnp.zeros((DISPATCH_ROWS, tm), F32)
        for k in range(TOPK):
            p = jnp.where(rowi == loct[k:k + 1, :], 1.0, p)
        pb = p.astype(BF16)
        lo = jnp.dot(pb, x_lo, preferred_element_type=F32)
        hi = jnp.dot(pb, x_hi, preferred_element_type=F32)
        obuf[slot, r0:r0 + DISPATCH_ROWS, :] = ((lax.bitcast_convert_type(lo, jnp.uint32) >> 16)
                                                | lax.bitcast_convert_type(hi, jnp.uint32))

    def send(c, priority):
        src = pl.multiple_of(buf_rows_ref[c], RUN_ALIGN)
        dst = pl.multiple_of(hbm_rows_ref[c], RUN_ALIGN)
        pltpu.make_async_copy(obuf.at[slot, pl.ds(src, RUN_ALIGN)], o_hbm.at[pl.ds(dst, RUN_ALIGN)],
                              sem.at[slot]).start(priority=priority)

    def send_pair(p, carry):
        send(2 * p, 0)
        send(2 * p + 1, 1)
        return carry

    n_send = tot_ref[i]
    lax.fori_loop(0, n_send // 2, send_pair, 0)

    @pl.when(n_send % 2 == 1)
    def _():
        send(n_send - 1, 0)

    @pl.when(i == pl.num_programs(0) - 1)
    def _():
        drain(slot, tot_ref[i])

        @pl.when(i >= 1)
        def _():
            drain(1 - slot, tot_ref[jnp.maximum(i - 1, 0)])

        zbuf[...] = jnp.zeros(zbuf.shape, zbuf.dtype)

        def zero_copy(dst_row):
            return pltpu.make_async_copy(zbuf, o_hbm.at[pl.ds(pl.multiple_of(dst_row, RUN_ALIGN), RUN_ALIGN)], zsem)

        def zero_expert(e, c):
            return lax.fori_loop(0, zc_ref[e], lambda j, c2: (zero_copy(zs_ref[e] + j * RUN_ALIGN).start(), c2)[1], c)

        def zero_wait(e, c):
            return lax.fori_loop(0, zc_ref[e], lambda j, c2: (zero_copy(0).wait(), c2)[1], c)

        lax.fori_loop(0, n_experts, zero_expert, 0)
        lax.fori_loop(0, n_experts, zero_wait, 0)


def _dispatch(xb, loc_t, tile_chunks, chunk_buf, chunk_hbm, zero_start, zero_chunks, n_rows, n_experts):
    m, d = xb.shape
    tm = CMB_TILE
    buf_rows = -(-(tm * TOPK + n_experts * (RUN_ALIGN - 1)) // DISPATCH_ROWS) * DISPATCH_ROWS
    lst = pl.BlockSpec((CHUNK_LIST,), lambda i, *_: (i,), memory_space=pltpu.SMEM)
    grid_spec = pltpu.PrefetchScalarGridSpec(
        num_scalar_prefetch=3,
        grid=(m // tm,),
        in_specs=[lst, lst, pl.BlockSpec((tm, d), lambda i, *_: (i, 0)),
                  pl.BlockSpec((None, TOPK, tm), lambda i, *_: (i, 0, 0))],
        out_specs=pl.BlockSpec(memory_space=pl.ANY),
        scratch_shapes=[pltpu.VMEM((2, buf_rows, d // 2), jnp.uint32), pltpu.VMEM((RUN_ALIGN, d // 2), jnp.uint32),
                        pltpu.SemaphoreType.DMA((2,)), pltpu.SemaphoreType.DMA(())],
    )
    return pl.pallas_call(
        functools.partial(_dispatch_kernel, n_experts=n_experts),
        grid_spec=grid_spec,
        out_shape=jax.ShapeDtypeStruct((n_rows, d // 2), jnp.uint32),
        compiler_params=_cparams("arbitrary"),
        name="moe_dispatch",
    )(tile_chunks, zero_start, zero_chunks, chunk_buf, chunk_hbm, xb, loc_t)


def _moe_kernel(be_ref, nu_ref, x_ref, w1_ref, w3_ref, w2_ref, o_ref, w1b, w3b, w2b):
    i = pl.program_id(0)

    @pl.when(i < nu_ref[0])
    def _():
        prev = be_ref[jnp.maximum(i - 1, 0)]

        @pl.when(jnp.logical_or(i == 0, be_ref[i] != prev))
        def _():
            w1b[...] = w1_ref[...].astype(BF16)
            w3b[...] = w3_ref[...].astype(BF16)
            w2b[...] = w2_ref[...].astype(BF16)

        hk = x_ref.shape[1]
        sub = x_ref.shape[0] // MOE_SPLIT
        for s in range(MOE_SPLIT):
            xlo, xhi = _unpack_halves(x_ref[s * sub:(s + 1) * sub, :])
            a = (jnp.dot(xlo, w1b[0:hk, :], preferred_element_type=F32)
                 + jnp.dot(xhi, w1b[hk:2 * hk, :], preferred_element_type=F32))
            b = (jnp.dot(xlo, w3b[0:hk, :], preferred_element_type=F32)
                 + jnp.dot(xhi, w3b[hk:2 * hk, :], preferred_element_type=F32))
            h = (_silu(a) * b).astype(BF16)
            o_ref[s * sub:(s + 1) * sub, :] = _pack_halves(jnp.dot(h, w2b[...], preferred_element_type=F32))


def _moe_experts(x_sorted, w1, w3, w2, layer, block_expert, n_used):
    n_rows, hw = x_sorted.shape
    d, de = w1.shape[2], w1.shape[3]
    nb = n_rows // MOE_BLK
    blk = lambda i, be, nu: (jnp.minimum(i, nu[0] - 1), 0)
    wmap = lambda i, be, nu: (layer, be[jnp.minimum(i, nu[0] - 1)], 0, 0)
    grid_spec = pltpu.PrefetchScalarGridSpec(
        num_scalar_prefetch=2,
        grid=(nb,),
        in_specs=[pl.BlockSpec((MOE_BLK, hw), blk),
                  pl.BlockSpec((None, None, d, de), wmap), pl.BlockSpec((None, None, d, de), wmap),
                  pl.BlockSpec((None, None, de, d), wmap)],
        out_specs=pl.BlockSpec((MOE_BLK, d // 2), blk),
        scratch_shapes=[pltpu.VMEM((d, de), BF16), pltpu.VMEM((d, de), BF16), pltpu.VMEM((de, d), BF16)],
    )
    return pl.pallas_call(
        _moe_kernel,
        grid_spec=grid_spec,
        out_shape=jax.ShapeDtypeStruct((n_rows, d // 2), jnp.uint32),
        compiler_params=_cparams("arbitrary"),
        name="moe_experts",
    )(block_expert, n_used, x_sorted, w1, w3, w2)


def _route(logits, router_bias, valid, n_experts):
    t = logits.shape[0]
    per = n_experts // N_GROUPS
    scores = jax.nn.sigmoid(logits)
    biased = scores + router_bias.astype(F32)
    grp = biased.reshape(t, N_GROUPS, per)
    m1 = jnp.max(grp, -1, keepdims=True)
    is_max = grp == m1
    first = is_max & (jnp.cumsum(is_max.astype(jnp.int32), -1) == 1)
    m2 = jnp.max(jnp.where(first, -jnp.inf, grp), -1)
    grp_score = m1[..., 0] + m2

    def rank(v):
        n = v.shape[-1]
        idx = jnp.arange(n)
        vi, vj = v[:, :, None], v[:, None, :]
        beaten = (vj > vi) | ((vj == vi) & (idx[None, None, :] < idx[None, :, None]))
        return jnp.sum(beaten.astype(jnp.int32), -1)

    gsel = rank(grp_score) < TOPK_GROUPS
    emask = jnp.repeat(gsel, per, axis=-1)
    e_rank = rank(jnp.where(emask, biased, NEG_INF))
    slot = e_rank[:, None, :] == jnp.arange(TOPK)[None, :, None]
    eidx = jnp.sum(jnp.where(slot, jnp.arange(n_experts)[None, None, :], 0), -1)
    g = jnp.sum(jnp.where(slot, scores[:, None, :], 0.0), -1)
    g = g / (jnp.sum(g, -1, keepdims=True) + 1e-20) * ROUTED_SCALE
    g = jnp.where(valid[:, None], g, 0.0)
    member = ((e_rank < TOPK) & valid[:, None]).astype(jnp.int32)
    return eidx.astype(jnp.int32), g, member


def _dispatch_plan(eidx, member, valid, n_experts, n_rows):
    t = eidx.shape[0]
    nt = t // CMB_TILE
    i32 = jnp.int32
    m3 = member.reshape(nt, CMB_TILE, n_experts)
    cnt = jnp.sum(m3, axis=1)
    rlen = (cnt + RUN_ALIGN - 1) // RUN_ALIGN * RUN_ALIGN
    tri = (jnp.arange(CMB_TILE)[:, None] > jnp.arange(CMB_TILE)[None, :]).astype(BF16)
    rank = jnp.einsum("ts,nse->nte", tri, m3.astype(BF16), preferred_element_type=F32)
    rank = rank.astype(i32).reshape(t, n_experts)
    seg = jnp.sum(rlen, axis=0)
    padded = (seg + MOE_BLK - 1) // MOE_BLK * MOE_BLK
    pad_end = jnp.cumsum(padded)
    pad_start = pad_end - padded
    run_off = jnp.cumsum(rlen, axis=0) - rlen
    run_start = pad_start[None, :] + run_off
    lbase = jnp.cumsum(rlen, axis=1) - rlen
    pos = (jnp.repeat(lbase, CMB_TILE, axis=0) + rank)
    hit = eidx[:, :, None] == jnp.arange(n_experts, dtype=i32)[None, None, :]
    loc = jnp.sum(jnp.where(hit, pos[:, None, :], 0), axis=-1)
    loc = jnp.where(valid[:, None], loc, -1).astype(i32)
    loc_t = loc.reshape(nt, CMB_TILE, TOPK).transpose(0, 2, 1)
    nb = n_rows // MOE_BLK
    blk_row = jnp.arange(nb, dtype=i32) * MOE_BLK
    block_expert = jnp.minimum(jnp.sum((pad_end[None, :] <= blk_row[:, None]).astype(i32), axis=1),
                               n_experts - 1).astype(i32)
    n_used = jnp.maximum(pad_end[-1] // MOE_BLK, 1).astype(i32).reshape(1)

    n_chunks = (rlen // RUN_ALIGN).astype(i32)
    cum = jnp.cumsum(n_chunks, axis=1)
    c = jnp.arange(CHUNK_LIST, dtype=i32)
    e_c = jnp.minimum(jnp.sum((cum[:, None, :] <= c[None, :, None]).astype(i32), axis=-1), n_experts - 1)
    own = e_c[:, :, None] == jnp.arange(n_experts, dtype=i32)[None, None, :]
    pick = lambda tab: jnp.sum(jnp.where(own, tab[:, None, :], 0), axis=-1)
    off = (c[None, :] - pick(cum - n_chunks)) * RUN_ALIGN
    chunk_buf = (pick(lbase) + off).reshape(-1).astype(i32)
    chunk_hbm = (pick(run_start) + off).reshape(-1).astype(i32)
    return (loc, loc_t, cum[:, -1].astype(i32), chunk_buf, chunk_hbm, (pad_start + seg).astype(i32),
            ((padded - seg) // RUN_ALIGN).astype(i32), block_expert, n_used)


def _split_bf16(w):
    hi = w.astype(BF16)
    lo = (w - hi.astype(F32)).astype(BF16)
    return hi, lo


def kernel(x_prompt, x_sample, cache_meta_k, cache_meta_v, cache_win_k, cache_win_v, state_conv, state_h, meta_tokens, ln_g, ln_b, attn_w_in, attn_w_out, attn_sink, lru_w_in, lru_conv_w, lru_conv_b, lru_gate_a_w, lru_gate_a_b, lru_gate_x_w, lru_gate_x_b, lru_lambda, lru_w_out, moe_router_w, moe_router_bias, moe_w1, moe_w3, moe_w2, moe_shared_w1, moe_shared_w3, moe_shared_w2):
    bsz, seq, d = x_prompt.shape
    dec_b, dec_s, _ = x_sample.shape
    n_meta = meta_tokens.shape[0]
    depth = ln_g.shape[0]
    n_experts = moe_router_w.shape[2]
    past_len = PAST_LEN
    n_heads = d // HEAD_DIM
    n_kv = n_heads // GROUP
    q_w, kv_w = n_heads * HEAD_DIM, n_kv * HEAD_DIM
    c_rnn = lru_w_in.shape[2] // 2
    alpha = (2 * depth) ** 0.25

    t_real = bsz * seq
    s_base, s_rows = t_real, dec_b * dec_s
    m_base = s_base + s_rows
    t_pad = m_base + bsz * META_BLK
    assert seq % LRU_TILE == 0 and seq % QBLK == 0 and s_rows % QBLK == 0 and t_pad % ROW_TILE == 0
    assert n_meta <= META_BLK and dec_s <= QBLK and q_w == d and kv_w % LANES == 0
    assert (CMB_TILE * TOPK + n_experts * (RUN_ALIGN - 1)) // RUN_ALIGN <= CHUNK_LIST and t_pad % CMB_TILE == 0
    assert cache_win_k.shape[2] == WINDOW and WINDOW == QBLK and n_meta + WINDOW + dec_s <= 3 * QBLK
    nqb = seq // QBLK

    pos = np.zeros((t_pad,), np.float32)
    pos[:t_real] = np.tile(np.arange(seq) + n_meta, bsz)
    pos[s_base:m_base] = np.tile(past_len + n_meta + np.arange(dec_s), dec_b)
    valid_np = np.zeros((t_pad,), bool)
    valid_np[:m_base] = True
    for b in range(bsz):
        pos[m_base + b * META_BLK:m_base + b * META_BLK + n_meta] = np.arange(n_meta)
        valid_np[m_base + b * META_BLK:m_base + b * META_BLK + n_meta] = True
    valid = jnp.asarray(valid_np)

    half = ROT_DIM // 2
    freqs = ROPE_THETA ** (-jnp.arange(0, ROT_DIM, 2, dtype=F32) / ROT_DIM)
    ang = jnp.asarray(pos)[:, None] * freqs[None, :]
    cos, sin = jnp.cos(ang), jnp.sin(ang)
    ones = jnp.ones((t_pad, HEAD_DIM - ROT_DIM), F32)
    zeros_h = jnp.zeros((t_pad, half), F32)
    zeros_r = jnp.zeros((t_pad, HEAD_DIM - ROT_DIM), F32)
    reps = LANES // HEAD_DIM
    cos_t = jnp.tile(jnp.concatenate([cos, cos, ones], 1), (1, reps))
    sin_lo_t = jnp.tile(jnp.concatenate([zeros_h, sin, zeros_r], 1), (1, reps))
    sin_hi_t = jnp.tile(jnp.concatenate([-sin, zeros_h, zeros_r], 1), (1, reps))

    kj = np.arange(3 * QBLK)[:, None]
    qc = np.arange(QBLK)[None, :] // CHUNK
    key_chunk = np.where(kj < 2 * QBLK, kj // CHUNK - 2, 0)
    band_ok = (kj < 2 * QBLK) & (qc - key_chunk >= 0) & (qc - key_chunk <= WINDOW // CHUNK)
    meta_ok = (kj >= 2 * QBLK) & (kj < 2 * QBLK + n_meta)
    later = band_ok | meta_ok
    first = (band_ok & (kj >= QBLK)) | meta_ok
    only_meta = np.broadcast_to(meta_ok, later.shape)
    bias_p = jnp.asarray(np.where(np.stack([first, later, only_meta]), 0.0, NEG_INF).astype(np.float32))
    n_keys_s = n_meta + WINDOW + dec_s
    bias_s = jnp.asarray(np.where(np.broadcast_to(kj < n_keys_s, later.shape), 0.0, NEG_INF)
                         .astype(np.float32))[None]

    meta_rows = jnp.concatenate([meta_tokens.astype(F32), jnp.zeros((META_BLK - n_meta, d), F32)], 0)
    x = jnp.concatenate([x_prompt.reshape(t_real, d), x_sample.reshape(s_rows, d),
                         jnp.tile(meta_rows, (bsz, 1))], 0)
    xb = x.astype(BF16)

    n_assign = (t_real + s_rows + bsz * n_meta) * TOPK
    n_runs = (t_pad // CMB_TILE) * n_experts
    n_rows = -(-(n_assign + n_runs * (RUN_ALIGN - 1) + n_experts * (MOE_BLK - 1)) // MOE_BLK) * MOE_BLK

    outs = {k: [] for k in ("mk", "mv", "wk", "wv", "ks", "vs", "cp", "hp", "cs", "hs")}
    for l in range(depth):
        idx = l // 2
        if l % 2 == 0:
            w_in = attn_w_in[idx].astype(BF16)
            qkv = _qkv_proj(xb, w_in, cos_t, sin_lo_t, sin_hi_t, q_w + kv_w)
            k_col, v_col = q_w // kv_w, q_w // kv_w + 1
            sink_rows = jnp.repeat(attn_sink[idx].astype(F32).reshape(n_kv, 1, GROUP), QBLK, axis=2)

            q_s = qkv[s_base:m_base, :q_w].reshape(dec_b, dec_s, q_w)
            q_s = jnp.pad(q_s, ((0, 0), (0, QBLK - dec_s), (0, 0))).reshape(dec_b * QBLK, q_w)
            k_new = qkv[s_base:m_base, q_w:q_w + kv_w].reshape(dec_b, dec_s, kv_w)
            v_new = qkv[s_base:m_base, q_w + kv_w:].reshape(dec_b, dec_s, kv_w)
            pad_k = jnp.zeros((dec_b, 3 * QBLK - n_keys_s, kv_w), F32)
            k_s = jnp.concatenate([cache_meta_k[idx].reshape(dec_b, n_meta, kv_w).astype(F32),
                                   cache_win_k[idx].reshape(dec_b, WINDOW, kv_w).astype(F32), k_new, pad_k], 1)
            v_s = jnp.concatenate([cache_meta_v[idx].reshape(dec_b, n_meta, kv_w).astype(F32),
                                   cache_win_v[idx].reshape(dec_b, WINDOW, kv_w).astype(F32), v_new, pad_k], 1)
            kv_s = jnp.concatenate([k_s, v_s], axis=2).reshape(dec_b * 3 * QBLK, 2 * kv_w)
            o_s = _attention(q_s, kv_s, dec_b, lambda i: i,
                             [lambda i: 3 * i, lambda i: 3 * i + 1, lambda i: 3 * i + 2], bias_s, lambda i: 0,
                             sink_rows, jnp.zeros((dec_b * QBLK, q_w), BF16), lambda i: i, q_w, kv_w, 0, 1,
                             "attn_sample")
            o_s = o_s.reshape(dec_b, QBLK, q_w)[:, :dec_s].reshape(s_rows, q_w)

            steps = nqb + 1
            is_meta = lambda i: (i % steps) == nqb
            bidx = lambda i: i // steps
            pblk = lambda i: i % steps
            frame_blk = lambda i: bidx(i) * nqb + jnp.minimum(pblk(i), nqb - 1)
            meta_blk = lambda i: m_base // QBLK + bidx(i)
            q_map = lambda i: jnp.where(is_meta(i), meta_blk(i), frame_blk(i))
            prev_map = lambda i: bidx(i) * nqb + jnp.clip(pblk(i) - 1, 0, nqb - 1)
            bias_map = lambda i: jnp.where(is_meta(i), 2, jnp.minimum(pblk(i), 1))
            o_init = jnp.concatenate([jnp.zeros((t_real, q_w), BF16), o_s,
                                      jnp.zeros((t_pad - m_base, q_w), BF16)], 0)
            o = _attention(qkv, qkv, bsz * steps, q_map, [prev_map, frame_blk, meta_blk], bias_p, bias_map,
                           sink_rows, o_init, q_map, q_w, kv_w, k_col, v_col, "attn_prompt")

            kp = qkv[:t_real, q_w:q_w + kv_w].reshape(bsz, seq, n_kv, HEAD_DIM)
            vp = qkv[:t_real, q_w + kv_w:].reshape(bsz, seq, n_kv, HEAD_DIM)
            km = qkv[m_base:, q_w:q_w + kv_w].reshape(bsz, META_BLK, n_kv, HEAD_DIM)[:, :n_meta]
            vm = qkv[m_base:, q_w + kv_w:].reshape(bsz, META_BLK, n_kv, HEAD_DIM)[:, :n_meta]
            outs["mk"].append(km); outs["mv"].append(vm)
            outs["wk"].append(kp[:, -WINDOW:]); outs["wv"].append(vp[:, -WINDOW:])
            outs["ks"].append(k_new.reshape(dec_b, dec_s, n_kv, HEAD_DIM))
            outs["vs"].append(v_new.reshape(dec_b, dec_s, n_kv, HEAD_DIM))
            w_out = attn_w_out[idx].astype(BF16)
        else:
            u = _matmul(xb, lru_w_in[idx].astype(BF16), 2048, "lru_in_proj")
            sp = jax.nn.softplus(-lru_lambda[idx].astype(F32)).reshape(1, c_rnn)
            wts = (lru_conv_w[idx].astype(F32), lru_conv_b[idx].astype(F32).reshape(1, c_rnn),
                   lru_gate_a_w[idx].astype(BF16), lru_gate_a_b[idx].astype(F32).reshape(1, c_rnn),
                   lru_gate_x_w[idx].astype(BF16), lru_gate_x_b[idx].astype(F32).reshape(1, c_rnn), sp)
            zc = jnp.zeros((bsz, CONV_W - 1, c_rnn), F32)
            zh = jnp.zeros((bsz, 1, c_rnn), F32)
            y_m, c_m, h_m = _lru_seq(u, bsz, 1, META_BLK, m_base, META_BLK, n_meta,
                                     jnp.zeros((bsz * META_BLK, c_rnn), BF16), 0, zc, zh, wts)
            y_s, c_s, h_s = _lru_seq(u, dec_b, 1, dec_s, s_base, dec_s, dec_s,
                                     jnp.zeros((s_rows, c_rnn), BF16), 0,
                                     state_conv[idx].astype(F32), state_h[idx].astype(F32).reshape(dec_b, 1, c_rnn),
                                     wts)
            o_init = jnp.concatenate([jnp.zeros((t_real, c_rnn), BF16), y_s, y_m], 0)
            o, c_p, h_p = _lru_seq(u, bsz, seq // LRU_TILE, LRU_TILE, 0, seq, LRU_TILE, o_init, 0,
                                   c_m, h_m, wts)
            outs["cp"].append(c_p); outs["hp"].append(h_p.reshape(bsz, c_rnn))
            outs["cs"].append(c_s); outs["hs"].append(h_s.reshape(dec_b, c_rnn))
            w_out = lru_w_out[idx].astype(BF16)

        ne_pad = -(-n_experts // LANES) * LANES
        wr = jnp.pad(moe_router_w[l].astype(F32), ((0, 0), (0, ne_pad - n_experts)))
        wr_hi, wr_lo = _split_bf16(wr)
        x1, x1b, logits = _proj_ln(o, w_out, x, ln_g[l, 0].astype(F32).reshape(1, d),
                                   ln_b[l, 0].astype(F32).reshape(1, d), wr_hi, wr_lo, alpha)

        eidx, gate, member = _route(logits[:, :n_experts], moe_router_bias[l], valid, n_experts)
        (loc, loc_t, tile_chunks, chunk_buf, chunk_hbm, zero_start, zero_chunks, block_expert,
         n_used) = _dispatch_plan(eidx, member, valid, n_experts, n_rows)
        x_sorted = _dispatch(x1b, loc_t, tile_chunks, chunk_buf, chunk_hbm, zero_start, zero_chunks,
                             n_rows, n_experts)
        y_sorted = _moe_experts(x_sorted, moe_w1, moe_w3, moe_w2, l, block_expert, n_used)

        w13 = jnp.concatenate([moe_shared_w1[l], moe_shared_w3[l]], axis=1).astype(BF16)
        x, xb = _combine_ln(x1b, x1, loc, gate, y_sorted, tile_chunks, chunk_buf, chunk_hbm, w13,
                            moe_shared_w2[l].astype(BF16), ln_g[l, 1].astype(F32).reshape(1, d),
                            ln_b[l, 1].astype(F32).reshape(1, d), alpha, n_experts)

    y_prompt = x[:t_real].reshape(bsz, seq, d)
    y_sample = x[s_base:m_base].reshape(dec_b, dec_s, d)
    st = lambda k: jnp.stack(outs[k])
    return (y_prompt, y_sample, st("mk"), st("mv"), st("wk"), st("wv"), st("ks"), st("vs"),
            st("cp"), st("hp"), st("cs"), st("hs"))
```
